```python
import math
import jax, jax.numpy as jnp
from jax import lax
import numpy as np

D_MODEL = 1024
BATCH = 4
SEQ = 8192
DEPTH = 4

HEAD_DIM = 64
MIX_WIDTH = D_MODEL
SWA_HEADS = 4
SWA_KV_HEADS = 2
SWA_WINDOW = 128
SWA_BLOCK = 128
NSA_HEADS = 4
NSA_CMP_BLOCK = 32
NSA_CMP_STRIDE = 16
NSA_CMP_HIDDEN = 128
NSA_SEL_BLOCK = 64
NSA_TOP_N = 16
NSA_WINDOW = 512
NSA_Q_BLOCK = 128
NSA_KV_STREAMS = 6
NSA_BRANCHES = 3
SWA_WIDTH = SWA_HEADS * HEAD_DIM
SWA_KV_WIDTH = SWA_KV_HEADS * HEAD_DIM
NSA_WIDTH = NSA_HEADS * HEAD_DIM
SSM_WIDTH = MIX_WIDTH - SWA_WIDTH - NSA_WIDTH
SSM_GROUP_CH = 16
SSM_GROUPS = SSM_WIDTH // SSM_GROUP_CH
SSM_STATE = 64
DT_MIN = 1e-3
DT_MAX = 1e-1
NUM_BUCKETS = 32
BUCKET_EXACT = NUM_BUCKETS // 2
BUCKET_MAX_DIST = 1024
N_BIAS_HEADS = SWA_HEADS + NSA_HEADS
D_FF = 4 * D_MODEL
EPS = 1e-6
NEG_INF = -1e30
SEL_FORCE = 1e4
OFF_QA = 0
OFF_KA = OFF_QA + SWA_WIDTH
OFF_VA = OFF_KA + SWA_KV_WIDTH
OFF_U = OFF_VA + SWA_KV_WIDTH
OFF_QC = OFF_U + SSM_WIDTH
OFF_KVC = OFF_QC + NSA_WIDTH
OFF_GC = OFF_KVC + NSA_KV_STREAMS * HEAD_DIM
IN_WIDTH = OFF_GC + NSA_BRANCHES * NSA_HEADS

kernel_name = "hybrid_swa_s5_nsa_trunk"


def rms_norm(x, g):
    xf = x.astype(jnp.float32)
    y = xf * lax.rsqrt(jnp.mean(xf * xf, axis=-1, keepdims=True) + EPS)
    return (y * g.astype(jnp.float32)).astype(x.dtype)


def rel_bucket(dist):
    n = jnp.maximum(dist, 0)
    nf = jnp.maximum(n, 1).astype(jnp.float32)
    large = BUCKET_EXACT + (jnp.log(nf / BUCKET_EXACT) / math.log(BUCKET_MAX_DIST / BUCKET_EXACT)
                            * (NUM_BUCKETS - BUCKET_EXACT)).astype(jnp.int32)
    return jnp.where(n < BUCKET_EXACT, n, jnp.minimum(large, NUM_BUCKETS - 1))


def masked_softmax(logits, valid):
    p = jax.nn.softmax(jnp.where(valid, logits, NEG_INF), axis=-1)
    return jnp.where(valid, p, 0.0)


def swa_sink_mixer(q, k, v, sinks, bias_tbl):
    Bsz, L, H, d = q.shape
    KV = k.shape[2]
    G = H // KV
    nb = L // SWA_BLOCK
    pad = ((0, 0), (SWA_BLOCK, 0), (0, 0), (0, 0))
    kp = jnp.pad(k, pad).reshape(Bsz, nb + 1, SWA_BLOCK, KV, d)
    vp = jnp.pad(v, pad).reshape(Bsz, nb + 1, SWA_BLOCK, KV, d)
    kb = jnp.concatenate([kp[:, :-1], kp[:, 1:]], axis=2)
    vb = jnp.concatenate([vp[:, :-1], vp[:, 1:]], axis=2)
    qb = q.reshape(Bsz, nb, SWA_BLOCK, KV, G, d)
    lg = jnp.einsum('bnqkgd,bnskd->bnkgqs', qb, kb, preferred_element_type=jnp.float32) * (d ** -0.5)
    i = jnp.arange(SWA_BLOCK)[:, None]
    j = jnp.arange(2 * SWA_BLOCK)[None, :]
    dist = i - j + SWA_BLOCK
    bias = bias_tbl.astype(jnp.float32)[rel_bucket(dist)]
    bias = bias.transpose(2, 0, 1).reshape(KV, G, SWA_BLOCK, 2 * SWA_BLOCK)
    s_pos = jnp.arange(nb)[:, None] * SWA_BLOCK - SWA_BLOCK + j
    valid = ((dist >= 0) & (dist < SWA_WINDOW))[None] & (s_pos >= 0)[:, None, :]
    valid = valid[None, :, None, None]
    lg = jnp.where(valid, lg + bias, NEG_INF)
    sink = sinks.astype(jnp.float32).reshape(KV, G, 1, 1)
    m = jnp.maximum(jnp.max(lg, axis=-1, keepdims=True), sink)
    p = jnp.exp(lg - m)
    denom = jnp.sum(p, axis=-1, keepdims=True) + jnp.exp(sink - m)
    o = jnp.einsum('bnkgqs,bnskd->bnqkgd', p / denom, vb.astype(jnp.float32))
    return o.reshape(Bsz, L, H * d).astype(q.dtype)


def _scan_combine(left, right):
    a_l, b_l = left
    a_r, b_r = right
    return a_r * a_l, a_r * b_l + b_r


def s5_mixer(u, a_re, a_im, log_dt, b_re, b_im, c_re, c_im, d_skip, glu_w, glu_b):
    Bsz, L, W = u.shape
    f32 = jnp.float32
    uf = u.astype(f32).reshape(Bsz, L, SSM_GROUPS, SSM_GROUP_CH)
    A = lax.complex(a_re.astype(f32), a_im.astype(f32))
    dt = jnp.exp(log_dt.astype(f32))[:, None]
    A_bar = jnp.exp(A * dt)
    Bm = lax.complex(b_re.astype(f32), b_im.astype(f32))
    B_bar = ((A_bar - 1.0) / A)[..., None] * Bm
    Bu = jnp.einsum('blgp,gnp->blgn', uf.astype(jnp.complex64), B_bar)
    a_seq = jnp.broadcast_to(A_bar, (1, L) + A_bar.shape)
    _, states = lax.associative_scan(_scan_combine, (a_seq, Bu), axis=1)
    Cm = lax.complex(c_re.astype(f32), c_im.astype(f32))
    y = jnp.einsum('blgn,gpn->blgp', states, Cm).real
    y = y + d_skip.astype(f32).reshape(SSM_GROUPS, SSM_GROUP_CH) * uf
    z = jax.nn.gelu(y.reshape(Bsz, L, W))
    ab = z @ glu_w.astype(f32) + glu_b.astype(f32)
    out = ab[..., :W] * jax.nn.sigmoid(ab[..., W:])
    return out.astype(u.dtype)


def nsa_compress(tok, pos, w1, w2):
    Bsz, L, d = tok.shape
    ch = tok.reshape(Bsz, L // NSA_CMP_STRIDE, NSA_CMP_STRIDE, d)
    blocks = jnp.concatenate([ch[:, :-1], ch[:, 1:]], axis=2)
    M = blocks.shape[1]
    blocks = (blocks + pos).reshape(Bsz, M, NSA_CMP_BLOCK * d)
    return jax.nn.gelu(blocks @ w1) @ w2


def nsa_mixer(q, kv, gates, cmp_pos, cmp_w1, cmp_w2, g_kc, g_ks, g_kw, bias_tbl):
    Bsz, L, HC, d = q.shape
    f32 = jnp.float32
    k_cmp = rms_norm(nsa_compress(kv[:, :, 0], cmp_pos[0], cmp_w1[0], cmp_w2[0]), g_kc)
    v_cmp = nsa_compress(kv[:, :, 1], cmp_pos[1], cmp_w1[1], cmp_w2[1]).astype(f32)
    NS = L // NSA_SEL_BLOCK
    n_top = min(NSA_TOP_N, NS)
    k_sel = rms_norm(kv[:, :, 2], g_ks).reshape(Bsz, NS, NSA_SEL_BLOCK, d)
    v_sel = kv[:, :, 3].reshape(Bsz, NS, NSA_SEL_BLOCK, d)
    wpad = ((0, 0), (NSA_WINDOW, 0), (0, 0))
    k_win = jnp.pad(rms_norm(kv[:, :, 4], g_kw), wpad)
    v_win = jnp.pad(kv[:, :, 5], wpad)
    M = k_cmp.shape[1]
    cmp_start = jnp.arange(M) * NSA_CMP_STRIDE
    cmp_end = cmp_start + NSA_CMP_BLOCK - 1
    sel_start = jnp.arange(NS) * NSA_SEL_BLOCK
    overlap = ((cmp_start[:, None] < sel_start[None, :] + NSA_SEL_BLOCK)
               & (cmp_start[:, None] + NSA_CMP_BLOCK > sel_start[None, :])).astype(f32)
    bias_c = bias_tbl.astype(f32)
    bidx = jnp.arange(Bsz)[:, None, None]
    scale = d ** -0.5
    QB = NSA_Q_BLOCK

    def block(bi):
        t = bi * QB + jnp.arange(QB)
        qb = lax.dynamic_slice_in_dim(q, bi * QB, QB, axis=1)
        gb = lax.dynamic_slice_in_dim(gates, bi * QB, QB, axis=1).astype(f32)
        dist_c = t[:, None] - cmp_end[None, :]
        lg_c = jnp.einsum('bqhd,bmd->bhqm', qb, k_cmp, preferred_element_type=f32) * scale
        lg_c = lg_c + bias_c[rel_bucket(dist_c)].transpose(2, 0, 1)
        p_cmp = masked_softmax(lg_c, dist_c >= 0)
        o_cmp = jnp.einsum('bhqm,bmd->bqhd', p_cmp, v_cmp)
        imp = jnp.einsum('bhqm,mj->bqj', p_cmp, overlap)
        cur = t // NSA_SEL_BLOCK
        blk = jnp.arange(NS)[None, :]
        forced = (blk == 0) | (blk == cur[:, None]) | (blk == cur[:, None] - 1)
        future = sel_start[None, :] > t[:, None]
        score = jnp.where(forced, SEL_FORCE, jnp.where(future, -SEL_FORCE, imp))
        _, idx = lax.top_k(score, n_top)
        ks = k_sel[bidx, idx]
        vs = v_sel[bidx, idx].astype(f32)
        s_pos = idx[..., None] * NSA_SEL_BLOCK + jnp.arange(NSA_SEL_BLOCK)
        dist_s = t[None, :, None, None] - s_pos
        lg_s = jnp.einsum('bqhd,bqnsd->bqhns', qb, ks, preferred_element_type=f32) * scale
        lg_s = lg_s + jnp.moveaxis(bias_c[rel_bucket(dist_s)], -1, 2)
        shp = lg_s.shape
        valid_s = (dist_s >= 0).reshape(Bsz, QB, 1, -1)
        p_s = masked_softmax(lg_s.reshape(shp[:3] + (-1,)), valid_s).reshape(shp)
        o_sel = jnp.einsum('bqhns,bqnsd->bqhd', p_s, vs)
        kw = lax.dynamic_slice_in_dim(k_win, bi * QB, QB + NSA_WINDOW, axis=1)
        vw = lax.dynamic_slice_in_dim(v_win, bi * QB, QB + NSA_WINDOW, axis=1).astype(f32)
        w_pos = bi * QB - NSA_WINDOW + jnp.arange(QB + NSA_WINDOW)
        dist_w = t[:, None] - w_pos[None, :]
        valid_w = (dist_w >= 0) & (dist_w < NSA_WINDOW) & (w_pos[None, :] >= 0)
        lg_w = jnp.einsum('bqhd,bsd->bhqs', qb, kw, preferred_element_type=f32) * scale
        lg_w = lg_w + bias_c[rel_bucket(dist_w)].transpose(2, 0, 1)
        p_w = masked_softmax(lg_w, valid_w)
        o_win = jnp.einsum('bhqs,bsd->bqhd', p_w, vw)
        o = gb[..., 0:1] * o_cmp + gb[..., 1:2] * o_sel + gb[..., 2:3] * o_win
        return o.reshape(Bsz, QB, HC * d).astype(q.dtype)

    out = lax.map(block, jnp.arange(L // QB))
    return out.transpose(1, 0, 2, 3).reshape(Bsz, L, HC * d)


def setup_inputs(seed: int = 0) -> dict:
    key = jax.random.key(seed)
    ks = jax.random.split(key, 32)
    f32 = jnp.float32

    def nrm(k, shape, scale):
        return jax.random.normal(k, shape, f32) * scale

    G, N, P = SSM_GROUPS, SSM_STATE, SSM_GROUP_CH
    return {
        'x': nrm(ks[0], (BATCH, SEQ, D_MODEL), 1.0),
        'norm1_g': 1.0 + nrm(ks[1], (DEPTH, D_MODEL), 0.02),
        'w_in': nrm(ks[2], (DEPTH, D_MODEL, IN_WIDTH), D_MODEL ** -0.5),
        'qk_g': 1.0 + nrm(ks[3], (DEPTH, 6, HEAD_DIM), 0.02),
        'sinks': nrm(ks[4], (DEPTH, SWA_HEADS), 0.5),
        'rel_bias': nrm(ks[5], (NUM_BUCKETS, N_BIAS_HEADS), 0.2),
        'ssm_a_re': -0.5 * jnp.exp(nrm(ks[6], (DEPTH, G, N), 0.05)),
        'ssm_a_im': math.pi * jnp.arange(N, dtype=f32)[None, None, :] + nrm(ks[7], (DEPTH, G, N), 0.01),
        'ssm_log_dt': jax.random.uniform(ks[8], (DEPTH, G), f32, math.log(DT_MIN), math.log(DT_MAX)),
        'ssm_b_re': nrm(ks[9], (DEPTH, G, N, P), (2 * P) ** -0.5),
        'ssm_b_im': nrm(ks[10], (DEPTH, G, N, P), (2 * P) ** -0.5),
        'ssm_c_re': nrm(ks[11], (DEPTH, G, P, N), N ** -0.5),
        'ssm_c_im': nrm(ks[12], (DEPTH, G, P, N), N ** -0.5),
        'ssm_d': nrm(ks[13], (DEPTH, SSM_WIDTH), 0.5),
        'glu_w': nrm(ks[14], (DEPTH, SSM_WIDTH, 2 * SSM_WIDTH), SSM_WIDTH ** -0.5),
        'glu_b': nrm(ks[15], (DEPTH, 2 * SSM_WIDTH), 0.02),
        'cmp_pos': nrm(ks[16], (DEPTH, 2, NSA_CMP_BLOCK, HEAD_DIM), 0.1),
        'cmp_w1': nrm(ks[17], (DEPTH, 2, NSA_CMP_BLOCK * HEAD_DIM, NSA_CMP_HIDDEN), (NSA_CMP_BLOCK * HEAD_DIM) ** -0.5),
        'cmp_w2': nrm(ks[18], (DEPTH, 2, NSA_CMP_HIDDEN, HEAD_DIM), NSA_CMP_HIDDEN ** -0.5),
        'out_norm_g': 1.0 + nrm(ks[19], (DEPTH, MIX_WIDTH), 0.02),
        'w_out': nrm(ks[20], (DEPTH, MIX_WIDTH, D_MODEL), 0.5 * MIX_WIDTH ** -0.5),
        'norm2_g': 1.0 + nrm(ks[21], (DEPTH, D_MODEL), 0.02),
        'w_up': nrm(ks[22], (DEPTH, D_MODEL, D_FF), D_MODEL ** -0.5),
        'w_down': nrm(ks[23], (DEPTH, D_FF, D_MODEL), 0.5 * D_FF ** -0.5),
    }


def reference(x, norm1_g, w_in, qk_g, sinks, rel_bias, ssm_a_re, ssm_a_im, ssm_log_dt,
              ssm_b_re, ssm_b_im, ssm_c_re, ssm_c_im, ssm_d, glu_w, glu_b,
              cmp_pos, cmp_w1, cmp_w2, out_norm_g, w_out, norm2_g, w_up, w_down):
    Bsz, L, _ = x.shape
    for l in range(DEPTH):
        h = rms_norm(x, norm1_g[l])
        proj = jnp.einsum('bld,de->ble', h, w_in[l])
        qa = rms_norm(proj[..., OFF_QA:OFF_KA].reshape(Bsz, L, SWA_HEADS, HEAD_DIM), qk_g[l, 0])
        ka = rms_norm(proj[..., OFF_KA:OFF_VA].reshape(Bsz, L, SWA_KV_HEADS, HEAD_DIM), qk_g[l, 1])
        va = proj[..., OFF_VA:OFF_U].reshape(Bsz, L, SWA_KV_HEADS, HEAD_DIM)
        o_a = swa_sink_mixer(qa, ka, va, sinks[l], rel_bias[:, :SWA_HEADS])
        o_b = s5_mixer(proj[..., OFF_U:OFF_QC], ssm_a_re[l], ssm_a_im[l], ssm_log_dt[l],
                       ssm_b_re[l], ssm_b_im[l], ssm_c_re[l], ssm_c_im[l], ssm_d[l], glu_w[l], glu_b[l])
        qc = rms_norm(proj[..., OFF_QC:OFF_KVC].reshape(Bsz, L, NSA_HEADS, HEAD_DIM), qk_g[l, 2])
        kvc = proj[..., OFF_KVC:OFF_GC].reshape(Bsz, L, NSA_KV_STREAMS, HEAD_DIM)
        gc = jax.nn.sigmoid(proj[..., OFF_GC:IN_WIDTH].reshape(Bsz, L, NSA_HEADS, NSA_BRANCHES))
        o_c = nsa_mixer(qc, kvc, gc, cmp_pos[l], cmp_w1[l], cmp_w2[l],
                        qk_g[l, 3], qk_g[l, 4], qk_g[l, 5], rel_bias[:, SWA_HEADS:])
        g_out = out_norm_g[l]
        mixed = jnp.concatenate([
            rms_norm(o_a, g_out[:SWA_WIDTH]),
            rms_norm(o_b, g_out[SWA_WIDTH:SWA_WIDTH + SSM_WIDTH]),
            rms_norm(o_c, g_out[SWA_WIDTH + SSM_WIDTH:]),
        ], axis=-1)
        x = x + jnp.einsum('ble,ed->bld', mixed, w_out[l])
        h2 = rms_norm(x, norm2_g[l])
        hid = jax.nn.relu(jnp.einsum('bld,df->blf', h2, w_up[l]))
        x = x + jnp.einsum('blf,fd->bld', hid * hid, w_down[l])
    return x
```

```python
import functools
import math

import jax
import jax.numpy as jnp
from jax import lax
from jax.experimental import pallas as pl
from jax.experimental.pallas import tpu as pltpu

F32 = jnp.float32
BF16 = jnp.bfloat16

HEAD_DIM = 64
SWA_HEADS = 4
SWA_KV_HEADS = 2
SWA_WINDOW = 128
NSA_HEADS = 4
NSA_CMP_BLOCK = 32
NSA_CMP_STRIDE = 16
NSA_SEL_BLOCK = 64
NSA_TOP_N = 16
NSA_WINDOW = 512
NSA_BRANCHES = 3
SSM_GROUP_CH = 16
SSM_STATE = 64
NUM_BUCKETS = 32
BUCKET_EXACT = NUM_BUCKETS // 2
BUCKET_MAX_DIST = 1024
EPS = 1e-6
NEG_INF = -1e30
SEL_FORCE = 1e4
SEL_PENALTY = -30000.0

LANES = 128
QB = 128
SSM_CHUNK = 16
VMEM_LIMIT = 56 * 1024 * 1024

SWA_WIDTH = SWA_HEADS * HEAD_DIM
SWA_KV_WIDTH = SWA_KV_HEADS * HEAD_DIM
NSA_WIDTH = NSA_HEADS * HEAD_DIM


def _cparams(*sem):
    return pltpu.CompilerParams(dimension_semantics=sem, vmem_limit_bytes=VMEM_LIMIT)


def _dot(a, b):
    return jnp.dot(a, b, preferred_element_type=F32)


def _dot_nt(a, b):
    return lax.dot_general(a, b, (((1,), (1,)), ((), ())), preferred_element_type=F32)


def _dot_split(a, b):
    hi = a.astype(BF16)
    lo = (a - hi.astype(F32)).astype(BF16)
    return _dot(hi, b) + _dot(lo, b)


def _lane_lo(rows):
    return lax.broadcasted_iota(jnp.int32, (rows, LANES), 1) < HEAD_DIM


def _stack_heads(q):
    lo = _lane_lo(q.shape[0])
    zero = jnp.zeros_like(q[:, :LANES])
    g0 = q[:, :LANES]
    g1 = q[:, LANES:]
    return jnp.concatenate([jnp.where(lo, g0, zero), jnp.where(lo, zero, g0),
                            jnp.where(lo, g1, zero), jnp.where(lo, zero, g1)], axis=0)


def _unstack_heads(o):
    n = o.shape[0] // 4
    lo = _lane_lo(n)
    return jnp.concatenate([jnp.where(lo, o[0:n], o[n:2 * n]),
                            jnp.where(lo, o[2 * n:3 * n], o[3 * n:4 * n])], axis=1)


def _seg_rms(p, seg, gain):
    ms = _dot_split(p * p, seg)
    return p * lax.rsqrt(ms + EPS) * gain


def _in_proj_kernel(x_ref, g_ref, w_ref, qkg_ref, seg_ref,
                    qa_ref, ka_ref, va_ref, u_ref, qc_ref, kvc_ref,
                    ks_ref, vs_ref, kw_ref, vw_ref, gc_ref):
    x = x_ref[...]
    ms = jnp.mean(x * x, axis=-1, keepdims=True)
    h = (x * lax.rsqrt(ms + EPS) * g_ref[...]).astype(BF16)
    seg = seg_ref[...]
    scale = HEAD_DIM ** -0.5

    def proj(a, b):
        return _dot(h, w_ref[:, a:b])

    qa_ref[...] = (_seg_rms(proj(0, 256), seg, qkg_ref[0:1, :]) * scale).astype(BF16)
    ka_ref[...] = _seg_rms(proj(256, 384), seg[:LANES, :LANES], qkg_ref[1:2, :LANES]).astype(BF16)
    va_ref[...] = proj(384, 512).astype(BF16)
    u_ref[...] = proj(512, 1024)
    qc_ref[...] = (_seg_rms(proj(1024, 1280), seg, qkg_ref[2:3, :]) * scale).astype(BF16)
    kvc_ref[...] = proj(1280, 1408)

    def dup_rms(p, gain):
        m = jnp.mean(p * p, axis=-1, keepdims=True)
        return p * lax.rsqrt(m + EPS) * gain

    ks_ref[...] = dup_rms(proj(1408, 1536), qkg_ref[3:4, :LANES]).astype(BF16)
    vs_ref[...] = proj(1536, 1664).astype(BF16)
    kw_ref[...] = dup_rms(proj(1664, 1792), qkg_ref[4:5, :LANES]).astype(BF16)
    vw_ref[...] = proj(1792, 1920).astype(BF16)
    gc_ref[...] = proj(1920, 2048)


def _in_proj(x2, g1, w, qkg, seg, tm):
    n, d = x2.shape
    widths = [(256, BF16), (128, BF16), (128, BF16), (512, F32), (256, BF16), (128, F32),
              (128, BF16), (128, BF16), (128, BF16), (128, BF16), (128, F32)]
    full = lambda a: pl.BlockSpec(a.shape, lambda i: (0,) * a.ndim)
    return pl.pallas_call(
        _in_proj_kernel,
        grid=(n // tm,),
        in_specs=[pl.BlockSpec((tm, d), lambda i: (i, 0)), full(g1), full(w), full(qkg), full(seg)],
        out_specs=[pl.BlockSpec((tm, wd), lambda i: (i, 0)) for wd, _ in widths],
        out_shape=[jax.ShapeDtypeStruct((n, wd), dt) for wd, dt in widths],
        compiler_params=_cparams("parallel"),
        name="in_proj",
    )(x2, g1, w, qkg, seg)


SWA_TQ = 512


def _swa_kernel(q_ref, kc_ref, kp_ref, vc_ref, vp_ref, bias_ref, sink_ref, o_ref):
    first = pl.program_id(1) == 0
    pen = jnp.where(first, NEG_INF, 0.0).astype(F32)
    sink = sink_ref[...]
    bias = bias_ref[...]
    for s in range(SWA_TQ // QB):
        r0, r1 = s * QB, (s + 1) * QB
        qs = _stack_heads(q_ref[r0:r1, :])
        k_cur = kc_ref[r0:r1, :]
        v_cur = vc_ref[r0:r1, :]
        if s == 0:
            k_prev, v_prev = kp_ref[...], vp_ref[...]
        else:
            k_prev, v_prev = kc_ref[r0 - QB:r0, :], vc_ref[r0 - QB:r0, :]
        lg_p = _dot_nt(qs, k_prev)
        if s == 0:
            lg_p = lg_p + pen
        lg = jnp.concatenate([lg_p, _dot_nt(qs, k_cur)], axis=1) + bias
        m = jnp.maximum(jnp.max(lg, axis=-1, keepdims=True), sink)
        p = jnp.exp(lg - m)
        denom = jnp.sum(p, axis=-1, keepdims=True) + jnp.exp(sink - m)
        pv = _dot(p[:, :QB].astype(BF16), v_prev) + _dot(p[:, QB:].astype(BF16), v_cur)
        o_ref[r0:r1, :] = _unstack_heads(pv / denom)


def _swa(qa, ka, va, bias, sink_rows, bsz, seq):
    n = qa.shape[0]
    nq = seq // SWA_TQ
    per = SWA_TQ // QB
    cur = lambda b, i: (b * nq + i, 0)
    prev = lambda b, i: (jnp.maximum((b * nq + i) * per - 1, 0), 0)
    const = lambda b, i: (0, 0)
    return pl.pallas_call(
        _swa_kernel,
        grid=(bsz, nq),
        in_specs=[pl.BlockSpec((SWA_TQ, 256), cur),
                  pl.BlockSpec((SWA_TQ, LANES), cur), pl.BlockSpec((QB, LANES), prev),
                  pl.BlockSpec((SWA_TQ, LANES), cur), pl.BlockSpec((QB, LANES), prev),
                  pl.BlockSpec(bias.shape, const), pl.BlockSpec(sink_rows.shape, const)],
        out_specs=pl.BlockSpec((SWA_TQ, 256), cur),
        out_shape=jax.ShapeDtypeStruct((n, 256), F32),
        compiler_params=_cparams("parallel", "parallel"),
        name="swa",
    )(qa, ka, ka, va, va, bias, sink_rows)


def _ssm_z_kernel(v_ref, pz_ref, zre_ref, zim_ref):
    z = _dot(v_ref[0].astype(BF16), pz_ref[0])
    zre_ref[...] = z[:, :LANES]
    zim_ref[...] = z[:, LANES:]


def _ssm_z(v, pz, tn):
    npair, nc, w = v.shape
    return pl.pallas_call(
        _ssm_z_kernel,
        grid=(npair, nc // tn),
        in_specs=[pl.BlockSpec((1, tn, w), lambda j, r: (j, r, 0)),
                  pl.BlockSpec((1, w, 2 * LANES), lambda j, r: (j, 0, 0))],
        out_specs=[pl.BlockSpec((tn, LANES), lambda j, r: (r, j))] * 2,
        out_shape=[jax.ShapeDtypeStruct((nc, npair * LANES), F32)] * 2,
        compiler_params=_cparams("parallel", "parallel"),
        name="ssm_chunk_state",
    )(v, pz)


def _ssm_scan_kernel(zre_ref, zim_ref, ar_ref, ai_ref, sre_ref, sim_ref):
    a_r = ar_ref[...]
    a_i = ai_ref[...]

    def body(c, carry):
        s_r, s_i = carry
        sre_ref[c] = s_r
        sim_ref[c] = s_i
        return (a_r * s_r - a_i * s_i + zre_ref[c], a_r * s_i + a_i * s_r + zim_ref[c])

    zero = jnp.zeros(a_r.shape, F32)
    lax.fori_loop(0, zre_ref.shape[0], body, (zero, zero))


def _ssm_scan(zre, zim, a_r, a_i, tl):
    nchunk, rows, width = zre.shape
    blk = pl.BlockSpec((nchunk, rows, tl), lambda j: (0, 0, j))
    coef = pl.BlockSpec((rows, tl), lambda j: (0, j))
    return pl.pallas_call(
        _ssm_scan_kernel,
        grid=(width // tl,),
        in_specs=[blk, blk, coef, coef],
        out_specs=[blk, blk],
        out_shape=[jax.ShapeDtypeStruct(zre.shape, F32)] * 2,
        compiler_params=_cparams("parallel"),
        name="ssm_scan",
    )(zre, zim, a_r, a_i)


def _ssm_y_kernel(v_ref, sre_ref, sim_ref, m_ref, cre_ref, cim_ref, d_ref, y_ref):
    v = v_ref[0]
    y = _dot(v.astype(BF16), m_ref[0])
    y = y + _dot(sre_ref[...].astype(BF16), cre_ref[0]) + _dot(sim_ref[...].astype(BF16), cim_ref[0])
    y_ref[0] = y + d_ref[0] * v


def _ssm_y(v, sre, sim, mm, cre, cim, dvec, tn):
    npair, nc, w = v.shape
    per_pair = lambda shape: pl.BlockSpec((1,) + shape, lambda j, r: (j, 0, 0))
    state = pl.BlockSpec((tn, LANES), lambda j, r: (r, j))
    return pl.pallas_call(
        _ssm_y_kernel,
        grid=(npair, nc // tn),
        in_specs=[pl.BlockSpec((1, tn, w), lambda j, r: (j, r, 0)), state, state,
                  per_pair((w, w)), per_pair((LANES, w)), per_pair((LANES, w)), per_pair((1, w))],
        out_specs=pl.BlockSpec((1, tn, w), lambda j, r: (j, r, 0)),
        out_shape=jax.ShapeDtypeStruct(v.shape, F32),
        compiler_params=_cparams("parallel", "parallel"),
        name="ssm_output",
    )(v, sre, sim, mm, cre, cim, dvec)


def _ssm_operators(a_re, a_im, log_dt, b_re, b_im, c_re, c_im, d_skip):
    g, n = a_re.shape
    p = SSM_GROUP_CH
    t = SSM_CHUNK
    a = lax.complex(a_re.astype(F32), a_im.astype(F32))
    adt = a * jnp.exp(log_dt.astype(F32))[:, None]
    b_bar = ((jnp.exp(adt) - 1.0) / a)[..., None] * lax.complex(b_re.astype(F32), b_im.astype(F32))
    cm = lax.complex(c_re.astype(F32), c_im.astype(F32))
    pw = jnp.exp(adt[None] * jnp.arange(t + 1, dtype=F32)[:, None, None].astype(jnp.complex64))
    kern = jnp.einsum('gpn,tgn,gnq->tgpq', cm, pw[:t], b_bar).real
    lag = jnp.arange(t)[None, :] - jnp.arange(t)[:, None]
    m_op = jnp.where((lag >= 0)[None, :, None, :, None],
                     kern[jnp.maximum(lag, 0)].transpose(2, 0, 4, 1, 3), 0.0)
    m_op = m_op.reshape(g, t * p, t * p)
    bz = pw[:t][::-1][:, :, :, None] * b_bar[None]
    bz = bz.transpose(1, 0, 3, 2).reshape(g, t * p, n)
    cz = (cm[None] * pw[1:, :, None, :]).transpose(1, 3, 0, 2).reshape(g, n, t * p)
    a_chunk = pw[t]

    def pair_diag(x):
        r, c = x.shape[1:]
        x = x.reshape(g // 2, 2, r, c)
        z = jnp.zeros_like(x[:, 0])
        return jnp.concatenate([jnp.concatenate([x[:, 0], z], axis=2),
                                jnp.concatenate([z, x[:, 1]], axis=2)], axis=1)

    m_pair = pair_diag(m_op).astype(BF16)
    pz = jnp.concatenate([pair_diag(bz.real), pair_diag(bz.imag)], axis=2).astype(BF16)
    c_re_pair = pair_diag(cz.real).astype(BF16)
    c_im_pair = pair_diag(-cz.imag).astype(BF16)
    dvec = jnp.broadcast_to(d_skip.astype(F32).reshape(g // 2, 2, 1, p), (g // 2, 2, t, p)).reshape(g // 2, 1, 2 * t * p)
    return m_pair, pz, c_re_pair, c_im_pair, dvec, a_chunk.real.reshape(-1), a_chunk.imag.reshape(-1)


def _s5(u, ops, bsz, seq):
    m_pair, pz, c_re_pair, c_im_pair, dvec, a_r, a_i = ops
    npair = m_pair.shape[0]
    t, p = SSM_CHUNK, SSM_GROUP_CH
    nch = seq // t
    nc = bsz * nch
    v = u.reshape(bsz, nch, t, npair, 2, p).transpose(3, 1, 0, 4, 2, 5).reshape(npair, nc, 2 * t * p)
    tn = min(nc, 1024)
    zre, zim = _ssm_z(v, pz, tn)
    width = zre.shape[1]
    rows = 2 * bsz
    half = width // 2
    shape3 = (nch, rows, half)
    coef = lambda c: jnp.tile(c.reshape(2, half), (bsz, 1))
    sre, sim = _ssm_scan(zre.reshape(shape3), zim.reshape(shape3), coef(a_r), coef(a_i), 256)
    y = _ssm_y(v, sre.reshape(nc, width), sim.reshape(nc, width), m_pair, c_re_pair, c_im_pair, dvec, tn)
    return y.reshape(npair, nch, bsz, 2, t, p).transpose(2, 1, 4, 0, 3, 5).reshape(bsz * seq, npair * 2 * p)


def _cmp_kernel(ck_ref, cv_ref, pos_ref, w1_ref, w2_ref, gk_ref, kk_ref, vv_ref):
    rows = ck_ref.shape[0]

    def mlp(ch, s):
        top = _dot((ch + pos_ref[s, 0:1, :]).astype(BF16), w1_ref[s, 0])
        bot = _dot((ch + pos_ref[s, 1:2, :]).astype(BF16), w1_ref[s, 1])
        hid = top + pltpu.roll(bot, rows - 1, 0)
        return _dot(jax.nn.gelu(hid).astype(BF16), w2_ref[s])

    k = mlp(ck_ref[...], 0)
    ms = jnp.mean(k * k, axis=-1, keepdims=True)
    kk_ref[0] = (k * lax.rsqrt(ms + EPS) * gk_ref[...]).astype(BF16)
    vv_ref[0] = mlp(cv_ref[...], 1).astype(BF16)


def _nsa_compress(ck, cv, pos, w1, w2, gk, bsz):
    rows = ck.shape[0] // bsz
    full = lambda a: pl.BlockSpec(a.shape, lambda b: (0,) * a.ndim)
    tok = pl.BlockSpec((rows, ck.shape[1]), lambda b: (b, 0))
    out = pl.BlockSpec((1, rows, LANES), lambda b: (b, 0, 0))
    return pl.pallas_call(
        _cmp_kernel,
        grid=(bsz,),
        in_specs=[tok, tok, full(pos), full(w1), full(w2), full(gk)],
        out_specs=[out, out],
        out_shape=[jax.ShapeDtypeStruct((bsz, rows, LANES), BF16)] * 2,
        compiler_params=_cparams("parallel"),
        name="nsa_compress",
    )(ck, cv, pos, w1, w2, gk)


def _gate_lanes(gc, gexp):
    return _dot_split(jax.nn.sigmoid(gc), gexp)


def _nsa_cmp_kernel(n_sel_blocks, q_ref, kk_ref, vv_ref, bias_ref, gc_ref, ovl_ref, gexp_ref,
                    o_ref, nsel_ref):
    bi = pl.program_id(0)
    qs = _stack_heads(q_ref[...])
    lg = _dot_nt(qs, kk_ref[0]) + bias_ref[0]
    valid = lg > 0.5 * NEG_INF
    m = jnp.max(lg, axis=-1, keepdims=True)
    p = jnp.where(valid, jnp.exp(lg - m), 0.0)
    denom = jnp.sum(p, axis=-1, keepdims=True)
    p = p / jnp.where(denom > 0.0, denom, 1.0)
    o_cmp = _unstack_heads(_dot(p.astype(BF16), vv_ref[0]))
    gates = _gate_lanes(gc_ref[...], gexp_ref[...])
    o_ref[...] = gates[:, :NSA_WIDTH] * o_cmp
    p_sum = p[0:QB] + p[QB:2 * QB] + p[2 * QB:3 * QB] + p[3 * QB:4 * QB]
    imp = _dot_split(p_sum, ovl_ref[...])
    t = bi * QB + lax.broadcasted_iota(jnp.int32, (QB, LANES), 0)
    blk = lax.broadcasted_iota(jnp.int32, (QB, LANES), 1)
    cur = t // NSA_SEL_BLOCK
    forced = (blk == 0) | (blk == cur) | (blk == cur - 1)
    score = jnp.where(forced, SEL_FORCE, jnp.where(blk > cur, -SEL_FORCE, imp))
    score = jnp.where(blk < n_sel_blocks, score, -3e38)
    st = score.T
    jidx = lax.broadcasted_iota(jnp.int32, (LANES, QB), 0).astype(F32)
    unsel = jnp.ones((LANES, QB), F32)
    for _ in range(min(NSA_TOP_N, n_sel_blocks)):
        mx = jnp.max(st, axis=0, keepdims=True)
        first = jnp.min(jnp.where(st == mx, jidx, 1e9), axis=0, keepdims=True)
        pick = jidx == first
        unsel = jnp.where(pick, 0.0, unsel)
        st = jnp.where(pick, -3.4e38, st)
    nsel_ref[...] = unsel.T.astype(BF16)


def _nsa_cmp(qc, kk, vv, bias_c, gc, ovl, gexp, bsz, seq):
    n = qc.shape[0]
    nb = seq // QB
    m = kk.shape[1]
    tok = lambda w: pl.BlockSpec((QB, w), lambda bi, b: (b * nb + bi, 0))
    seqblk = pl.BlockSpec((1, m, LANES), lambda bi, b: (b, 0, 0))
    const = lambda a: pl.BlockSpec(a.shape, lambda bi, b: (0,) * a.ndim)
    return pl.pallas_call(
        functools.partial(_nsa_cmp_kernel, seq // NSA_SEL_BLOCK),
        grid=(nb, bsz),
        in_specs=[tok(256), seqblk, seqblk,
                  pl.BlockSpec((1, 4 * QB, m), lambda bi, b: (bi, 0, 0)),
                  tok(LANES), const(ovl), const(gexp)],
        out_specs=[tok(256), tok(LANES)],
        out_shape=[jax.ShapeDtypeStruct((n, 256), F32), jax.ShapeDtypeStruct((n, LANES), BF16)],
        compiler_params=_cparams("parallel", "parallel"),
        name="nsa_compressed_topn",
    )(qc, kk, vv, bias_c, gc, ovl, gexp)


def _flash_step(q, k, v, bias, carry):
    m, l, acc = carry
    lg = _dot_nt(q, k) + bias
    m_new = jnp.maximum(m, jnp.max(lg, axis=-1, keepdims=True))
    alpha = jnp.exp(m - m_new)
    p = jnp.exp(lg - m_new)
    l = alpha * l + jnp.sum(p, axis=-1, keepdims=True)
    acc = alpha * acc + _dot(p.astype(BF16), v)
    return m_new, l, acc


def _nsa_attn_kernel(q_ref, nsel_ref, ocmp_ref, gc_ref, ks_ref, e_ref, vs_ref, kw_ref, vw_ref,
                     tsel_ref, twin_ref, gexp_ref, o_ref):
    bi = pl.program_id(1)
    qs = _stack_heads(q_ref[...])
    nsel = nsel_ref[...]
    q_aug = jnp.concatenate([qs, jnp.concatenate([nsel] * NSA_HEADS, axis=0)], axis=1)
    init = (jnp.full((4 * QB, 1), NEG_INF, F32), jnp.zeros((4 * QB, 1), F32), jnp.zeros((4 * QB, LANES), F32))
    n_far = tsel_ref.shape[0] - 1

    def sel_body(kj, carry):
        r = pl.multiple_of(kj * QB, QB)
        k_aug = jnp.concatenate([ks_ref[pl.ds(r, QB), :], e_ref[pl.ds(r, QB), :]], axis=1)
        return _flash_step(q_aug, k_aug, vs_ref[pl.ds(r, QB), :], tsel_ref[jnp.minimum(bi - kj, n_far)], carry)

    _, l_s, acc_s = lax.fori_loop(0, bi + 1, sel_body, init)

    def win_body(dl, carry):
        r = pl.multiple_of((bi - dl) * QB, QB)
        return _flash_step(qs, kw_ref[pl.ds(r, QB), :], vw_ref[pl.ds(r, QB), :], twin_ref[dl], carry)

    _, l_w, acc_w = lax.fori_loop(0, jnp.minimum(bi, twin_ref.shape[0] - 1) + 1, win_body, init)

    gates = _gate_lanes(gc_ref[...], gexp_ref[...])
    o_sel = _unstack_heads(acc_s / l_s)
    o_win = _unstack_heads(acc_w / l_w)
    o_ref[...] = (ocmp_ref[...] + gates[:, NSA_WIDTH:2 * NSA_WIDTH] * o_sel
                  + gates[:, 2 * NSA_WIDTH:] * o_win)


def _nsa_attn(qc, nsel, ocmp, gc, ks, e_pen, vs, kw, vw, tsel, twin, gexp, bsz, seq):
    n = qc.shape[0]
    nb = seq // QB
    tok = lambda w: pl.BlockSpec((QB, w), lambda b, bi: (b * nb + bi, 0))
    seqblk = pl.BlockSpec((seq, LANES), lambda b, bi: (b, 0))
    const = lambda a: pl.BlockSpec(a.shape, lambda b, bi: (0,) * a.ndim)
    return pl.pallas_call(
        _nsa_attn_kernel,
        grid=(bsz, nb),
        in_specs=[tok(256), tok(LANES), tok(256), tok(LANES),
                  seqblk, const(e_pen), seqblk, seqblk, seqblk,
                  const(tsel), const(twin), const(gexp)],
        out_specs=tok(256),
        out_shape=jax.ShapeDtypeStruct((n, 256), F32),
        compiler_params=_cparams("parallel", "arbitrary"),
        name="nsa_selected_window",
    )(qc, nsel, ocmp, gc, ks, e_pen, vs, kw, vw, tsel, twin, gexp)


def _rms(x, gain):
    return x * lax.rsqrt(jnp.mean(x * x, axis=-1, keepdims=True) + EPS) * gain


def _out_kernel(x_ref, oa_ref, y_ref, oc_ref, gluw_ref, glub_ref, gout_ref, wout_ref, o_ref):
    wb = y_ref.shape[1]
    ab = _dot(jax.nn.gelu(y_ref[...]).astype(BF16), gluw_ref[...]) + glub_ref[...]
    ob = ab[:, :wb] * jax.nn.sigmoid(ab[:, wb:])
    a0, a1 = SWA_WIDTH, SWA_WIDTH + wb
    acc = x_ref[...]
    acc = acc + _dot(_rms(oa_ref[...], gout_ref[:, :a0]).astype(BF16), wout_ref[:a0, :])
    acc = acc + _dot(_rms(ob, gout_ref[:, a0:a1]).astype(BF16), wout_ref[a0:a1, :])
    acc = acc + _dot(_rms(oc_ref[...], gout_ref[:, a1:]).astype(BF16), wout_ref[a1:, :])
    o_ref[...] = acc


def _out_proj(x2, oa, y, oc, gluw, glub, gout, wout, tm):
    n, d = x2.shape
    tok = lambda w: pl.BlockSpec((tm, w), lambda i: (i, 0))
    full = lambda a: pl.BlockSpec(a.shape, lambda i: (0,) * a.ndim)
    return pl.pallas_call(
        _out_kernel,
        grid=(n // tm,),
        in_specs=[tok(d), tok(oa.shape[1]), tok(y.shape[1]), tok(oc.shape[1]),
                  full(gluw), full(glub), full(gout), full(wout)],
        out_specs=tok(d),
        out_shape=jax.ShapeDtypeStruct((n, d), F32),
        compiler_params=_cparams("parallel"),
        name="out_proj",
    )(x2, oa, y, oc, gluw, glub, gout, wout)


FFN_CHUNK = 1024


def _ffn_kernel(x_ref, g_ref, wu_ref, wd_ref, o_ref):
    x = x_ref[...]
    h = _rms(x, g_ref[...]).astype(BF16)
    acc = x
    for c in range(wu_ref.shape[1] // FFN_CHUNK):
        c0, c1 = c * FFN_CHUNK, (c + 1) * FFN_CHUNK
        hid = jnp.maximum(_dot(h, wu_ref[:, c0:c1]), 0.0)
        acc = acc + _dot((hid * hid).astype(BF16), wd_ref[c0:c1, :])
    o_ref[...] = acc


def _ffn(x2, g2, wu, wd, tm):
    n, d = x2.shape
    full = lambda a: pl.BlockSpec(a.shape, lambda i: (0,) * a.ndim)
    return pl.pallas_call(
        _ffn_kernel,
        grid=(n // tm,),
        in_specs=[pl.BlockSpec((tm, d), lambda i: (i, 0)), full(g2), full(wu), full(wd)],
        out_specs=pl.BlockSpec((tm, d), lambda i: (i, 0)),
        out_shape=jax.ShapeDtypeStruct((n, d), F32),
        compiler_params=_cparams("parallel"),
        name="ffn",
    )(x2, g2, wu, wd)


def _rel_bucket(dist):
    n = jnp.maximum(dist, 0)
    nf = jnp.maximum(n, 1).astype(F32)
    large = BUCKET_EXACT + (jnp.log(nf / BUCKET_EXACT) / math.log(BUCKET_MAX_DIST / BUCKET_EXACT)
                            * (NUM_BUCKETS - BUCKET_EXACT)).astype(jnp.int32)
    return jnp.where(n < BUCKET_EXACT, n, jnp.minimum(large, NUM_BUCKETS - 1))


def _bias_rows(tbl, dist, valid):
    b = jnp.moveaxis(tbl.astype(F32)[_rel_bucket(dist)], -1, -3)
    b = jnp.where(valid[..., None, :, :], b, NEG_INF)
    return b.reshape(b.shape[:-3] + (b.shape[-3] * b.shape[-2], b.shape[-1]))


def _bias_tables(rel_bias, seq):
    swa_order = jnp.array([0, 2, 1, 3])
    tbl_a = rel_bias[:, :SWA_HEADS][:, swa_order]
    tbl_c = rel_bias[:, SWA_HEADS:]
    i = jnp.arange(QB)[:, None]
    d_swa = i - jnp.arange(2 * QB)[None, :] + QB
    bias_swa = _bias_rows(tbl_a, d_swa, (d_swa >= 0) & (d_swa < SWA_WINDOW))
    nb = seq // QB
    m = seq // NSA_CMP_STRIDE
    t = (jnp.arange(nb)[:, None, None] * QB + i[None])
    d_cmp = t - (jnp.arange(m)[None, None, :] * NSA_CMP_STRIDE + NSA_CMP_BLOCK - 1)
    bias_cmp = _bias_rows(tbl_c, d_cmp, d_cmp >= 0)
    j = jnp.arange(QB)[None, :]
    n_far = -(-(BUCKET_MAX_DIST + QB) // QB)
    d_sel = jnp.arange(n_far)[:, None, None] * QB + (i - j)[None]
    tsel = _bias_rows(tbl_c, d_sel, d_sel >= 0)
    d_win = jnp.arange(NSA_WINDOW // QB + 1)[:, None, None] * QB + (i - j)[None]
    twin = _bias_rows(tbl_c, d_win, (d_win >= 0) & (d_win < NSA_WINDOW))
    return bias_swa, bias_cmp, tsel, twin


def _nsa_constants(seq):
    m = seq // NSA_CMP_STRIDE
    cs = jnp.arange(m)[:, None] * NSA_CMP_STRIDE
    ss = jnp.arange(LANES)[None, :] * NSA_SEL_BLOCK
    ovl = ((cs < ss + NSA_SEL_BLOCK) & (cs + NSA_CMP_BLOCK > ss) & (cs < seq - NSA_CMP_STRIDE)).astype(BF16)
    key_blk = jnp.arange(seq)[:, None] // NSA_SEL_BLOCK
    e_pen = jnp.where(key_blk == jnp.arange(LANES)[None, :], SEL_PENALTY, 0.0).astype(BF16)
    col = jnp.arange(LANES)[:, None]
    lane = jnp.arange(NSA_BRANCHES * NSA_WIDTH)[None, :]
    br, hd = lane // NSA_WIDTH, (lane % NSA_WIDTH) // HEAD_DIM
    gexp = (col == hd * NSA_BRANCHES + br).astype(BF16)
    return ovl, e_pen, gexp


def _seg_matrix():
    r = jnp.arange(2 * LANES)
    return jnp.where((r[:, None] // HEAD_DIM) == (r[None, :] // HEAD_DIM), 1.0 / HEAD_DIM, 0.0).astype(BF16)


def _prep_w_in(w):
    d = w.shape[0]
    hd = HEAD_DIM
    ssm_w = d - SWA_WIDTH - NSA_WIDTH
    o_ka = SWA_WIDTH
    o_va = o_ka + SWA_KV_WIDTH
    o_u = o_va + SWA_KV_WIDTH
    o_qc = o_u + ssm_w
    o_kv = o_qc + NSA_WIDTH
    o_gc = o_kv + 6 * hd
    head = lambda off, h: w[:, off + h * hd: off + (h + 1) * hd]
    kv = lambda s: w[:, o_kv + s * hd: o_kv + (s + 1) * hd]
    gates = w[:, o_gc:]
    cols = [head(0, 0), head(0, 2), head(0, 1), head(0, 3),
            w[:, o_ka:o_va], w[:, o_va:o_u], w[:, o_u:o_qc], w[:, o_qc:o_kv],
            kv(0), kv(1), kv(2), kv(2), kv(3), kv(3), kv(4), kv(4), kv(5), kv(5),
            gates, jnp.zeros((d, LANES - gates.shape[1]), w.dtype)]
    return jnp.concatenate(cols, axis=1).astype(BF16)


def _prep_qk_gains(qk_g):
    g = qk_g.astype(F32)
    t4 = lambda v: jnp.tile(v, 4)
    return jnp.stack([t4(g[0]), t4(g[1]), t4(g[2]), t4(g[4]), t4(g[5]),
                      t4(g[3]), t4(g[3]), t4(g[3])])


def kernel(x, norm1_g, w_in, qk_g, sinks, rel_bias, ssm_a_re, ssm_a_im, ssm_log_dt, ssm_b_re, ssm_b_im, ssm_c_re, ssm_c_im, ssm_d, glu_w, glu_b, cmp_pos, cmp_w1, cmp_w2, out_norm_g, w_out, norm2_g, w_up, w_down):
    bsz, seq, d = x.shape
    depth = w_in.shape[0]
    n = bsz * seq
    assert seq % SWA_TQ == 0 and seq // NSA_SEL_BLOCK <= LANES and d == 1024
    tm = 512
    ssm_w = d - SWA_WIDTH - NSA_WIDTH
    row = lambda v: v.astype(F32).reshape(1, -1)

    bias_swa, bias_cmp, tsel, twin = _bias_tables(rel_bias, seq)
    ovl, e_pen, gexp = _nsa_constants(seq)
    seg = _seg_matrix()
    swa_order = jnp.array([0, 2, 1, 3])
    swa_cols = (swa_order[:, None] * HEAD_DIM + jnp.arange(HEAD_DIM)[None, :]).reshape(-1)

    x2 = x.reshape(n, d)
    for l in range(depth):
        qkg = _prep_qk_gains(qk_g[l])
        (qa, ka, va, u, qc, kvc, ks, vs, kw, vw, gc) = _in_proj(
            x2, row(norm1_g[l]), _prep_w_in(w_in[l]), qkg, seg, tm)

        sink_rows = jnp.repeat(sinks[l].astype(F32)[swa_order], QB).reshape(-1, 1)
        o_a = _swa(qa, ka, va, bias_swa, sink_rows, bsz, seq)

        ops = _ssm_operators(ssm_a_re[l], ssm_a_im[l], ssm_log_dt[l], ssm_b_re[l], ssm_b_im[l],
                             ssm_c_re[l], ssm_c_im[l], ssm_d[l])
        y = _s5(u, ops, bsz, seq)

        nchunk = seq // NSA_CMP_STRIDE
        ck = kvc[:, :HEAD_DIM].reshape(bsz * nchunk, NSA_CMP_STRIDE * HEAD_DIM)
        cv = kvc[:, HEAD_DIM:].reshape(bsz * nchunk, NSA_CMP_STRIDE * HEAD_DIM)
        half = NSA_CMP_STRIDE * HEAD_DIM
        pos = cmp_pos[l].astype(F32).reshape(2, 2, half)
        w1 = cmp_w1[l].reshape(2, 2, half, -1).astype(BF16)
        w2 = jnp.concatenate([cmp_w2[l], cmp_w2[l]], axis=-1).astype(BF16)
        kk, vv = _nsa_compress(ck, cv, pos, w1, w2, qkg[5:6, :LANES], bsz)
        o_cmp, nsel = _nsa_cmp(qc, kk, vv, bias_cmp, gc, ovl, gexp, bsz, seq)
        o_c = _nsa_attn(qc, nsel, o_cmp, gc, ks, e_pen, vs, kw, vw, tsel, twin, gexp, bsz, seq)

        gout = out_norm_g[l].astype(F32)
        gout = jnp.concatenate([gout[:SWA_WIDTH][swa_cols], gout[SWA_WIDTH:]]).reshape(1, -1)
        wout = jnp.concatenate([w_out[l][:SWA_WIDTH][swa_cols], w_out[l][SWA_WIDTH:]], axis=0).astype(BF16)
        x2 = _out_proj(x2, o_a, y, o_c, glu_w[l].astype(BF16), row(glu_b[l]), gout, wout, tm)

        x2 = _ffn(x2, row(norm2_g[l]), w_up[l].astype(BF16), w_down[l].astype(BF16), tm)
    return x2.reshape(bsz, seq, d)
```

```python
import functools
import math

import jax
import jax.numpy as jnp
from jax import lax
from jax.experimental import pallas as pl
from jax.experimental.pallas import tpu as pltpu

F32 = jnp.float32
BF16 = jnp.bfloat16
HIGHEST = lax.Precision.HIGHEST

HEAD_DIM = 64
SWA_HEADS = 4
SWA_KV_HEADS = 2
SWA_WINDOW = 128
NSA_HEADS = 4
NSA_CMP_BLOCK = 32
NSA_CMP_STRIDE = 16
NSA_SEL_BLOCK = 64
NSA_TOP_N = 16
NSA_WINDOW = 512
NSA_BRANCHES = 3
SSM_GROUP_CH = 16
SSM_STATE = 64
NUM_BUCKETS = 32
BUCKET_EXACT = NUM_BUCKETS // 2
BUCKET_MAX_DIST = 1024
EPS = 1e-6
NEG_INF = -1e30
SEL_FORCE = 1e4
SEL_PENALTY = -30000.0

LANES = 128
QB = 128
SSM_CHUNK = 16
SSM_OCT = LANES // SSM_GROUP_CH
CMP_PHASES = LANES * NSA_CMP_STRIDE // QB
VMEM_LIMIT = 56 * 1024 * 1024

SWA_WIDTH = SWA_HEADS * HEAD_DIM
SWA_KV_WIDTH = SWA_KV_HEADS * HEAD_DIM
NSA_WIDTH = NSA_HEADS * HEAD_DIM


def _cparams(*sem):
    return pltpu.CompilerParams(dimension_semantics=sem, vmem_limit_bytes=VMEM_LIMIT)


def _dot(a, b):
    return jnp.dot(a, b, preferred_element_type=F32)


def _dot_nt(a, b):
    return lax.dot_general(a, b, (((1,), (1,)), ((), ())), preferred_element_type=F32)


def _dot_split(a, b):
    hi = a.astype(BF16)
    lo = (a - hi.astype(F32)).astype(BF16)
    return _dot(hi, b) + _dot(lo, b)


def _lane_lo(rows):
    return lax.broadcasted_iota(jnp.int32, (rows, LANES), 1) < HEAD_DIM


def _stack_heads(q):
    lo = _lane_lo(q.shape[0])
    zero = jnp.zeros_like(q[:, :LANES])
    g0 = q[:, :LANES]
    g1 = q[:, LANES:]
    return jnp.concatenate([jnp.where(lo, g0, zero), jnp.where(lo, zero, g0),
                            jnp.where(lo, g1, zero), jnp.where(lo, zero, g1)], axis=0)


def _unstack_heads(o):
    n = o.shape[0] // 4
    lo = _lane_lo(n)
    return jnp.concatenate([jnp.where(lo, o[0:n], o[n:2 * n]),
                            jnp.where(lo, o[2 * n:3 * n], o[3 * n:4 * n])], axis=1)


def _seg_rms(p, seg, gain):
    ms = _dot_split(p * p, seg)
    return p * lax.rsqrt(ms + EPS) * gain


def _in_proj_kernel(x_ref, g_ref, w_ref, qkg_ref, seg_ref,
                    qa_ref, ka_ref, va_ref, u_ref, qc_ref, kvc_ref,
                    ks_ref, vs_ref, kw_ref, vw_ref, gc_ref):
    x = x_ref[...]
    ms = jnp.mean(x * x, axis=-1, keepdims=True)
    h = (x * lax.rsqrt(ms + EPS) * g_ref[...]).astype(BF16)
    seg = seg_ref[...]
    scale = HEAD_DIM ** -0.5

    def proj(a, b):
        return _dot(h, w_ref[:, a:b])

    qa_ref[...] = (_seg_rms(proj(0, 256), seg, qkg_ref[0:1, :]) * scale).astype(BF16)
    ka_ref[...] = _seg_rms(proj(256, 384), seg[:LANES, :LANES], qkg_ref[1:2, :LANES]).astype(BF16)
    va_ref[...] = proj(384, 512).astype(BF16)
    u_ref[...] = proj(512, 1024)
    qc_ref[...] = (_seg_rms(proj(1024, 1280), seg, qkg_ref[2:3, :]) * scale).astype(BF16)
    kvc_ref[...] = proj(1280, 1408)

    def dup_rms(p, gain):
        m = jnp.mean(p * p, axis=-1, keepdims=True)
        return p * lax.rsqrt(m + EPS) * gain

    ks_ref[...] = dup_rms(proj(1408, 1536), qkg_ref[3:4, :LANES]).astype(BF16)
    vs_ref[...] = proj(1536, 1664).astype(BF16)
    kw_ref[...] = dup_rms(proj(1664, 1792), qkg_ref[4:5, :LANES]).astype(BF16)
    vw_ref[...] = proj(1792, 1920).astype(BF16)
    gc_ref[...] = proj(1920, 2048)


def _in_proj(x2, g1, w, qkg, seg, tm):
    n, d = x2.shape
    widths = [(256, BF16), (128, BF16), (128, BF16), (512, F32), (256, BF16), (128, F32),
              (128, BF16), (128, BF16), (128, BF16), (128, BF16), (128, F32)]
    full = lambda a: pl.BlockSpec(a.shape, lambda i: (0,) * a.ndim)
    return pl.pallas_call(
        _in_proj_kernel,
        grid=(n // tm,),
        in_specs=[pl.BlockSpec((tm, d), lambda i: (i, 0)), full(g1), full(w), full(qkg), full(seg)],
        out_specs=[pl.BlockSpec((tm, wd), lambda i: (i, 0)) for wd, _ in widths],
        out_shape=[jax.ShapeDtypeStruct((n, wd), dt) for wd, dt in widths],
        compiler_params=_cparams("parallel"),
        name="in_proj",
    )(x2, g1, w, qkg, seg)


SWA_TQ = 512


def _swa_kernel(q_ref, kc_ref, kp_ref, vc_ref, vp_ref, bias_ref, sink_ref, o_ref):
    first = pl.program_id(1) == 0
    pen = jnp.where(first, NEG_INF, 0.0).astype(F32)
    sink = sink_ref[...]
    bias = bias_ref[...]
    for s in range(SWA_TQ // QB):
        r0, r1 = s * QB, (s + 1) * QB
        qs = _stack_heads(q_ref[r0:r1, :])
        k_cur = kc_ref[r0:r1, :]
        v_cur = vc_ref[r0:r1, :]
        if s == 0:
            k_prev, v_prev = kp_ref[...], vp_ref[...]
        else:
            k_prev, v_prev = kc_ref[r0 - QB:r0, :], vc_ref[r0 - QB:r0, :]
        lg_p = _dot_nt(qs, k_prev)
        if s == 0:
            lg_p = lg_p + pen
        lg = jnp.concatenate([lg_p, _dot_nt(qs, k_cur)], axis=1) + bias
        m = jnp.maximum(jnp.max(lg, axis=-1, keepdims=True), sink)
        p = jnp.exp(lg - m)
        denom = jnp.sum(p, axis=-1, keepdims=True) + jnp.exp(sink - m)
        pv = _dot(p[:, :QB].astype(BF16), v_prev) + _dot(p[:, QB:].astype(BF16), v_cur)
        o_ref[r0:r1, :] = _unstack_heads(pv / denom)


def _swa(qa, ka, va, bias, sink_rows, bsz, seq):
    n = qa.shape[0]
    nq = seq // SWA_TQ
    per = SWA_TQ // QB
    cur = lambda b, i: (b * nq + i, 0)
    prev = lambda b, i: (jnp.maximum((b * nq + i) * per - 1, 0), 0)
    const = lambda b, i: (0, 0)
    return pl.pallas_call(
        _swa_kernel,
        grid=(bsz, nq),
        in_specs=[pl.BlockSpec((SWA_TQ, 256), cur),
                  pl.BlockSpec((SWA_TQ, LANES), cur), pl.BlockSpec((QB, LANES), prev),
                  pl.BlockSpec((SWA_TQ, LANES), cur), pl.BlockSpec((QB, LANES), prev),
                  pl.BlockSpec(bias.shape, const), pl.BlockSpec(sink_rows.shape, const)],
        out_specs=pl.BlockSpec((SWA_TQ, 256), cur),
        out_shape=jax.ShapeDtypeStruct((n, 256), F32),
        compiler_params=_cparams("parallel", "parallel"),
        name="swa",
    )(qa, ka, ka, va, va, bias, sink_rows)


def _chunk_rows(u_ref):
    return jnp.concatenate([u_ref[:, t, :] for t in range(SSM_CHUNK)], axis=1)


def _ssm_z_kernel(u_ref, pz_ref, zre_ref, zim_ref):
    z = _dot(_chunk_rows(u_ref).astype(BF16), pz_ref[0])
    half = z.shape[1] // 2
    zre_ref[...] = z[:, :half]
    zim_ref[...] = z[:, half:]


def _u_spec(tnc):
    return pl.BlockSpec((None, tnc, SSM_CHUNK, LANES), lambda j, b, r: (b, r, 0, j))


def _state_spec(tnc, sw, noct):
    return pl.BlockSpec((tnc, sw), lambda j, b, r: (r, b * noct + j))


def _ssm_z(u4, pz, tnc):
    bsz, nch = u4.shape[:2]
    noct = pz.shape[0]
    sw = pz.shape[2] // 2
    state = _state_spec(tnc, sw, noct)
    return pl.pallas_call(
        _ssm_z_kernel,
        grid=(noct, bsz, nch // tnc),
        in_specs=[_u_spec(tnc), pl.BlockSpec((1,) + pz.shape[1:], lambda j, b, r: (j, 0, 0))],
        out_specs=[state, state],
        out_shape=[jax.ShapeDtypeStruct((nch, bsz * noct * sw), F32)] * 2,
        compiler_params=_cparams("parallel", "parallel", "parallel"),
        name="ssm_chunk_state",
    )(u4, pz)


def _ssm_scan_kernel(zre_ref, zim_ref, ar_ref, ai_ref, sre_ref, sim_ref):
    a_r = ar_ref[...]
    a_i = ai_ref[...]

    def body(c, carry):
        s_r, s_i = carry
        row = pl.ds(c, 1)
        sre_ref[row, :] = s_r
        sim_ref[row, :] = s_i
        return (a_r * s_r - a_i * s_i + zre_ref[row, :], a_r * s_i + a_i * s_r + zim_ref[row, :])

    zero = jnp.zeros(a_r.shape, F32)
    lax.fori_loop(0, zre_ref.shape[0], body, (zero, zero))


def _ssm_scan(zre, zim, a_r, a_i, tl):
    nchunk, width = zre.shape
    blk = pl.BlockSpec((nchunk, tl), lambda j: (0, j))
    coef = pl.BlockSpec((1, tl), lambda j: (0, j))
    return pl.pallas_call(
        _ssm_scan_kernel,
        grid=(width // tl,),
        in_specs=[blk, blk, coef, coef],
        out_specs=[blk, blk],
        out_shape=[jax.ShapeDtypeStruct(zre.shape, F32)] * 2,
        compiler_params=_cparams("parallel"),
        name="ssm_scan",
    )(zre, zim, a_r, a_i)


def _ssm_y_kernel(u_ref, sre_ref, sim_ref, m_ref, cre_ref, cim_ref, d_ref, y_ref):
    v = _chunk_rows(u_ref)
    y = _dot(v.astype(BF16), m_ref[0])
    y = y + _dot(sre_ref[...].astype(BF16), cre_ref[0]) + _dot(sim_ref[...].astype(BF16), cim_ref[0])
    y = y + d_ref[0] * v
    for t in range(SSM_CHUNK):
        y_ref[:, t, :] = y[:, t * LANES:(t + 1) * LANES]


def _ssm_y(u4, sre, sim, mm, cre, cim, dvec, tnc):
    bsz, nch = u4.shape[:2]
    noct, sw = cre.shape[:2]
    per_oct = lambda a: pl.BlockSpec((1,) + a.shape[1:], lambda j, b, r: (j, 0, 0))
    state = _state_spec(tnc, sw, noct)
    return pl.pallas_call(
        _ssm_y_kernel,
        grid=(noct, bsz, nch // tnc),
        in_specs=[_u_spec(tnc), state, state, per_oct(mm), per_oct(cre), per_oct(cim), per_oct(dvec)],
        out_specs=_u_spec(tnc),
        out_shape=jax.ShapeDtypeStruct(u4.shape, F32),
        compiler_params=_cparams("parallel", "parallel", "parallel"),
        name="ssm_output",
    )(u4, sre, sim, mm, cre, cim, dvec)


def _ssm_operators(a_re, a_im, log_dt, b_re, b_im, c_re, c_im, d_skip):
    g, n = a_re.shape
    p = SSM_GROUP_CH
    t = SSM_CHUNK
    o = SSM_OCT
    noct = g // o
    a = lax.complex(a_re.astype(F32), a_im.astype(F32))
    adt = a * jnp.exp(log_dt.astype(F32))[:, None]
    b_bar = ((jnp.exp(adt) - 1.0) / a)[..., None] * lax.complex(b_re.astype(F32), b_im.astype(F32))
    cm = lax.complex(c_re.astype(F32), c_im.astype(F32))
    pw = jnp.exp(adt[None] * jnp.arange(t + 1, dtype=F32)[:, None, None].astype(jnp.complex64))
    kern = jnp.einsum('gpn,tgn,gnq->tgpq', cm, pw[:t], b_bar, precision=HIGHEST).real
    lag = jnp.arange(t)[None, :] - jnp.arange(t)[:, None]
    m_op = jnp.where((lag >= 0)[None, :, None, :, None],
                     kern[jnp.maximum(lag, 0)].transpose(2, 0, 4, 1, 3), 0.0)
    bz = (pw[:t][::-1][:, :, :, None] * b_bar[None]).transpose(1, 0, 3, 2)
    cz = (cm[None] * pw[1:, :, None, :]).transpose(1, 3, 0, 2)
    eye = jnp.eye(o, dtype=F32)
    oct_ = lambda x: x.reshape((noct, o) + x.shape[1:])
    m_oct = jnp.einsum('jasqtp,ab->jsaqtbp', oct_(m_op), eye).reshape(noct, t * o * p, t * o * p)
    pz_part = lambda x: jnp.einsum('jasqn,ab->jsaqbn', oct_(x), eye).reshape(noct, t * o * p, o * n)
    c_part = lambda x: jnp.einsum('jantp,ab->jantbp', oct_(x), eye).reshape(noct, o * n, t * o * p)
    pz = jnp.concatenate([pz_part(bz.real), pz_part(bz.imag)], axis=2)
    dvec = jnp.broadcast_to(d_skip.astype(F32).reshape(noct, 1, o, p), (noct, t, o, p)).reshape(noct, 1, t * o * p)
    a_chunk = pw[t].reshape(1, -1)
    return (m_oct.astype(BF16), pz.astype(BF16), c_part(cz.real).astype(BF16), c_part(-cz.imag).astype(BF16),
            dvec, a_chunk.real, a_chunk.imag)


def _s5(u, ops, bsz, seq):
    m_oct, pz, c_re, c_im, dvec, a_r, a_i = ops
    nch = seq // SSM_CHUNK
    u4 = u.reshape(bsz, nch, SSM_CHUNK, u.shape[1])
    tnc = min(nch, 256)
    zre, zim = _ssm_z(u4, pz, tnc)
    coef = lambda c: jnp.tile(c, (1, bsz))
    sre, sim = _ssm_scan(zre, zim, coef(a_r), coef(a_i), 2048)
    return _ssm_y(u4, sre, sim, m_oct, c_re, c_im, dvec, tnc).reshape(u.shape)


def _cmp_kernel(kvc_ref, pos_ref, w1t_ref, w1b_ref, w2_ref, gk_ref, kk_ref, vv_ref):
    rows = kvc_ref.shape[0] // NSA_CMP_STRIDE
    top = jnp.zeros((rows, w1t_ref.shape[2]), F32)
    bot = top
    for tau in range(NSA_CMP_STRIDE):
        tok = kvc_ref[pl.ds(tau, rows, stride=NSA_CMP_STRIDE), :]
        top = top + _dot((tok + pos_ref[tau:tau + 1, :]).astype(BF16), w1t_ref[tau])
        bot = bot + _dot((tok + pos_ref[NSA_CMP_STRIDE + tau:NSA_CMP_STRIDE + tau + 1, :]).astype(BF16), w1b_ref[tau])
    hid = top + pltpu.roll(bot, rows - 1, 0)
    out = _dot(jax.nn.gelu(hid).astype(BF16), w2_ref[...])
    k = out[:, :LANES]
    ms = jnp.mean(k * k, axis=-1, keepdims=True)
    kk_ref[0] = (k * lax.rsqrt(ms + EPS) * gk_ref[...]).astype(BF16)
    vv_ref[0] = out[:, LANES:].astype(BF16)


def _nsa_compress(kvc, pos, w1t, w1b, w2, gk, bsz, seq):
    rows = seq // NSA_CMP_STRIDE
    full = lambda a: pl.BlockSpec(a.shape, lambda b: (0,) * a.ndim)
    out = pl.BlockSpec((1, rows, LANES), lambda b: (b, 0, 0))
    return pl.pallas_call(
        _cmp_kernel,
        grid=(bsz,),
        in_specs=[pl.BlockSpec((seq, LANES), lambda b: (b, 0)), full(pos), full(w1t), full(w1b), full(w2), full(gk)],
        out_specs=[out, out],
        out_shape=[jax.ShapeDtypeStruct((bsz, rows, LANES), BF16)] * 2,
        compiler_params=_cparams("parallel"),
        name="nsa_compress",
    )(kvc, pos, w1t, w1b, w2, gk)


def _prep_compress(cmp_pos, cmp_w1, cmp_w2):
    eye = jnp.eye(2, dtype=F32)
    pos = cmp_pos.astype(F32).transpose(1, 0, 2).reshape(NSA_CMP_BLOCK, 2 * HEAD_DIM)
    w1 = cmp_w1.astype(F32).reshape(2, NSA_CMP_BLOCK, HEAD_DIM, -1)
    w1 = jnp.einsum('spdh,sz->psdzh', w1, eye).reshape(NSA_CMP_BLOCK, 2 * HEAD_DIM, -1).astype(BF16)
    w2 = jnp.concatenate([cmp_w2, cmp_w2], axis=-1).astype(F32)
    w2 = jnp.einsum('shd,sz->shzd', w2, eye).reshape(2 * w2.shape[1], 2 * LANES).astype(BF16)
    return pos, w1[:NSA_CMP_STRIDE], w1[NSA_CMP_STRIDE:], w2


def _gate_lanes(gc, gexp):
    return _dot_split(jax.nn.sigmoid(gc), gexp)


def _nsa_cmp_kernel(n_sel_blocks, q_ref, kk_ref, vv_ref, band_ref, gc_ref, ovl_ref, gexp_ref,
                    o_ref, nsel_ref):
    bi = pl.program_id(0)
    qs = _stack_heads(q_ref[...])
    band = band_ref[bi % CMP_PHASES]
    diag = bi // CMP_PHASES
    far = band[:, 0:1]
    tiles = []
    for kt in range(kk_ref.shape[1] // LANES):
        bias = jnp.where(kt == diag, band[:, LANES:],
                         jnp.where(kt == diag - 1, band[:, :LANES], jnp.where(kt < diag, far, NEG_INF)))
        tiles.append(_dot_nt(qs, kk_ref[0, kt * LANES:(kt + 1) * LANES, :]) + bias)
    lg = jnp.concatenate(tiles, axis=1)
    valid = lg > 0.5 * NEG_INF
    m = jnp.max(lg, axis=-1, keepdims=True)
    p = jnp.where(valid, jnp.exp(lg - m), 0.0)
    denom = jnp.sum(p, axis=-1, keepdims=True)
    p = p / jnp.where(denom > 0.0, denom, 1.0)
    o_cmp = _unstack_heads(_dot(p.astype(BF16), vv_ref[0]))
    gates = _gate_lanes(gc_ref[...], gexp_ref[...])
    o_ref[...] = gates[:, :NSA_WIDTH] * o_cmp
    p_sum = p[0:QB] + p[QB:2 * QB] + p[2 * QB:3 * QB] + p[3 * QB:4 * QB]
    imp = _dot_split(p_sum, ovl_ref[...])
    t = bi * QB + lax.broadcasted_iota(jnp.int32, (QB, LANES), 0)
    blk = lax.broadcasted_iota(jnp.int32, (QB, LANES), 1)
    cur = t // NSA_SEL_BLOCK
    forced = (blk == 0) | (blk == cur) | (blk == cur - 1)
    score = jnp.where(forced, SEL_FORCE, jnp.where(blk > cur, -SEL_FORCE, imp))
    score = jnp.where(blk < n_sel_blocks, score, -3e38)
    st = score.T
    jidx = lax.broadcasted_iota(jnp.int32, (LANES, QB), 0).astype(F32)
    unsel = jnp.ones((LANES, QB), F32)
    for _ in range(min(NSA_TOP_N, n_sel_blocks)):
        mx = jnp.max(st, axis=0, keepdims=True)
        first = jnp.min(jnp.where(st == mx, jidx, 1e9), axis=0, keepdims=True)
        pick = jidx == first
        unsel = jnp.where(pick, 0.0, unsel)
        st = jnp.where(pick, -3.4e38, st)
    nsel_ref[...] = unsel.T.astype(BF16)


def _nsa_cmp(qc, kk, vv, band, gc, ovl, gexp, bsz, seq):
    n = qc.shape[0]
    nb = seq // QB
    m = kk.shape[1]
    tok = lambda w: pl.BlockSpec((QB, w), lambda bi, b: (b * nb + bi, 0))
    seqblk = pl.BlockSpec((1, m, LANES), lambda bi, b: (b, 0, 0))
    const = lambda a: pl.BlockSpec(a.shape, lambda bi, b: (0,) * a.ndim)
    return pl.pallas_call(
        functools.partial(_nsa_cmp_kernel, seq // NSA_SEL_BLOCK),
        grid=(nb, bsz),
        in_specs=[tok(256), seqblk, seqblk, const(band), tok(LANES), const(ovl), const(gexp)],
        out_specs=[tok(256), tok(LANES)],
        out_shape=[jax.ShapeDtypeStruct((n, 256), F32), jax.ShapeDtypeStruct((n, LANES), BF16)],
        compiler_params=_cparams("parallel", "parallel"),
        name="nsa_compressed_topn",
    )(qc, kk, vv, band, gc, ovl, gexp)


def _flash_step(q, k, v, bias, carry):
    m, l, acc = carry
    lg = _dot_nt(q, k) + bias
    m_new = jnp.maximum(m, jnp.max(lg, axis=-1, keepdims=True))
    alpha = jnp.exp(m - m_new)
    p = jnp.exp(lg - m_new)
    l = alpha * l + jnp.sum(p, axis=-1, keepdims=True)
    acc = alpha * acc + _dot(p.astype(BF16), v)
    return m_new, l, acc


def _nsa_attn_kernel(q_ref, nsel_ref, ocmp_ref, gc_ref, ks_ref, e_ref, vs_ref, kw_ref, vw_ref,
                     tsel_ref, twin_ref, gexp_ref, o_ref):
    bi = pl.program_id(1)
    qs = _stack_heads(q_ref[...])
    nsel = nsel_ref[...]
    q_aug = jnp.concatenate([qs, jnp.concatenate([nsel] * NSA_HEADS, axis=0)], axis=1)
    init = (jnp.full((4 * QB, 1), NEG_INF, F32), jnp.zeros((4 * QB, 1), F32), jnp.zeros((4 * QB, LANES), F32))
    n_far = tsel_ref.shape[0] - 1

    def sel_body(kj, carry):
        r = pl.multiple_of(kj * QB, QB)
        k_aug = jnp.concatenate([ks_ref[pl.ds(r, QB), :], e_ref[pl.ds(r, QB), :]], axis=1)
        return _flash_step(q_aug, k_aug, vs_ref[pl.ds(r, QB), :], tsel_ref[jnp.minimum(bi - kj, n_far)], carry)

    _, l_s, acc_s = lax.fori_loop(0, bi + 1, sel_body, init)

    def win_body(dl, carry):
        r = pl.multiple_of((bi - dl) * QB, QB)
        return _flash_step(qs, kw_ref[pl.ds(r, QB), :], vw_ref[pl.ds(r, QB), :], twin_ref[dl], carry)

    _, l_w, acc_w = lax.fori_loop(0, jnp.minimum(bi, twin_ref.shape[0] - 1) + 1, win_body, init)

    gates = _gate_lanes(gc_ref[...], gexp_ref[...])
    o_sel = _unstack_heads(acc_s / l_s)
    o_win = _unstack_heads(acc_w / l_w)
    o_ref[...] = (ocmp_ref[...] + gates[:, NSA_WIDTH:2 * NSA_WIDTH] * o_sel
                  + gates[:, 2 * NSA_WIDTH:] * o_win)


def _nsa_attn(qc, nsel, ocmp, gc, ks, e_pen, vs, kw, vw, tsel, twin, gexp, bsz, seq):
    n = qc.shape[0]
    nb = seq // QB
    tok = lambda w: pl.BlockSpec((QB, w), lambda b, bi: (b * nb + bi, 0))
    seqblk = pl.BlockSpec((seq, LANES), lambda b, bi: (b, 0))
    const = lambda a: pl.BlockSpec(a.shape, lambda b, bi: (0,) * a.ndim)
    return pl.pallas_call(
        _nsa_attn_kernel,
        grid=(bsz, nb),
        in_specs=[tok(256), tok(LANES), tok(256), tok(LANES),
                  seqblk, const(e_pen), seqblk, seqblk, seqblk,
                  const(tsel), const(twin), const(gexp)],
        out_specs=tok(256),
        out_shape=jax.ShapeDtypeStruct((n, 256), F32),
        compiler_params=_cparams("parallel", "arbitrary"),
        name="nsa_selected_window",
    )(qc, nsel, ocmp, gc, ks, e_pen, vs, kw, vw, tsel, twin, gexp)


def _rms(x, gain):
    return x * lax.rsqrt(jnp.mean(x * x, axis=-1, keepdims=True) + EPS) * gain


def _out_kernel(x_ref, oa_ref, y_ref, oc_ref, gluw_ref, glub_ref, gout_ref, wout_ref, o_ref):
    wb = y_ref.shape[1]
    ab = _dot(jax.nn.gelu(y_ref[...]).astype(BF16), gluw_ref[...]) + glub_ref[...]
    ob = ab[:, :wb] * jax.nn.sigmoid(ab[:, wb:])
    a0, a1 = SWA_WIDTH, SWA_WIDTH + wb
    acc = x_ref[...]
    acc = acc + _dot(_rms(oa_ref[...], gout_ref[:, :a0]).astype(BF16), wout_ref[:a0, :])
    acc = acc + _dot(_rms(ob, gout_ref[:, a0:a1]).astype(BF16), wout_ref[a0:a1, :])
    acc = acc + _dot(_rms(oc_ref[...], gout_ref[:, a1:]).astype(BF16), wout_ref[a1:, :])
    o_ref[...] = acc


def _out_proj(x2, oa, y, oc, gluw, glub, gout, wout, tm):
    n, d = x2.shape
    tok = lambda w: pl.BlockSpec((tm, w), lambda i: (i, 0))
    full = lambda a: pl.BlockSpec(a.shape, lambda i: (0,) * a.ndim)
    return pl.pallas_call(
        _out_kernel,
        grid=(n // tm,),
        in_specs=[tok(d), tok(oa.shape[1]), tok(y.shape[1]), tok(oc.shape[1]),
                  full(gluw), full(glub), full(gout), full(wout)],
        out_specs=tok(d),
        out_shape=jax.ShapeDtypeStruct((n, d), F32),
        compiler_params=_cparams("parallel"),
        name="out_proj",
    )(x2, oa, y, oc, gluw, glub, gout, wout)


FFN_CHUNK = 1024


def _ffn_kernel(x_ref, g_ref, wu_ref, wd_ref, o_ref):
    x = x_ref[...]
    h = _rms(x, g_ref[...]).astype(BF16)
    acc = x
    for c in range(wu_ref.shape[1] // FFN_CHUNK):
        c0, c1 = c * FFN_CHUNK, (c + 1) * FFN_CHUNK
        hid = jnp.maximum(_dot(h, wu_ref[:, c0:c1]), 0.0)
        acc = acc + _dot((hid * hid).astype(BF16), wd_ref[c0:c1, :])
    o_ref[...] = acc


def _ffn(x2, g2, wu, wd, tm):
    n, d = x2.shape
    full = lambda a: pl.BlockSpec(a.shape, lambda i: (0,) * a.ndim)
    return pl.pallas_call(
        _ffn_kernel,
        grid=(n // tm,),
        in_specs=[pl.BlockSpec((tm, d), lambda i: (i, 0)), full(g2), full(wu), full(wd)],
        out_specs=pl.BlockSpec((tm, d), lambda i: (i, 0)),
        out_shape=jax.ShapeDtypeStruct((n, d), F32),
        compiler_params=_cparams("parallel"),
        name="ffn",
    )(x2, g2, wu, wd)


def _rel_bucket(dist):
    n = jnp.maximum(dist, 0)
    nf = jnp.maximum(n, 1).astype(F32)
    large = BUCKET_EXACT + (jnp.log(nf / BUCKET_EXACT) / math.log(BUCKET_MAX_DIST / BUCKET_EXACT)
                            * (NUM_BUCKETS - BUCKET_EXACT)).astype(jnp.int32)
    return jnp.where(n < BUCKET_EXACT, n, jnp.minimum(large, NUM_BUCKETS - 1))


def _bias_rows(tbl, dist, valid):
    onehot = jax.nn.one_hot(_rel_bucket(dist), NUM_BUCKETS, dtype=F32)
    b = jnp.einsum('...qkc,ch->...hqk', onehot, tbl.astype(F32), precision=HIGHEST)
    b = jnp.where(valid[..., None, :, :], b, NEG_INF)
    return b.reshape(b.shape[:-3] + (b.shape[-3] * b.shape[-2], b.shape[-1]))


def _bias_tables(rel_bias):
    swa_order = jnp.array([0, 2, 1, 3])
    tbl_a = rel_bias[:, :SWA_HEADS][:, swa_order]
    tbl_c = rel_bias[:, SWA_HEADS:]
    i = jnp.arange(QB)[:, None]
    d_swa = i - jnp.arange(2 * QB)[None, :] + QB
    bias_swa = _bias_rows(tbl_a, d_swa, (d_swa >= 0) & (d_swa < SWA_WINDOW))
    r = jnp.arange(CMP_PHASES)[:, None, None]
    c = jnp.arange(2 * LANES)[None, None, :]
    d_cmp = r * QB + i[None] - (c - LANES) * NSA_CMP_STRIDE - (NSA_CMP_BLOCK - 1)
    band_cmp = _bias_rows(tbl_c, d_cmp, d_cmp >= 0)
    j = jnp.arange(QB)[None, :]
    n_far = -(-(BUCKET_MAX_DIST + QB) // QB)
    d_sel = jnp.arange(n_far)[:, None, None] * QB + (i - j)[None]
    tsel = _bias_rows(tbl_c, d_sel, d_sel >= 0)
    d_win = jnp.arange(NSA_WINDOW // QB + 1)[:, None, None] * QB + (i - j)[None]
    twin = _bias_rows(tbl_c, d_win, (d_win >= 0) & (d_win < NSA_WINDOW))
    return bias_swa, band_cmp, tsel, twin


def _nsa_constants(seq):
    m = seq // NSA_CMP_STRIDE
    cs = jnp.arange(m)[:, None] * NSA_CMP_STRIDE
    ss = jnp.arange(LANES)[None, :] * NSA_SEL_BLOCK
    ovl = ((cs < ss + NSA_SEL_BLOCK) & (cs + NSA_CMP_BLOCK > ss) & (cs < seq - NSA_CMP_STRIDE)).astype(BF16)
    key_blk = jnp.arange(seq)[:, None] // NSA_SEL_BLOCK
    e_pen = jnp.where(key_blk == jnp.arange(LANES)[None, :], SEL_PENALTY, 0.0).astype(BF16)
    col = jnp.arange(LANES)[:, None]
    lane = jnp.arange(NSA_BRANCHES * NSA_WIDTH)[None, :]
    br, hd = lane // NSA_WIDTH, (lane % NSA_WIDTH) // HEAD_DIM
    gexp = (col == hd * NSA_BRANCHES + br).astype(BF16)
    return ovl, e_pen, gexp


def _seg_matrix():
    r = jnp.arange(2 * LANES)
    return jnp.where((r[:, None] // HEAD_DIM) == (r[None, :] // HEAD_DIM), 1.0 / HEAD_DIM, 0.0).astype(BF16)


def _prep_w_in(w):
    d = w.shape[0]
    hd = HEAD_DIM
    ssm_w = d - SWA_WIDTH - NSA_WIDTH
    o_ka = SWA_WIDTH
    o_va = o_ka + SWA_KV_WIDTH
    o_u = o_va + SWA_KV_WIDTH
    o_qc = o_u + ssm_w
    o_kv = o_qc + NSA_WIDTH
    o_gc = o_kv + 6 * hd
    head = lambda off, h: w[:, off + h * hd: off + (h + 1) * hd]
    kv = lambda s: w[:, o_kv + s * hd: o_kv + (s + 1) * hd]
    gates = w[:, o_gc:]
    cols = [head(0, 0), head(0, 2), head(0, 1), head(0, 3),
            w[:, o_ka:o_va], w[:, o_va:o_u], w[:, o_u:o_qc], w[:, o_qc:o_kv],
            kv(0), kv(1), kv(2), kv(2), kv(3), kv(3), kv(4), kv(4), kv(5), kv(5),
            gates, jnp.zeros((d, LANES - gates.shape[1]), w.dtype)]
    return jnp.concatenate(cols, axis=1).astype(BF16)


def _prep_qk_gains(qk_g):
    g = qk_g.astype(F32)
    t4 = lambda v: jnp.tile(v, 4)
    return jnp.stack([t4(g[0]), t4(g[1]), t4(g[2]), t4(g[4]), t4(g[5]),
                      t4(g[3]), t4(g[3]), t4(g[3])])


def kernel(x, norm1_g, w_in, qk_g, sinks, rel_bias, ssm_a_re, ssm_a_im, ssm_log_dt, ssm_b_re, ssm_b_im, ssm_c_re, ssm_c_im, ssm_d, glu_w, glu_b, cmp_pos, cmp_w1, cmp_w2, out_norm_g, w_out, norm2_g, w_up, w_down):
    bsz, seq, d = x.shape
    depth = w_in.shape[0]
    n = bsz * seq
    assert seq % SWA_TQ == 0 and seq // NSA_SEL_BLOCK <= LANES and d == 1024
    assert (seq // NSA_CMP_STRIDE) % LANES == 0
    tm = 512
    row = lambda v: v.astype(F32).reshape(1, -1)

    bias_swa, band_cmp, tsel, twin = _bias_tables(rel_bias)
    ovl, e_pen, gexp = _nsa_constants(seq)
    seg = _seg_matrix()
    swa_order = jnp.array([0, 2, 1, 3])
    swa_cols = (swa_order[:, None] * HEAD_DIM + jnp.arange(HEAD_DIM)[None, :]).reshape(-1)

    x2 = x.reshape(n, d)
    for l in range(depth):
        qkg = _prep_qk_gains(qk_g[l])
        (qa, ka, va, u, qc, kvc, ks, vs, kw, vw, gc) = _in_proj(
            x2, row(norm1_g[l]), _prep_w_in(w_in[l]), qkg, seg, tm)

        sink_rows = jnp.repeat(sinks[l].astype(F32)[swa_order], QB).reshape(-1, 1)
        o_a = _swa(qa, ka, va, bias_swa, sink_rows, bsz, seq)

        ops = _ssm_operators(ssm_a_re[l], ssm_a_im[l], ssm_log_dt[l], ssm_b_re[l], ssm_b_im[l],
                             ssm_c_re[l], ssm_c_im[l], ssm_d[l])
        y = _s5(u, ops, bsz, seq)

        pos, w1t, w1b, w2 = _prep_compress(cmp_pos[l], cmp_w1[l], cmp_w2[l])
        kk, vv = _nsa_compress(kvc, pos, w1t, w1b, w2, qkg[5:6, :LANES], bsz, seq)
        o_cmp, nsel = _nsa_cmp(qc, kk, vv, band_cmp, gc, ovl, gexp, bsz, seq)
        o_c = _nsa_attn(qc, nsel, o_cmp, gc, ks, e_pen, vs, kw, vw, tsel, twin, gexp, bsz, seq)

        gout = out_norm_g[l].astype(F32)
        gout = jnp.concatenate([gout[:SWA_WIDTH][swa_cols], gout[SWA_WIDTH:]]).reshape(1, -1)
        wout = jnp.concatenate([w_out[l][:SWA_WIDTH][swa_cols], w_out[l][SWA_WIDTH:]], axis=0).astype(BF16)
        x2 = _out_proj(x2, o_a, y, o_c, glu_w[l].astype(BF16), row(glu_b[l]), gout, wout, tm)

        x2 = _ffn(x2, row(norm2_g[l]), w_up[l].astype(BF16), w_down[l].astype(BF16), tm)
    return x2.reshape(bsz, seq, d)
```

```python
import functools
import math

import jax
import jax.numpy as jnp
from jax import lax
from jax.experimental import pallas as pl
from jax.experimental.pallas import tpu as pltpu

F32 = jnp.float32
BF16 = jnp.bfloat16
HIGHEST = lax.Precision.HIGHEST

HEAD_DIM = 64
SWA_HEADS = 4
SWA_KV_HEADS = 2
SWA_WINDOW = 128
NSA_HEADS = 4
NSA_CMP_BLOCK = 32
NSA_CMP_STRIDE = 16
NSA_SEL_BLOCK = 64
NSA_TOP_N = 16
NSA_WINDOW = 512
NSA_BRANCHES = 3
SSM_GROUP_CH = 16
SSM_STATE = 64
NUM_BUCKETS = 32
BUCKET_EXACT = NUM_BUCKETS // 2
BUCKET_MAX_DIST = 1024
EPS = 1e-6
NEG_INF = -1e30
SEL_FORCE = 1e4
SEL_PENALTY = -30000.0

LANES = 128
QB = 128
SSM_CHUNK = 16
SSM_OCT = LANES // SSM_GROUP_CH
CMP_PHASES = LANES * NSA_CMP_STRIDE // QB
VMEM_LIMIT = 56 * 1024 * 1024

SWA_WIDTH = SWA_HEADS * HEAD_DIM
SWA_KV_WIDTH = SWA_KV_HEADS * HEAD_DIM
NSA_WIDTH = NSA_HEADS * HEAD_DIM


def _cparams(*sem):
    return pltpu.CompilerParams(dimension_semantics=sem, vmem_limit_bytes=VMEM_LIMIT)


def _dot(a, b):
    return jnp.dot(a, b, preferred_element_type=F32)


def _dot_nt(a, b):
    return lax.dot_general(a, b, (((1,), (1,)), ((), ())), preferred_element_type=F32)


def _dot_split(a, b):
    hi = a.astype(BF16)
    lo = (a - hi.astype(F32)).astype(BF16)
    return _dot(hi, b) + _dot(lo, b)


def _lane_lo(rows):
    return lax.broadcasted_iota(jnp.int32, (rows, LANES), 1) < HEAD_DIM


def _stack_heads(q):
    lo = _lane_lo(q.shape[0])
    zero = jnp.zeros_like(q[:, :LANES])
    g0 = q[:, :LANES]
    g1 = q[:, LANES:]
    return jnp.concatenate([jnp.where(lo, g0, zero), jnp.where(lo, zero, g0),
                            jnp.where(lo, g1, zero), jnp.where(lo, zero, g1)], axis=0)


def _unstack_heads(o):
    n = o.shape[0] // 4
    lo = _lane_lo(n)
    return jnp.concatenate([jnp.where(lo, o[0:n], o[n:2 * n]),
                            jnp.where(lo, o[2 * n:3 * n], o[3 * n:4 * n])], axis=1)


def _seg_rms(p, seg, gain):
    ms = _dot_split(p * p, seg)
    return p * lax.rsqrt(ms + EPS) * gain


def _in_proj_kernel(x_ref, g_ref, w_ref, qkg_ref, seg_ref,
                    qa_ref, ka_ref, va_ref, u_ref, qc_ref, kvc_ref,
                    ks_ref, vs_ref, kw_ref, vw_ref, gc_ref):
    x = x_ref[...]
    ms = jnp.mean(x * x, axis=-1, keepdims=True)
    h = (x * lax.rsqrt(ms + EPS) * g_ref[...]).astype(BF16)
    seg = seg_ref[...]
    scale = HEAD_DIM ** -0.5

    def proj(a, b):
        return _dot(h, w_ref[:, a:b])

    qa_ref[...] = (_seg_rms(proj(0, 256), seg, qkg_ref[0:1, :]) * scale).astype(BF16)
    ka_ref[...] = _seg_rms(proj(256, 384), seg[:LANES, :LANES], qkg_ref[1:2, :LANES]).astype(BF16)
    va_ref[...] = proj(384, 512).astype(BF16)
    u_ref[...] = proj(512, 1024)
    qc_ref[...] = (_seg_rms(proj(1024, 1280), seg, qkg_ref[2:3, :]) * scale).astype(BF16)
    kvc_ref[...] = proj(1280, 1408)

    def dup_rms(p, gain):
        m = jnp.mean(p * p, axis=-1, keepdims=True)
        return p * lax.rsqrt(m + EPS) * gain

    ks_ref[...] = dup_rms(proj(1408, 1536), qkg_ref[3:4, :LANES]).astype(BF16)
    vs_ref[...] = proj(1536, 1664).astype(BF16)
    kw_ref[...] = dup_rms(proj(1664, 1792), qkg_ref[4:5, :LANES]).astype(BF16)
    vw_ref[...] = proj(1792, 1920).astype(BF16)
    gc_ref[...] = proj(1920, 2048)


def _in_proj(x2, g1, w, qkg, seg, tm):
    n, d = x2.shape
    widths = [(256, BF16), (128, BF16), (128, BF16), (512, F32), (256, BF16), (128, F32),
              (128, BF16), (128, BF16), (128, BF16), (128, BF16), (128, F32)]
    full = lambda a: pl.BlockSpec(a.shape, lambda i: (0,) * a.ndim)
    return pl.pallas_call(
        _in_proj_kernel,
        grid=(n // tm,),
        in_specs=[pl.BlockSpec((tm, d), lambda i: (i, 0)), full(g1), full(w), full(qkg), full(seg)],
        out_specs=[pl.BlockSpec((tm, wd), lambda i: (i, 0)) for wd, _ in widths],
        out_shape=[jax.ShapeDtypeStruct((n, wd), dt) for wd, dt in widths],
        compiler_params=_cparams("parallel"),
        name="in_proj",
    )(x2, g1, w, qkg, seg)


SWA_TQ = 512


def _swa_kernel(q_ref, kc_ref, kp_ref, vc_ref, vp_ref, bias_ref, sink_ref, o_ref):
    first = pl.program_id(1) == 0
    pen = jnp.where(first, NEG_INF, 0.0).astype(F32)
    sink = sink_ref[...]
    bias = bias_ref[...]
    for s in range(SWA_TQ // QB):
        r0, r1 = s * QB, (s + 1) * QB
        qs = _stack_heads(q_ref[r0:r1, :])
        k_cur = kc_ref[r0:r1, :]
        v_cur = vc_ref[r0:r1, :]
        if s == 0:
            k_prev, v_prev = kp_ref[...], vp_ref[...]
        else:
            k_prev, v_prev = kc_ref[r0 - QB:r0, :], vc_ref[r0 - QB:r0, :]
        lg_p = _dot_nt(qs, k_prev)
        if s == 0:
            lg_p = lg_p + pen
        lg = jnp.concatenate([lg_p, _dot_nt(qs, k_cur)], axis=1) + bias
        m = jnp.maximum(jnp.max(lg, axis=-1, keepdims=True), sink)
        p = jnp.exp(lg - m)
        denom = jnp.sum(p, axis=-1, keepdims=True) + jnp.exp(sink - m)
        pv = _dot(p[:, :QB].astype(BF16), v_prev) + _dot(p[:, QB:].astype(BF16), v_cur)
        o_ref[r0:r1, :] = _unstack_heads(pv / denom)


def _swa(qa, ka, va, bias, sink_rows, bsz, seq):
    n = qa.shape[0]
    nq = seq // SWA_TQ
    per = SWA_TQ // QB
    cur = lambda b, i: (b * nq + i, 0)
    prev = lambda b, i: (jnp.maximum((b * nq + i) * per - 1, 0), 0)
    const = lambda b, i: (0, 0)
    return pl.pallas_call(
        _swa_kernel,
        grid=(bsz, nq),
        in_specs=[pl.BlockSpec((SWA_TQ, 256), cur),
                  pl.BlockSpec((SWA_TQ, LANES), cur), pl.BlockSpec((QB, LANES), prev),
                  pl.BlockSpec((SWA_TQ, LANES), cur), pl.BlockSpec((QB, LANES), prev),
                  pl.BlockSpec(bias.shape, const), pl.BlockSpec(sink_rows.shape, const)],
        out_specs=pl.BlockSpec((SWA_TQ, 256), cur),
        out_shape=jax.ShapeDtypeStruct((n, 256), F32),
        compiler_params=_cparams("parallel", "parallel"),
        name="swa",
    )(qa, ka, ka, va, va, bias, sink_rows)


def _chunk_rows(u_ref):
    return jnp.concatenate([u_ref[:, t, :] for t in range(SSM_CHUNK)], axis=1)


def _ssm_z_kernel(u_ref, pz_ref, zre_ref, zim_ref):
    z = _dot(_chunk_rows(u_ref).astype(BF16), pz_ref[0])
    half = z.shape[1] // 2
    zre_ref[...] = z[:, :half]
    zim_ref[...] = z[:, half:]


def _u_spec(tnc):
    return pl.BlockSpec((None, tnc, SSM_CHUNK, LANES), lambda j, b, r: (b, r, 0, j))


def _state_spec(tnc, sw, noct):
    return pl.BlockSpec((tnc, sw), lambda j, b, r: (r, b * noct + j))


def _ssm_z(u4, pz, tnc):
    bsz, nch = u4.shape[:2]
    noct = pz.shape[0]
    sw = pz.shape[2] // 2
    state = _state_spec(tnc, sw, noct)
    return pl.pallas_call(
        _ssm_z_kernel,
        grid=(noct, bsz, nch // tnc),
        in_specs=[_u_spec(tnc), pl.BlockSpec((1,) + pz.shape[1:], lambda j, b, r: (j, 0, 0))],
        out_specs=[state, state],
        out_shape=[jax.ShapeDtypeStruct((nch, bsz * noct * sw), F32)] * 2,
        compiler_params=_cparams("parallel", "parallel", "parallel"),
        name="ssm_chunk_state",
    )(u4, pz)


def _ssm_scan_kernel(zre_ref, zim_ref, ar_ref, ai_ref, sre_ref, sim_ref):
    a_r = ar_ref[...]
    a_i = ai_ref[...]

    def body(c, carry):
        s_r, s_i = carry
        row = pl.ds(c, 1)
        sre_ref[row, :] = s_r
        sim_ref[row, :] = s_i
        return (a_r * s_r - a_i * s_i + zre_ref[row, :], a_r * s_i + a_i * s_r + zim_ref[row, :])

    zero = jnp.zeros(a_r.shape, F32)
    lax.fori_loop(0, zre_ref.shape[0], body, (zero, zero))


def _ssm_scan(zre, zim, a_r, a_i, tl):
    nchunk, width = zre.shape
    blk = pl.BlockSpec((nchunk, tl), lambda j: (0, j))
    coef = pl.BlockSpec((1, tl), lambda j: (0, j))
    return pl.pallas_call(
        _ssm_scan_kernel,
        grid=(width // tl,),
        in_specs=[blk, blk, coef, coef],
        out_specs=[blk, blk],
        out_shape=[jax.ShapeDtypeStruct(zre.shape, F32)] * 2,
        compiler_params=_cparams("parallel"),
        name="ssm_scan",
    )(zre, zim, a_r, a_i)


def _ssm_y_kernel(u_ref, sre_ref, sim_ref, tz_ref, cre_ref, cim_ref, d_ref, y_ref):
    v = _chunk_rows(u_ref)
    vb = v.astype(BF16)
    s_re = sre_ref[...].astype(BF16)
    s_im = sim_ref[...].astype(BF16)
    w = 2 * LANES
    for tt in range(SSM_CHUNK // 2):
        c0, c1 = tt * w, (tt + 1) * w
        acc = _dot(s_re, cre_ref[0, :, c0:c1]) + _dot(s_im, cim_ref[0, :, c0:c1]) + d_ref[0, :, c0:c1] * v[:, c0:c1]
        for ss in range(tt + 1):
            acc = acc + _dot(vb[:, ss * w:(ss + 1) * w], tz_ref[0, tt - ss])
        y_ref[:, 2 * tt, :] = acc[:, :LANES]
        y_ref[:, 2 * tt + 1, :] = acc[:, LANES:]


def _ssm_y(u4, sre, sim, mm, cre, cim, dvec, tnc):
    bsz, nch = u4.shape[:2]
    noct, sw = cre.shape[:2]
    per_oct = lambda a: pl.BlockSpec((1,) + a.shape[1:], lambda j, b, r: (j,) + (0,) * (a.ndim - 1))
    state = _state_spec(tnc, sw, noct)
    return pl.pallas_call(
        _ssm_y_kernel,
        grid=(noct, bsz, nch // tnc),
        in_specs=[_u_spec(tnc), state, state, per_oct(mm), per_oct(cre), per_oct(cim), per_oct(dvec)],
        out_specs=_u_spec(tnc),
        out_shape=jax.ShapeDtypeStruct(u4.shape, F32),
        compiler_params=_cparams("parallel", "parallel", "parallel"),
        name="ssm_output",
    )(u4, sre, sim, mm, cre, cim, dvec)


def _ssm_operators(a_re, a_im, log_dt, b_re, b_im, c_re, c_im, d_skip):
    g, n = a_re.shape
    p = SSM_GROUP_CH
    t = SSM_CHUNK
    o = SSM_OCT
    noct = g // o
    a = lax.complex(a_re.astype(F32), a_im.astype(F32))
    adt = a * jnp.exp(log_dt.astype(F32))[:, None]
    b_bar = ((jnp.exp(adt) - 1.0) / a)[..., None] * lax.complex(b_re.astype(F32), b_im.astype(F32))
    cm = lax.complex(c_re.astype(F32), c_im.astype(F32))
    pw = jnp.exp(adt[None] * jnp.arange(t + 1, dtype=F32)[:, None, None].astype(jnp.complex64))
    kern = jnp.einsum('gpn,tgn,gnq->tgpq', cm, pw[:t], b_bar, precision=HIGHEST).real
    bz = (pw[:t][::-1][:, :, :, None] * b_bar[None]).transpose(1, 0, 3, 2)
    cz = (cm[None] * pw[1:, :, None, :]).transpose(1, 3, 0, 2)
    eye = jnp.eye(o, dtype=F32)
    oct_ = lambda x: x.reshape((noct, o) + x.shape[1:])
    dlag = jnp.einsum('ljapq,ab->ljaqbp', kern.reshape(t, noct, o, p, p), eye).reshape(t, noct, o * p, o * p)
    dlag = jnp.concatenate([jnp.zeros_like(dlag[:1]), dlag], axis=0)

    def pair_block(dl):
        top = jnp.concatenate([dlag[2 * dl + 1], dlag[2 * dl + 2]], axis=-1)
        bot = jnp.concatenate([dlag[2 * dl], dlag[2 * dl + 1]], axis=-1)
        return jnp.concatenate([top, bot], axis=-2)

    m_oct = jnp.stack([pair_block(dl) for dl in range(t // 2)], axis=1)
    pz_part = lambda x: jnp.einsum('jasqn,ab->jsaqbn', oct_(x), eye).reshape(noct, t * o * p, o * n)
    c_part = lambda x: jnp.einsum('jantp,ab->jantbp', oct_(x), eye).reshape(noct, o * n, t * o * p)
    pz = jnp.concatenate([pz_part(bz.real), pz_part(bz.imag)], axis=2)
    dvec = jnp.broadcast_to(d_skip.astype(F32).reshape(noct, 1, o, p), (noct, t, o, p)).reshape(noct, 1, t * o * p)
    a_chunk = pw[t].reshape(1, -1)
    return (m_oct.astype(BF16), pz.astype(BF16), c_part(cz.real).astype(BF16), c_part(-cz.imag).astype(BF16),
            dvec, a_chunk.real, a_chunk.imag)


def _s5(u, ops, bsz, seq):
    m_oct, pz, c_re, c_im, dvec, a_r, a_i = ops
    nch = seq // SSM_CHUNK
    u4 = u.reshape(bsz, nch, SSM_CHUNK, u.shape[1])
    tnc = min(nch, 256)
    zre, zim = _ssm_z(u4, pz, tnc)
    coef = lambda c: jnp.tile(c, (1, bsz))
    sre, sim = _ssm_scan(zre, zim, coef(a_r), coef(a_i), 2048)
    return _ssm_y(u4, sre, sim, m_oct, c_re, c_im, dvec, tnc).reshape(u.shape)


def _cmp_kernel(kvc_ref, pos_ref, w1t_ref, w1b_ref, w2_ref, gk_ref, kk_ref, vv_ref):
    rows = kvc_ref.shape[0] // NSA_CMP_STRIDE
    top = jnp.zeros((rows, w1t_ref.shape[2]), F32)
    bot = top
    for tau in range(NSA_CMP_STRIDE):
        tok = kvc_ref[pl.ds(tau, rows, stride=NSA_CMP_STRIDE), :]
        top = top + _dot((tok + pos_ref[tau:tau + 1, :]).astype(BF16), w1t_ref[tau])
        bot = bot + _dot((tok + pos_ref[NSA_CMP_STRIDE + tau:NSA_CMP_STRIDE + tau + 1, :]).astype(BF16), w1b_ref[tau])
    hid = top + pltpu.roll(bot, rows - 1, 0)
    out = _dot(jax.nn.gelu(hid).astype(BF16), w2_ref[...])
    k = out[:, :LANES]
    ms = jnp.mean(k * k, axis=-1, keepdims=True)
    kk_ref[0] = (k * lax.rsqrt(ms + EPS) * gk_ref[...]).astype(BF16)
    vv_ref[0] = out[:, LANES:].astype(BF16)


def _nsa_compress(kvc, pos, w1t, w1b, w2, gk, bsz, seq):
    rows = seq // NSA_CMP_STRIDE
    full = lambda a: pl.BlockSpec(a.shape, lambda b: (0,) * a.ndim)
    out = pl.BlockSpec((1, rows, LANES), lambda b: (b, 0, 0))
    return pl.pallas_call(
        _cmp_kernel,
        grid=(bsz,),
        in_specs=[pl.BlockSpec((seq, LANES), lambda b: (b, 0)), full(pos), full(w1t), full(w1b), full(w2), full(gk)],
        out_specs=[out, out],
        out_shape=[jax.ShapeDtypeStruct((bsz, rows, LANES), BF16)] * 2,
        compiler_params=_cparams("parallel"),
        name="nsa_compress",
    )(kvc, pos, w1t, w1b, w2, gk)


def _prep_compress(cmp_pos, cmp_w1, cmp_w2):
    eye = jnp.eye(2, dtype=F32)
    pos = cmp_pos.astype(F32).transpose(1, 0, 2).reshape(NSA_CMP_BLOCK, 2 * HEAD_DIM)
    w1 = cmp_w1.astype(F32).reshape(2, NSA_CMP_BLOCK, HEAD_DIM, -1)
    w1 = jnp.einsum('spdh,sz->psdzh', w1, eye).reshape(NSA_CMP_BLOCK, 2 * HEAD_DIM, -1).astype(BF16)
    w2 = jnp.concatenate([cmp_w2, cmp_w2], axis=-1).astype(F32)
    w2 = jnp.einsum('shd,sz->shzd', w2, eye).reshape(2 * w2.shape[1], 2 * LANES).astype(BF16)
    return pos, w1[:NSA_CMP_STRIDE], w1[NSA_CMP_STRIDE:], w2


def _gate_lanes(gc, gexp):
    return _dot_split(jax.nn.sigmoid(gc), gexp)


def _nsa_cmp_kernel(n_sel_blocks, q_ref, kk_ref, vv_ref, band_ref, gc_ref, ovl_ref, gexp_ref,
                    o_ref, nsel_ref):
    bi = pl.program_id(0)
    qs = _stack_heads(q_ref[...])
    band = band_ref[bi % CMP_PHASES]
    diag = bi // CMP_PHASES
    far = band[:, 0:1]
    tiles = []
    for kt in range(kk_ref.shape[1] // LANES):
        bias = jnp.where(kt == diag, band[:, LANES:],
                         jnp.where(kt == diag - 1, band[:, :LANES], jnp.where(kt < diag, far, NEG_INF)))
        tiles.append(_dot_nt(qs, kk_ref[0, kt * LANES:(kt + 1) * LANES, :]) + bias)
    lg = jnp.concatenate(tiles, axis=1)
    valid = lg > 0.5 * NEG_INF
    m = jnp.max(lg, axis=-1, keepdims=True)
    p = jnp.where(valid, jnp.exp(lg - m), 0.0)
    denom = jnp.sum(p, axis=-1, keepdims=True)
    p = p / jnp.where(denom > 0.0, denom, 1.0)
    o_cmp = _unstack_heads(_dot(p.astype(BF16), vv_ref[0]))
    gates = _gate_lanes(gc_ref[...], gexp_ref[...])
    o_ref[...] = gates[:, :NSA_WIDTH] * o_cmp
    p_sum = p[0:QB] + p[QB:2 * QB] + p[2 * QB:3 * QB] + p[3 * QB:4 * QB]
    imp = _dot_split(p_sum, ovl_ref[...])
    t = bi * QB + lax.broadcasted_iota(jnp.int32, (QB, LANES), 0)
    blk = lax.broadcasted_iota(jnp.int32, (QB, LANES), 1)
    cur = t // NSA_SEL_BLOCK
    forced = (blk == 0) | (blk == cur) | (blk == cur - 1)
    score = jnp.where(forced, SEL_FORCE, jnp.where(blk > cur, -SEL_FORCE, imp))
    score = jnp.where(blk < n_sel_blocks, score, -3e38)
    st = score.T
    jidx = lax.broadcasted_iota(jnp.int32, (LANES, QB), 0).astype(F32)
    unsel = jnp.ones((LANES, QB), F32)
    for _ in range(min(NSA_TOP_N, n_sel_blocks)):
        mx = jnp.max(st, axis=0, keepdims=True)
        first = jnp.min(jnp.where(st == mx, jidx, 1e9), axis=0, keepdims=True)
        pick = jidx == first
        unsel = jnp.where(pick, 0.0, unsel)
        st = jnp.where(pick, -3.4e38, st)
    nsel_ref[...] = unsel.T.astype(BF16)


def _nsa_cmp(qc, kk, vv, band, gc, ovl, gexp, bsz, seq):
    n = qc.shape[0]
    nb = seq // QB
    m = kk.shape[1]
    tok = lambda w: pl.BlockSpec((QB, w), lambda bi, b: (b * nb + bi, 0))
    seqblk = pl.BlockSpec((1, m, LANES), lambda bi, b: (b, 0, 0))
    const = lambda a: pl.BlockSpec(a.shape, lambda bi, b: (0,) * a.ndim)
    return pl.pallas_call(
        functools.partial(_nsa_cmp_kernel, seq // NSA_SEL_BLOCK),
        grid=(nb, bsz),
        in_specs=[tok(256), seqblk, seqblk, const(band), tok(LANES), const(ovl), const(gexp)],
        out_specs=[tok(256), tok(LANES)],
        out_shape=[jax.ShapeDtypeStruct((n, 256), F32), jax.ShapeDtypeStruct((n, LANES), BF16)],
        compiler_params=_cparams("parallel", "parallel"),
        name="nsa_compressed_topn",
    )(qc, kk, vv, band, gc, ovl, gexp)


SEL_TK = 512
WIN_TILES = NSA_WINDOW // QB + 1


def _lag_bias(tbl_ref, bi, first_tile, n_tiles):
    last = tbl_ref.shape[0] - 2
    return jnp.concatenate([tbl_ref[jnp.clip(bi - (first_tile + i), -1, last) + 1] for i in range(n_tiles)], axis=1)


def _with_ones(v):
    return jnp.concatenate([v, jnp.ones_like(v)], axis=1)


def _nsa_attn_kernel(q_ref, nsel_ref, ocmp_ref, gc_ref, ks_ref, e_ref, vs_ref, kw_ref, vw_ref,
                     tsel_ref, twin_ref, gexp_ref, o_ref):
    bi = pl.program_id(1)
    qs = _stack_heads(q_ref[...])
    nsel = nsel_ref[...]
    q_aug = jnp.concatenate([qs, jnp.concatenate([nsel] * NSA_HEADS, axis=0)], axis=1)
    per = SEL_TK // QB

    last = bi // per

    def sel_logits(kt):
        r = pl.multiple_of(kt * SEL_TK, SEL_TK)
        k_aug = jnp.concatenate([ks_ref[pl.ds(r, SEL_TK), :], e_ref[pl.ds(r, SEL_TK), :]], axis=1)
        lg = _dot_nt(q_aug, k_aug) + _lag_bias(tsel_ref, bi, kt * per, per)
        return lg, jnp.max(lg, axis=-1, keepdims=True)

    def sel_body(kt, carry):
        m, acc, (lg, tile_max) = carry
        nxt = sel_logits(jnp.minimum(kt + 1, last))
        v_ext = _with_ones(vs_ref[pl.ds(pl.multiple_of(kt * SEL_TK, SEL_TK), SEL_TK), :])
        m_new = jnp.maximum(m, tile_max)
        p = jnp.exp(lg - m_new).astype(BF16)
        return m_new, jnp.exp(m - m_new) * acc + _dot(p, v_ext), nxt

    init = (jnp.full((4 * QB, 1), NEG_INF, F32), jnp.zeros((4 * QB, 2 * LANES), F32), sel_logits(0))
    _, acc_s, _ = lax.fori_loop(0, last + 1, sel_body, init)

    first = jnp.maximum(bi - (WIN_TILES - 1), 0)
    r = pl.multiple_of(first * QB, QB)
    lg = _dot_nt(qs, kw_ref[pl.ds(r, WIN_TILES * QB), :]) + _lag_bias(twin_ref, bi, first, WIN_TILES)
    p = jnp.exp(lg - jnp.max(lg, axis=-1, keepdims=True)).astype(BF16)
    acc_w = _dot(p, _with_ones(vw_ref[pl.ds(r, WIN_TILES * QB), :]))

    gates = _gate_lanes(gc_ref[...], gexp_ref[...])
    o_sel = _unstack_heads(acc_s[:, :LANES] / acc_s[:, LANES:])
    o_win = _unstack_heads(acc_w[:, :LANES] / acc_w[:, LANES:])
    o_ref[...] = (ocmp_ref[...] + gates[:, NSA_WIDTH:2 * NSA_WIDTH] * o_sel
                  + gates[:, 2 * NSA_WIDTH:] * o_win)


def _nsa_attn(qc, nsel, ocmp, gc, ks, e_pen, vs, kw, vw, tsel, twin, gexp, bsz, seq):
    n = qc.shape[0]
    nb = seq // QB
    tok = lambda w: pl.BlockSpec((QB, w), lambda b, bi: (b * nb + bi, 0))
    seqblk = pl.BlockSpec((seq, LANES), lambda b, bi: (b, 0))
    const = lambda a: pl.BlockSpec(a.shape, lambda b, bi: (0,) * a.ndim)
    return pl.pallas_call(
        _nsa_attn_kernel,
        grid=(bsz, nb),
        in_specs=[tok(256), tok(LANES), tok(256), tok(LANES),
                  seqblk, const(e_pen), seqblk, seqblk, seqblk,
                  const(tsel), const(twin), const(gexp)],
        out_specs=tok(256),
        out_shape=jax.ShapeDtypeStruct((n, 256), F32),
        compiler_params=_cparams("parallel", "arbitrary"),
        name="nsa_selected_window",
    )(qc, nsel, ocmp, gc, ks, e_pen, vs, kw, vw, tsel, twin, gexp)


def _rms(x, gain):
    return x * lax.rsqrt(jnp.mean(x * x, axis=-1, keepdims=True) + EPS) * gain


def _out_kernel(x_ref, oa_ref, y_ref, oc_ref, gluw_ref, glub_ref, gout_ref, wout_ref, o_ref):
    wb = y_ref.shape[1]
    ab = _dot(jax.nn.gelu(y_ref[...]).astype(BF16), gluw_ref[...]) + glub_ref[...]
    ob = ab[:, :wb] * jax.nn.sigmoid(ab[:, wb:])
    a0, a1 = SWA_WIDTH, SWA_WIDTH + wb
    acc = x_ref[...]
    acc = acc + _dot(_rms(oa_ref[...], gout_ref[:, :a0]).astype(BF16), wout_ref[:a0, :])
    acc = acc + _dot(_rms(ob, gout_ref[:, a0:a1]).astype(BF16), wout_ref[a0:a1, :])
    acc = acc + _dot(_rms(oc_ref[...], gout_ref[:, a1:]).astype(BF16), wout_ref[a1:, :])
    o_ref[...] = acc


def _out_proj(x2, oa, y, oc, gluw, glub, gout, wout, tm):
    n, d = x2.shape
    tok = lambda w: pl.BlockSpec((tm, w), lambda i: (i, 0))
    full = lambda a: pl.BlockSpec(a.shape, lambda i: (0,) * a.ndim)
    return pl.pallas_call(
        _out_kernel,
        grid=(n // tm,),
        in_specs=[tok(d), tok(oa.shape[1]), tok(y.shape[1]), tok(oc.shape[1]),
                  full(gluw), full(glub), full(gout), full(wout)],
        out_specs=tok(d),
        out_shape=jax.ShapeDtypeStruct((n, d), F32),
        compiler_params=_cparams("parallel"),
        name="out_proj",
    )(x2, oa, y, oc, gluw, glub, gout, wout)


FFN_CHUNK = 1024


def _ffn_kernel(x_ref, g_ref, wu_ref, wd_ref, o_ref):
    x = x_ref[...]
    h = _rms(x, g_ref[...]).astype(BF16)
    acc = x
    for c in range(wu_ref.shape[1] // FFN_CHUNK):
        c0, c1 = c * FFN_CHUNK, (c + 1) * FFN_CHUNK
        hid = jnp.maximum(_dot(h, wu_ref[:, c0:c1]), 0.0)
        acc = acc + _dot((hid * hid).astype(BF16), wd_ref[c0:c1, :])
    o_ref[...] = acc


def _ffn(x2, g2, wu, wd, tm):
    n, d = x2.shape
    full = lambda a: pl.BlockSpec(a.shape, lambda i: (0,) * a.ndim)
    return pl.pallas_call(
        _ffn_kernel,
        grid=(n // tm,),
        in_specs=[pl.BlockSpec((tm, d), lambda i: (i, 0)), full(g2), full(wu), full(wd)],
        out_specs=pl.BlockSpec((tm, d), lambda i: (i, 0)),
        out_shape=jax.ShapeDtypeStruct((n, d), F32),
        compiler_params=_cparams("parallel"),
        name="ffn",
    )(x2, g2, wu, wd)


def _rel_bucket(dist):
    n = jnp.maximum(dist, 0)
    nf = jnp.maximum(n, 1).astype(F32)
    large = BUCKET_EXACT + (jnp.log(nf / BUCKET_EXACT) / math.log(BUCKET_MAX_DIST / BUCKET_EXACT)
                            * (NUM_BUCKETS - BUCKET_EXACT)).astype(jnp.int32)
    return jnp.where(n < BUCKET_EXACT, n, jnp.minimum(large, NUM_BUCKETS - 1))


def _bias_rows(tbl, dist, valid):
    onehot = jax.nn.one_hot(_rel_bucket(dist), NUM_BUCKETS, dtype=F32)
    b = jnp.einsum('...qkc,ch->...hqk', onehot, tbl.astype(F32), precision=HIGHEST)
    b = jnp.where(valid[..., None, :, :], b, NEG_INF)
    return b.reshape(b.shape[:-3] + (b.shape[-3] * b.shape[-2], b.shape[-1]))


def _bias_tables(rel_bias):
    swa_order = jnp.array([0, 2, 1, 3])
    tbl_a = rel_bias[:, :SWA_HEADS][:, swa_order]
    tbl_c = rel_bias[:, SWA_HEADS:]
    i = jnp.arange(QB)[:, None]
    d_swa = i - jnp.arange(2 * QB)[None, :] + QB
    bias_swa = _bias_rows(tbl_a, d_swa, (d_swa >= 0) & (d_swa < SWA_WINDOW))
    r = jnp.arange(CMP_PHASES)[:, None, None]
    c = jnp.arange(2 * LANES)[None, None, :]
    d_cmp = r * QB + i[None] - (c - LANES) * NSA_CMP_STRIDE - (NSA_CMP_BLOCK - 1)
    band_cmp = _bias_rows(tbl_c, d_cmp, d_cmp >= 0)
    j = jnp.arange(QB)[None, :]
    n_far = -(-(BUCKET_MAX_DIST + QB) // QB)
    d_sel = jnp.arange(n_far)[:, None, None] * QB + (i - j)[None]
    tsel = _bias_rows(tbl_c, d_sel, d_sel >= 0)
    d_win = jnp.arange(WIN_TILES)[:, None, None] * QB + (i - j)[None]
    twin = _bias_rows(tbl_c, d_win, (d_win >= 0) & (d_win < NSA_WINDOW))
    future = jnp.full((1,) + tsel.shape[1:], NEG_INF, F32)
    return bias_swa, band_cmp, jnp.concatenate([future, tsel]), jnp.concatenate([future, twin])


def _nsa_constants(seq):
    m = seq // NSA_CMP_STRIDE
    cs = jnp.arange(m)[:, None] * NSA_CMP_STRIDE
    ss = jnp.arange(LANES)[None, :] * NSA_SEL_BLOCK
    ovl = ((cs < ss + NSA_SEL_BLOCK) & (cs + NSA_CMP_BLOCK > ss) & (cs < seq - NSA_CMP_STRIDE)).astype(BF16)
    key_blk = jnp.arange(seq)[:, None] // NSA_SEL_BLOCK
    e_pen = jnp.where(key_blk == jnp.arange(LANES)[None, :], SEL_PENALTY, 0.0).astype(BF16)
    col = jnp.arange(LANES)[:, None]
    lane = jnp.arange(NSA_BRANCHES * NSA_WIDTH)[None, :]
    br, hd = lane // NSA_WIDTH, (lane % NSA_WIDTH) // HEAD_DIM
    gexp = (col == hd * NSA_BRANCHES + br).astype(BF16)
    return ovl, e_pen, gexp


def _seg_matrix():
    r = jnp.arange(2 * LANES)
    return jnp.where((r[:, None] // HEAD_DIM) == (r[None, :] // HEAD_DIM), 1.0 / HEAD_DIM, 0.0).astype(BF16)


def _prep_w_in(w):
    d = w.shape[0]
    hd = HEAD_DIM
    ssm_w = d - SWA_WIDTH - NSA_WIDTH
    o_ka = SWA_WIDTH
    o_va = o_ka + SWA_KV_WIDTH
    o_u = o_va + SWA_KV_WIDTH
    o_qc = o_u + ssm_w
    o_kv = o_qc + NSA_WIDTH
    o_gc = o_kv + 6 * hd
    head = lambda off, h: w[:, off + h * hd: off + (h + 1) * hd]
    kv = lambda s: w[:, o_kv + s * hd: o_kv + (s + 1) * hd]
    gates = w[:, o_gc:]
    cols = [head(0, 0), head(0, 2), head(0, 1), head(0, 3),
            w[:, o_ka:o_va], w[:, o_va:o_u], w[:, o_u:o_qc], w[:, o_qc:o_kv],
            kv(0), kv(1), kv(2), kv(2), kv(3), kv(3), kv(4), kv(4), kv(5), kv(5),
            gates, jnp.zeros((d, LANES - gates.shape[1]), w.dtype)]
    return jnp.concatenate(cols, axis=1).astype(BF16)


def _prep_qk_gains(qk_g):
    g = qk_g.astype(F32)
    t4 = lambda v: jnp.tile(v, 4)
    return jnp.stack([t4(g[0]), t4(g[1]), t4(g[2]), t4(g[4]), t4(g[5]),
                      t4(g[3]), t4(g[3]), t4(g[3])])


def kernel(x, norm1_g, w_in, qk_g, sinks, rel_bias, ssm_a_re, ssm_a_im, ssm_log_dt, ssm_b_re, ssm_b_im, ssm_c_re, ssm_c_im, ssm_d, glu_w, glu_b, cmp_pos, cmp_w1, cmp_w2, out_norm_g, w_out, norm2_g, w_up, w_down):
    bsz, seq, d = x.shape
    depth = w_in.shape[0]
    n = bsz * seq
    assert seq % SWA_TQ == 0 and seq // NSA_SEL_BLOCK <= LANES and d == 1024
    assert (seq // NSA_CMP_STRIDE) % LANES == 0 and seq % SEL_TK == 0 and seq >= WIN_TILES * QB
    tm = 512
    row = lambda v: v.astype(F32).reshape(1, -1)

    bias_swa, band_cmp, tsel, twin = _bias_tables(rel_bias)
    ovl, e_pen, gexp = _nsa_constants(seq)
    seg = _seg_matrix()
    swa_order = jnp.array([0, 2, 1, 3])
    swa_cols = (swa_order[:, None] * HEAD_DIM + jnp.arange(HEAD_DIM)[None, :]).reshape(-1)

    x2 = x.reshape(n, d)
    for l in range(depth):
        qkg = _prep_qk_gains(qk_g[l])
        (qa, ka, va, u, qc, kvc, ks, vs, kw, vw, gc) = _in_proj(
            x2, row(norm1_g[l]), _prep_w_in(w_in[l]), qkg, seg, tm)

        sink_rows = jnp.repeat(sinks[l].astype(F32)[swa_order], QB).reshape(-1, 1)
        o_a = _swa(qa, ka, va, bias_swa, sink_rows, bsz, seq)

        ops = _ssm_operators(ssm_a_re[l], ssm_a_im[l], ssm_log_dt[l], ssm_b_re[l], ssm_b_im[l],
                             ssm_c_re[l], ssm_c_im[l], ssm_d[l])
        y = _s5(u, ops, bsz, seq)

        pos, w1t, w1b, w2 = _prep_compress(cmp_pos[l], cmp_w1[l], cmp_w2[l])
        kk, vv = _nsa_compress(kvc, pos, w1t, w1b, w2, qkg[5:6, :LANES], bsz, seq)
        o_cmp, nsel = _nsa_cmp(qc, kk, vv, band_cmp, gc, ovl, gexp, bsz, seq)
        o_c = _nsa_attn(qc, nsel, o_cmp, gc, ks, e_pen, vs, kw, vw, tsel, twin, gexp, bsz, seq)

        gout = out_norm_g[l].astype(F32)
        gout = jnp.concatenate([gout[:SWA_WIDTH][swa_cols], gout[SWA_WIDTH:]]).reshape(1, -1)
        wout = jnp.concatenate([w_out[l][:SWA_WIDTH][swa_cols], w_out[l][SWA_WIDTH:]], axis=0).astype(BF16)
        x2 = _out_proj(x2, o_a, y, o_c, glu_w[l].astype(BF16), row(glu_b[l]), gout, wout, tm)

        x2 = _ffn(x2, row(norm2_g[l]), w_up[l].astype(BF16), w_down[l].astype(BF16), tm)
    return x2.reshape(bsz, seq, d)
```

```python
import functools
import math

import jax
import jax.numpy as jnp
from jax import lax
from jax.experimental import pallas as pl
from jax.experimental.pallas import tpu as pltpu

F32 = jnp.float32
BF16 = jnp.bfloat16
HIGHEST = lax.Precision.HIGHEST

HEAD_DIM = 64
SWA_HEADS = 4
SWA_KV_HEADS = 2
SWA_WINDOW = 128
NSA_HEADS = 4
NSA_CMP_BLOCK = 32
NSA_CMP_STRIDE = 16
NSA_SEL_BLOCK = 64
NSA_TOP_N = 16
NSA_WINDOW = 512
NSA_BRANCHES = 3
SSM_GROUP_CH = 16
SSM_STATE = 64
NUM_BUCKETS = 32
BUCKET_EXACT = NUM_BUCKETS // 2
BUCKET_MAX_DIST = 1024
EPS = 1e-6
NEG_INF = -1e30
SEL_FORCE = 1e4
SEL_PENALTY = -30000.0
LOG2E = math.log2(math.e)

LANES = 128
QB = 128
SSM_CHUNK = 16
SSM_OCT = LANES // SSM_GROUP_CH
CMP_PHASES = LANES * NSA_CMP_STRIDE // QB
VMEM_LIMIT = 56 * 1024 * 1024

SWA_WIDTH = SWA_HEADS * HEAD_DIM
SWA_KV_WIDTH = SWA_KV_HEADS * HEAD_DIM
NSA_WIDTH = NSA_HEADS * HEAD_DIM


def _cparams(*sem):
    return pltpu.CompilerParams(dimension_semantics=sem, vmem_limit_bytes=VMEM_LIMIT)


def _dot(a, b):
    return jnp.dot(a, b, preferred_element_type=F32)


def _dot_nt(a, b):
    return lax.dot_general(a, b, (((1,), (1,)), ((), ())), preferred_element_type=F32)


def _dot_split(a, b):
    hi = a.astype(BF16)
    lo = (a - hi.astype(F32)).astype(BF16)
    return _dot(hi, b) + _dot(lo, b)


def _with_ones(v):
    return jnp.concatenate([v, jnp.ones_like(v)], axis=1)


def _lane_lo(rows):
    return lax.broadcasted_iota(jnp.int32, (rows, LANES), 1) < HEAD_DIM


def _stack_heads(q):
    lo = _lane_lo(q.shape[0])
    zero = jnp.zeros_like(q[:, :LANES])
    g0 = q[:, :LANES]
    g1 = q[:, LANES:]
    return jnp.concatenate([jnp.where(lo, g0, zero), jnp.where(lo, zero, g0),
                            jnp.where(lo, g1, zero), jnp.where(lo, zero, g1)], axis=0)


def _unstack_heads(o):
    n = o.shape[0] // 4
    lo = _lane_lo(n)
    return jnp.concatenate([jnp.where(lo, o[0:n], o[n:2 * n]),
                            jnp.where(lo, o[2 * n:3 * n], o[3 * n:4 * n])], axis=1)


def _seg_rms(p, seg, gain):
    ms = _dot_split(p * p, seg)
    return p * lax.rsqrt(ms + EPS) * gain


def _in_proj_kernel(x_ref, g_ref, w_ref, qkg_ref, seg_ref,
                    qa_ref, ka_ref, va_ref, u_ref, qc_ref, kvc_ref,
                    ks_ref, vs_ref, kw_ref, vw_ref, gc_ref):
    x = x_ref[...]
    ms = jnp.mean(x * x, axis=-1, keepdims=True)
    h = (x * lax.rsqrt(ms + EPS) * g_ref[...]).astype(BF16)
    seg = seg_ref[...]
    scale = HEAD_DIM ** -0.5

    def proj(a, b):
        return _dot(h, w_ref[:, a:b])

    lo = _lane_lo(x.shape[0])

    def dup_halves(kv):
        swapped = pltpu.roll(kv, HEAD_DIM, 1)
        return jnp.where(lo, kv, swapped), jnp.where(lo, swapped, kv)

    def dup_rms(p, gain):
        m = jnp.mean(p * p, axis=-1, keepdims=True)
        return p * lax.rsqrt(m + EPS) * (gain * LOG2E)

    qa_ref[...] = (_seg_rms(proj(0, 256), seg, qkg_ref[0:1, :]) * scale).astype(BF16)
    kva = proj(256, 512)
    ka_ref[...] = _seg_rms(kva, seg, qkg_ref[1:2, :])[:, :LANES].astype(BF16)
    va_ref[...] = kva[:, LANES:].astype(BF16)
    u_ref[...] = proj(512, 1024)
    qc_ref[...] = (_seg_rms(proj(1024, 1280), seg, qkg_ref[2:3, :]) * scale).astype(BF16)
    cmp_sel = proj(1280, 1536)
    kvc_ref[...] = cmp_sel[:, :LANES]
    ks, vs = dup_halves(cmp_sel[:, LANES:])
    ks_ref[...] = dup_rms(ks, qkg_ref[3:4, :LANES]).astype(BF16)
    vs_ref[...] = vs.astype(BF16)
    win_gate = proj(1536, 1792)
    kw, vw = dup_halves(win_gate[:, :LANES])
    kw_ref[...] = dup_rms(kw, qkg_ref[4:5, :LANES]).astype(BF16)
    vw_ref[...] = vw.astype(BF16)
    gc_ref[...] = win_gate[:, LANES:]


def _in_proj(x2, g1, w, qkg, seg, tm):
    n, d = x2.shape
    widths = [(256, BF16), (128, BF16), (128, BF16), (512, F32), (256, BF16), (128, F32),
              (128, BF16), (128, BF16), (128, BF16), (128, BF16), (128, F32)]
    full = lambda a: pl.BlockSpec(a.shape, lambda i: (0,) * a.ndim)
    return pl.pallas_call(
        _in_proj_kernel,
        grid=(n // tm,),
        in_specs=[pl.BlockSpec((tm, d), lambda i: (i, 0)), full(g1), full(w), full(qkg), full(seg)],
        out_specs=[pl.BlockSpec((tm, wd), lambda i: (i, 0)) for wd, _ in widths],
        out_shape=[jax.ShapeDtypeStruct((n, wd), dt) for wd, dt in widths],
        compiler_params=_cparams("parallel"),
        name="in_proj",
    )(x2, g1, w, qkg, seg)


SWA_TQ = 512


def _swa_kernel(q_ref, kc_ref, kp_ref, vc_ref, vp_ref, bias_ref, sink_ref, o_ref):
    first = pl.program_id(1) == 0
    prev_cols = lax.broadcasted_iota(jnp.int32, (1, 2 * QB), 1) < QB
    pen = jnp.where(first & prev_cols, NEG_INF, 0.0)
    sink = sink_ref[...]
    bias = bias_ref[...]
    for s in range(SWA_TQ // QB):
        r0, r1 = s * QB, (s + 1) * QB
        qs = _stack_heads(q_ref[r0:r1, :])
        k_cur = kc_ref[r0:r1, :]
        v_cur = vc_ref[r0:r1, :]
        if s == 0:
            k_prev, v_prev = kp_ref[...], vp_ref[...]
        else:
            k_prev, v_prev = kc_ref[r0 - QB:r0, :], vc_ref[r0 - QB:r0, :]
        lg = jnp.concatenate([_dot_nt(qs, k_prev), _dot_nt(qs, k_cur)], axis=1) + bias
        if s == 0:
            lg = lg + pen
        m = jnp.maximum(jnp.max(lg, axis=-1, keepdims=True), sink)
        p = jnp.exp(lg - m)
        denom = jnp.sum(p, axis=-1, keepdims=True) + jnp.exp(sink - m)
        pv = _dot(p[:, :QB].astype(BF16), v_prev) + _dot(p[:, QB:].astype(BF16), v_cur)
        o_ref[r0:r1, :] = _unstack_heads(pv / denom)


def _swa(qa, ka, va, bias, sink_rows, bsz, seq):
    n = qa.shape[0]
    nq = seq // SWA_TQ
    per = SWA_TQ // QB
    cur = lambda b, i: (b * nq + i, 0)
    prev = lambda b, i: (jnp.maximum((b * nq + i) * per - 1, 0), 0)
    const = lambda b, i: (0, 0)
    return pl.pallas_call(
        _swa_kernel,
        grid=(bsz, nq),
        in_specs=[pl.BlockSpec((SWA_TQ, 256), cur),
                  pl.BlockSpec((SWA_TQ, LANES), cur), pl.BlockSpec((QB, LANES), prev),
                  pl.BlockSpec((SWA_TQ, LANES), cur), pl.BlockSpec((QB, LANES), prev),
                  pl.BlockSpec(bias.shape, const), pl.BlockSpec(sink_rows.shape, const)],
        out_specs=pl.BlockSpec((SWA_TQ, 256), cur),
        out_shape=jax.ShapeDtypeStruct((n, 256), F32),
        compiler_params=_cparams("parallel", "parallel"),
        name="swa",
    )(qa, ka, ka, va, va, bias, sink_rows)


def _chunk_rows(u_ref):
    return jnp.concatenate([u_ref[:, t, :] for t in range(SSM_CHUNK)], axis=1)


def _ssm_z_kernel(u_ref, pz_ref, zre_ref, zim_ref):
    z = _dot(_chunk_rows(u_ref).astype(BF16), pz_ref[0])
    half = z.shape[1] // 2
    zre_ref[...] = z[:, :half]
    zim_ref[...] = z[:, half:]


def _u_spec(tnc):
    return pl.BlockSpec((None, tnc, SSM_CHUNK, LANES), lambda j, b, r: (b, r, 0, j))


def _state_spec(tnc, sw, noct):
    return pl.BlockSpec((tnc, sw), lambda j, b, r: (r, b * noct + j))


def _ssm_z(u4, pz, tnc):
    bsz, nch = u4.shape[:2]
    noct = pz.shape[0]
    sw = pz.shape[2] // 2
    state = _state_spec(tnc, sw, noct)
    return pl.pallas_call(
        _ssm_z_kernel,
        grid=(noct, bsz, nch // tnc),
        in_specs=[_u_spec(tnc), pl.BlockSpec((1,) + pz.shape[1:], lambda j, b, r: (j, 0, 0))],
        out_specs=[state, state],
        out_shape=[jax.ShapeDtypeStruct((nch, bsz * noct * sw), F32)] * 2,
        compiler_params=_cparams("parallel", "parallel", "parallel"),
        name="ssm_chunk_state",
    )(u4, pz)


def _ssm_scan_kernel(zre_ref, zim_ref, ar_ref, ai_ref, sre_ref, sim_ref):
    a_r = ar_ref[...]
    a_i = ai_ref[...]

    def body(c, carry):
        s_r, s_i = carry
        row = pl.ds(c, 1)
        sre_ref[row, :] = s_r
        sim_ref[row, :] = s_i
        return (a_r * s_r - a_i * s_i + zre_ref[row, :], a_r * s_i + a_i * s_r + zim_ref[row, :])

    zero = jnp.zeros(a_r.shape, F32)
    lax.fori_loop(0, zre_ref.shape[0], body, (zero, zero))


def _ssm_scan(zre, zim, a_r, a_i, tl):
    nchunk, width = zre.shape
    blk = pl.BlockSpec((nchunk, tl), lambda j: (0, j))
    coef = pl.BlockSpec((1, tl), lambda j: (0, j))
    return pl.pallas_call(
        _ssm_scan_kernel,
        grid=(width // tl,),
        in_specs=[blk, blk, coef, coef],
        out_specs=[blk, blk],
        out_shape=[jax.ShapeDtypeStruct(zre.shape, F32)] * 2,
        compiler_params=_cparams("parallel"),
        name="ssm_scan",
    )(zre, zim, a_r, a_i)


def _ssm_y_kernel(u_ref, sre_ref, sim_ref, tz_ref, cre_ref, cim_ref, d_ref, y_ref):
    v = _chunk_rows(u_ref)
    vb = v.astype(BF16)
    s_re = sre_ref[...].astype(BF16)
    s_im = sim_ref[...].astype(BF16)
    w = 2 * LANES
    for tt in range(SSM_CHUNK // 2):
        c0, c1 = tt * w, (tt + 1) * w
        acc = _dot(s_re, cre_ref[0, :, c0:c1]) + _dot(s_im, cim_ref[0, :, c0:c1]) + d_ref[0, :, c0:c1] * v[:, c0:c1]
        for ss in range(tt + 1):
            acc = acc + _dot(vb[:, ss * w:(ss + 1) * w], tz_ref[0, tt - ss])
        y_ref[:, 2 * tt, :] = acc[:, :LANES]
        y_ref[:, 2 * tt + 1, :] = acc[:, LANES:]


def _ssm_y(u4, sre, sim, mm, cre, cim, dvec, tnc):
    bsz, nch = u4.shape[:2]
    noct, sw = cre.shape[:2]
    per_oct = lambda a: pl.BlockSpec((1,) + a.shape[1:], lambda j, b, r: (j,) + (0,) * (a.ndim - 1))
    state = _state_spec(tnc, sw, noct)
    return pl.pallas_call(
        _ssm_y_kernel,
        grid=(noct, bsz, nch // tnc),
        in_specs=[_u_spec(tnc), state, state, per_oct(mm), per_oct(cre), per_oct(cim), per_oct(dvec)],
        out_specs=_u_spec(tnc),
        out_shape=jax.ShapeDtypeStruct(u4.shape, F32),
        compiler_params=_cparams("parallel", "parallel", "parallel"),
        name="ssm_output",
    )(u4, sre, sim, mm, cre, cim, dvec)


def _ssm_operators(a_re, a_im, log_dt, b_re, b_im, c_re, c_im, d_skip):
    g, n = a_re.shape
    p = SSM_GROUP_CH
    t = SSM_CHUNK
    o = SSM_OCT
    noct = g // o
    a = lax.complex(a_re.astype(F32), a_im.astype(F32))
    adt = a * jnp.exp(log_dt.astype(F32))[:, None]
    b_bar = ((jnp.exp(adt) - 1.0) / a)[..., None] * lax.complex(b_re.astype(F32), b_im.astype(F32))
    cm = lax.complex(c_re.astype(F32), c_im.astype(F32))
    pw = jnp.exp(adt[None] * jnp.arange(t + 1, dtype=F32)[:, None, None].astype(jnp.complex64))
    kern = jnp.einsum('gpn,tgn,gnq->tgpq', cm, pw[:t], b_bar, precision=HIGHEST).real
    bz = (pw[:t][::-1][:, :, :, None] * b_bar[None]).transpose(1, 0, 3, 2)
    cz = (cm[None] * pw[1:, :, None, :]).transpose(1, 3, 0, 2)
    eye = jnp.eye(o, dtype=F32)
    oct_ = lambda x: x.reshape((noct, o) + x.shape[1:])
    dlag = jnp.einsum('ljapq,ab->ljaqbp', kern.reshape(t, noct, o, p, p), eye).reshape(t, noct, o * p, o * p)
    dlag = jnp.concatenate([jnp.zeros_like(dlag[:1]), dlag], axis=0)

    def pair_block(dl):
        top = jnp.concatenate([dlag[2 * dl + 1], dlag[2 * dl + 2]], axis=-1)
        bot = jnp.concatenate([dlag[2 * dl], dlag[2 * dl + 1]], axis=-1)
        return jnp.concatenate([top, bot], axis=-2)

    m_oct = jnp.stack([pair_block(dl) for dl in range(t // 2)], axis=1)
    pz_part = lambda x: jnp.einsum('jasqn,ab->jsaqbn', oct_(x), eye).reshape(noct, t * o * p, o * n)
    c_part = lambda x: jnp.einsum('jantp,ab->jantbp', oct_(x), eye).reshape(noct, o * n, t * o * p)
    pz = jnp.concatenate([pz_part(bz.real), pz_part(bz.imag)], axis=2)
    dvec = jnp.broadcast_to(d_skip.astype(F32).reshape(noct, 1, o, p), (noct, t, o, p)).reshape(noct, 1, t * o * p)
    a_chunk = pw[t].reshape(1, -1)
    return (m_oct.astype(BF16), pz.astype(BF16), c_part(cz.real).astype(BF16), c_part(-cz.imag).astype(BF16),
            dvec, a_chunk.real, a_chunk.imag)


def _s5(u, ops, bsz, seq):
    m_oct, pz, c_re, c_im, dvec, a_r, a_i = ops
    nch = seq // SSM_CHUNK
    u4 = u.reshape(bsz, nch, SSM_CHUNK, u.shape[1])
    tnc = min(nch, 256)
    zre, zim = _ssm_z(u4, pz, tnc)
    coef = lambda c: jnp.tile(c, (1, bsz))
    sre, sim = _ssm_scan(zre, zim, coef(a_r), coef(a_i), 2048)
    return _ssm_y(u4, sre, sim, m_oct, c_re, c_im, dvec, tnc).reshape(u.shape)


def _cmp_kernel(kvc_ref, pos_ref, w1t_ref, w1b_ref, w2_ref, gk_ref, kk_ref, vv_ref):
    rows = kvc_ref.shape[0] // NSA_CMP_STRIDE
    top = jnp.zeros((rows, w1t_ref.shape[2]), F32)
    bot = top
    for tau in range(NSA_CMP_STRIDE):
        tok = kvc_ref[pl.ds(tau, rows, stride=NSA_CMP_STRIDE), :]
        top = top + _dot((tok + pos_ref[tau:tau + 1, :]).astype(BF16), w1t_ref[tau])
        bot = bot + _dot((tok + pos_ref[NSA_CMP_STRIDE + tau:NSA_CMP_STRIDE + tau + 1, :]).astype(BF16), w1b_ref[tau])
    hid = top + pltpu.roll(bot, rows - 1, 0)
    out = _dot(jax.nn.gelu(hid).astype(BF16), w2_ref[...])
    k = out[:, :LANES]
    ms = jnp.mean(k * k, axis=-1, keepdims=True)
    kk_ref[0] = (k * lax.rsqrt(ms + EPS) * gk_ref[...]).astype(BF16)
    vv_ref[0] = out[:, LANES:].astype(BF16)


def _nsa_compress(kvc, pos, w1t, w1b, w2, gk, bsz, seq):
    rows = seq // NSA_CMP_STRIDE
    full = lambda a: pl.BlockSpec(a.shape, lambda b: (0,) * a.ndim)
    out = pl.BlockSpec((1, rows, LANES), lambda b: (b, 0, 0))
    return pl.pallas_call(
        _cmp_kernel,
        grid=(bsz,),
        in_specs=[pl.BlockSpec((seq, LANES), lambda b: (b, 0)), full(pos), full(w1t), full(w1b), full(w2), full(gk)],
        out_specs=[out, out],
        out_shape=[jax.ShapeDtypeStruct((bsz, rows, LANES), BF16)] * 2,
        compiler_params=_cparams("parallel"),
        name="nsa_compress",
    )(kvc, pos, w1t, w1b, w2, gk)


def _prep_compress(cmp_pos, cmp_w1, cmp_w2):
    eye = jnp.eye(2, dtype=F32)
    pos = cmp_pos.astype(F32).transpose(1, 0, 2).reshape(NSA_CMP_BLOCK, 2 * HEAD_DIM)
    w1 = cmp_w1.astype(F32).reshape(2, NSA_CMP_BLOCK, HEAD_DIM, -1)
    w1 = jnp.einsum('spdh,sz->psdzh', w1, eye).reshape(NSA_CMP_BLOCK, 2 * HEAD_DIM, -1).astype(BF16)
    w2 = jnp.concatenate([cmp_w2, cmp_w2], axis=-1).astype(F32)
    w2 = jnp.einsum('shd,sz->shzd', w2, eye).reshape(2 * w2.shape[1], 2 * LANES).astype(BF16)
    return pos, w1[:NSA_CMP_STRIDE], w1[NSA_CMP_STRIDE:], w2


def _gate_lanes(gc, gexp):
    return _dot_split(jax.nn.sigmoid(gc), gexp)


def _nsa_cmp_kernel(n_sel_blocks, q_ref, kk_ref, vv_ref, band_ref, gc_ref, ovl_ref, gexp_ref,
                    o_ref, nsel_ref):
    bi = pl.program_id(0)
    qs = _stack_heads(q_ref[...])
    band = band_ref[bi % CMP_PHASES]
    far = band[:, 0:1]

    def attend(nt):
        def run():
            near = band if nt > 1 else band[:, LANES:]
            bias = jnp.concatenate([jnp.broadcast_to(far, (4 * QB, (nt - 2) * LANES)), near], axis=1) if nt > 2 else near
            lg = _dot_nt(qs, kk_ref[0, :nt * LANES, :]) + bias
            valid = lg > 0.5 * NEG_INF
            m = jnp.max(lg, axis=-1, keepdims=True)
            p = jnp.where(valid, jnp.exp(lg - m), 0.0)
            denom = jnp.sum(p, axis=-1, keepdims=True)
            p = p / jnp.where(denom > 0.0, denom, 1.0)
            o_rows = _dot(p.astype(BF16), vv_ref[0, :nt * LANES, :])
            p_sum = p[0:QB] + p[QB:2 * QB] + p[2 * QB:3 * QB] + p[3 * QB:4 * QB]
            return o_rows, _dot_split(p_sum, ovl_ref[:nt * LANES, :])
        return run

    o_rows, imp = lax.switch(bi // CMP_PHASES, [attend(nt) for nt in range(1, kk_ref.shape[1] // LANES + 1)])
    gates = _gate_lanes(gc_ref[...], gexp_ref[...])
    o_ref[...] = gates[:, :NSA_WIDTH] * _unstack_heads(o_rows)
    blk = lax.broadcasted_iota(jnp.int32, (LANES, QB), 0)
    cur = (bi * QB + lax.broadcasted_iota(jnp.int32, (LANES, QB), 1)) // NSA_SEL_BLOCK
    forced = (blk == 0) | (blk == cur) | (blk == cur - 1)
    st = jnp.where(forced, -3.4e38, jnp.where(blk > cur, -SEL_FORCE, imp.T))
    st = jnp.where(blk < n_sel_blocks, st, -3e38)
    jidx = blk.astype(F32)
    unsel = jnp.where(forced, 0.0, 1.0)
    for _ in range(min(NSA_TOP_N, n_sel_blocks) - 3):
        mx = jnp.max(st, axis=0, keepdims=True)
        first = jnp.min(jnp.where(st == mx, jidx, 1e9), axis=0, keepdims=True)
        pick = jidx == first
        unsel = jnp.where(pick, 0.0, unsel)
        st = jnp.where(pick, -3.4e38, st)
    nsel_ref[...] = unsel.T.astype(BF16)


def _nsa_cmp(qc, kk, vv, band, gc, ovl, gexp, bsz, seq):
    n = qc.shape[0]
    nb = seq // QB
    m = kk.shape[1]
    tok = lambda w: pl.BlockSpec((QB, w), lambda bi, b: (b * nb + bi, 0))
    seqblk = pl.BlockSpec((1, m, LANES), lambda bi, b: (b, 0, 0))
    const = lambda a: pl.BlockSpec(a.shape, lambda bi, b: (0,) * a.ndim)
    return pl.pallas_call(
        functools.partial(_nsa_cmp_kernel, seq // NSA_SEL_BLOCK),
        grid=(nb, bsz),
        in_specs=[tok(256), seqblk, seqblk, const(band), tok(LANES), const(ovl), const(gexp)],
        out_specs=[tok(256), tok(LANES)],
        out_shape=[jax.ShapeDtypeStruct((n, 256), F32), jax.ShapeDtypeStruct((n, LANES), BF16)],
        compiler_params=_cparams("parallel", "parallel"),
        name="nsa_compressed_topn",
    )(qc, kk, vv, band, gc, ovl, gexp)


SEL_TK = 512
WIN_TILES = NSA_WINDOW // QB + 1


def _lag_bias(tbl_ref, bi, first_tile, n_tiles):
    last = tbl_ref.shape[0] - 2
    return jnp.concatenate([tbl_ref[jnp.clip(bi - (first_tile + i), -1, last) + 1] for i in range(n_tiles)], axis=1)


SEL_RC = 64


def _nsa_attn_kernel(q_ref, nsel_ref, ocmp_ref, gc_ref, ks_ref, e_ref, vs_ref, kw_ref, vw_ref,
                     tsel_ref, twin_ref, gexp_ref, o_ref, qaug_ref, lga_ref, lgb_ref, acc_ref):
    bi = pl.program_id(1)
    qs = _stack_heads(q_ref[...])
    nsel = nsel_ref[...]
    qaug_ref[...] = jnp.concatenate([qs, jnp.concatenate([nsel] * NSA_HEADS, axis=0)], axis=1)
    per = SEL_TK // QB
    n_tiles = ks_ref.shape[0] // SEL_TK

    def key_rows(kt):
        return pl.ds(pl.multiple_of(jnp.minimum(kt, n_tiles - 1) * SEL_TK, SEL_TK), SEL_TK)

    def sel_logits(kt, lg_ref):
        k_aug = jnp.concatenate([ks_ref[key_rows(kt), :], e_ref[key_rows(kt), :]], axis=1)
        lg_ref[...] = _dot_nt(qaug_ref[...], k_aug) + _lag_bias(tsel_ref, bi, kt * per, per)

    def softmax_pv(kt, lg_ref, m):
        ps, ms, alphas = [], [], []
        for c in range(4 * QB // SEL_RC):
            rows = slice(c * SEL_RC, (c + 1) * SEL_RC)
            lg = lg_ref[rows, :]
            m_new = jnp.maximum(m[rows], jnp.max(lg, axis=-1, keepdims=True))
            ms.append(m_new)
            alphas.append(jnp.exp2(m[rows] - m_new))
            ps.append(jnp.exp2(lg - m_new).astype(BF16))
        pv = _dot(jnp.concatenate(ps, axis=0), _with_ones(vs_ref[key_rows(kt), :]))
        acc_ref[...] = jnp.concatenate(alphas, axis=0) * acc_ref[...] + pv
        return jnp.concatenate(ms, axis=0)

    def sel_body(j, m):
        kt = 2 * j
        sel_logits(kt + 1, lgb_ref)
        m = softmax_pv(kt, lga_ref, m)
        sel_logits(kt + 2, lga_ref)
        return softmax_pv(kt + 1, lgb_ref, m)

    acc_ref[...] = jnp.zeros(acc_ref.shape, F32)
    sel_logits(0, lga_ref)
    lax.fori_loop(0, bi // (2 * per) + 1, sel_body, jnp.full((4 * QB, 1), NEG_INF, F32))
    acc_s = acc_ref[...]

    first = jnp.maximum(bi - (WIN_TILES - 1), 0)
    r = pl.multiple_of(first * QB, QB)
    lg = _dot_nt(qs, kw_ref[pl.ds(r, WIN_TILES * QB), :]) + _lag_bias(twin_ref, bi, first, WIN_TILES)
    p = jnp.exp2(lg - jnp.max(lg, axis=-1, keepdims=True)).astype(BF16)
    acc_w = _dot(p, _with_ones(vw_ref[pl.ds(r, WIN_TILES * QB), :]))

    gates = _gate_lanes(gc_ref[...], gexp_ref[...])
    o_sel = _unstack_heads(acc_s[:, :LANES] / acc_s[:, LANES:])
    o_win = _unstack_heads(acc_w[:, :LANES] / acc_w[:, LANES:])
    o_ref[...] = (ocmp_ref[...] + gates[:, NSA_WIDTH:2 * NSA_WIDTH] * o_sel
                  + gates[:, 2 * NSA_WIDTH:] * o_win)


def _nsa_attn(qc, nsel, ocmp, gc, ks, e_pen, vs, kw, vw, tsel, twin, gexp, bsz, seq):
    n = qc.shape[0]
    nb = seq // QB
    tok = lambda w: pl.BlockSpec((QB, w), lambda b, bi: (b * nb + bi, 0))
    seqblk = pl.BlockSpec((seq, LANES), lambda b, bi: (b, 0))
    const = lambda a: pl.BlockSpec(a.shape, lambda b, bi: (0,) * a.ndim)
    return pl.pallas_call(
        _nsa_attn_kernel,
        grid=(bsz, nb),
        in_specs=[tok(256), tok(LANES), tok(256), tok(LANES),
                  seqblk, const(e_pen), seqblk, seqblk, seqblk,
                  const(tsel), const(twin), const(gexp)],
        out_specs=tok(256),
        out_shape=jax.ShapeDtypeStruct((n, 256), F32),
        scratch_shapes=[pltpu.VMEM((4 * QB, 2 * LANES), BF16), pltpu.VMEM((4 * QB, SEL_TK), F32),
                        pltpu.VMEM((4 * QB, SEL_TK), F32), pltpu.VMEM((4 * QB, 2 * LANES), F32)],
        compiler_params=_cparams("parallel", "arbitrary"),
        name="nsa_selected_window",
    )(qc, nsel, ocmp, gc, ks, e_pen, vs, kw, vw, tsel, twin, gexp)


def _rms(x, gain):
    return x * lax.rsqrt(jnp.mean(x * x, axis=-1, keepdims=True) + EPS) * gain


def _out_kernel(x_ref, oa_ref, y_ref, oc_ref, gluw_ref, glub_ref, gout_ref, wout_ref, o_ref):
    wb = y_ref.shape[1]
    ab = _dot(jax.nn.gelu(y_ref[...]).astype(BF16), gluw_ref[...]) + glub_ref[...]
    ob = ab[:, :wb] * jax.nn.sigmoid(ab[:, wb:])
    a0, a1 = SWA_WIDTH, SWA_WIDTH + wb
    acc = x_ref[...]
    acc = acc + _dot(_rms(oa_ref[...], gout_ref[:, :a0]).astype(BF16), wout_ref[:a0, :])
    acc = acc + _dot(_rms(ob, gout_ref[:, a0:a1]).astype(BF16), wout_ref[a0:a1, :])
    acc = acc + _dot(_rms(oc_ref[...], gout_ref[:, a1:]).astype(BF16), wout_ref[a1:, :])
    o_ref[...] = acc


def _out_proj(x2, oa, y, oc, gluw, glub, gout, wout, tm):
    n, d = x2.shape
    tok = lambda w: pl.BlockSpec((tm, w), lambda i: (i, 0))
    full = lambda a: pl.BlockSpec(a.shape, lambda i: (0,) * a.ndim)
    return pl.pallas_call(
        _out_kernel,
        grid=(n // tm,),
        in_specs=[tok(d), tok(oa.shape[1]), tok(y.shape[1]), tok(oc.shape[1]),
                  full(gluw), full(glub), full(gout), full(wout)],
        out_specs=tok(d),
        out_shape=jax.ShapeDtypeStruct((n, d), F32),
        compiler_params=_cparams("parallel"),
        name="out_proj",
    )(x2, oa, y, oc, gluw, glub, gout, wout)


FFN_CHUNK = 1024


def _ffn_kernel(x_ref, g_ref, wu_ref, wd_ref, o_ref):
    x = x_ref[...]
    h = _rms(x, g_ref[...]).astype(BF16)
    acc = x
    for c in range(wu_ref.shape[1] // FFN_CHUNK):
        c0, c1 = c * FFN_CHUNK, (c + 1) * FFN_CHUNK
        hid = jnp.maximum(_dot(h, wu_ref[:, c0:c1]), 0.0)
        acc = acc + _dot((hid * hid).astype(BF16), wd_ref[c0:c1, :])
    o_ref[...] = acc


def _ffn(x2, g2, wu, wd, tm):
    n, d = x2.shape
    full = lambda a: pl.BlockSpec(a.shape, lambda i: (0,) * a.ndim)
    return pl.pallas_call(
        _ffn_kernel,
        grid=(n // tm,),
        in_specs=[pl.BlockSpec((tm, d), lambda i: (i, 0)), full(g2), full(wu), full(wd)],
        out_specs=pl.BlockSpec((tm, d), lambda i: (i, 0)),
        out_shape=jax.ShapeDtypeStruct((n, d), F32),
        compiler_params=_cparams("parallel"),
        name="ffn",
    )(x2, g2, wu, wd)


def _rel_bucket(dist):
    n = jnp.maximum(dist, 0)
    nf = jnp.maximum(n, 1).astype(F32)
    large = BUCKET_EXACT + (jnp.log(nf / BUCKET_EXACT) / math.log(BUCKET_MAX_DIST / BUCKET_EXACT)
                            * (NUM_BUCKETS - BUCKET_EXACT)).astype(jnp.int32)
    return jnp.where(n < BUCKET_EXACT, n, jnp.minimum(large, NUM_BUCKETS - 1))


def _bias_rows(tbl, dist, valid):
    onehot = jax.nn.one_hot(_rel_bucket(dist), NUM_BUCKETS, dtype=F32)
    b = jnp.einsum('...qkc,ch->...hqk', onehot, tbl.astype(F32), precision=HIGHEST)
    b = jnp.where(valid[..., None, :, :], b, NEG_INF)
    return b.reshape(b.shape[:-3] + (b.shape[-3] * b.shape[-2], b.shape[-1]))


def _bias_tables(rel_bias):
    swa_order = jnp.array([0, 2, 1, 3])
    tbl_a = rel_bias[:, :SWA_HEADS][:, swa_order]
    tbl_c = rel_bias[:, SWA_HEADS:]
    i = jnp.arange(QB)[:, None]
    d_swa = i - jnp.arange(2 * QB)[None, :] + QB
    bias_swa = _bias_rows(tbl_a, d_swa, (d_swa >= 0) & (d_swa < SWA_WINDOW))
    r = jnp.arange(CMP_PHASES)[:, None, None]
    c = jnp.arange(2 * LANES)[None, None, :]
    d_cmp = r * QB + i[None] - (c - LANES) * NSA_CMP_STRIDE - (NSA_CMP_BLOCK - 1)
    band_cmp = _bias_rows(tbl_c, d_cmp, d_cmp >= 0)
    j = jnp.arange(QB)[None, :]
    n_far = -(-(BUCKET_MAX_DIST + QB) // QB)
    d_sel = jnp.arange(n_far)[:, None, None] * QB + (i - j)[None]
    tsel = _bias_rows(tbl_c, d_sel, d_sel >= 0)
    d_win = jnp.arange(WIN_TILES)[:, None, None] * QB + (i - j)[None]
    twin = _bias_rows(tbl_c, d_win, (d_win >= 0) & (d_win < NSA_WINDOW))
    future = jnp.full((1,) + tsel.shape[1:], NEG_INF, F32)
    return bias_swa, band_cmp, jnp.concatenate([future, tsel * LOG2E]), jnp.concatenate([future, twin * LOG2E])


def _nsa_constants(seq):
    m = seq // NSA_CMP_STRIDE
    cs = jnp.arange(m)[:, None] * NSA_CMP_STRIDE
    ss = jnp.arange(LANES)[None, :] * NSA_SEL_BLOCK
    ovl = ((cs < ss + NSA_SEL_BLOCK) & (cs + NSA_CMP_BLOCK > ss) & (cs < seq - NSA_CMP_STRIDE)).astype(BF16)
    key_blk = jnp.arange(seq)[:, None] // NSA_SEL_BLOCK
    e_pen = jnp.where(key_blk == jnp.arange(LANES)[None, :], SEL_PENALTY, 0.0).astype(BF16)
    col = jnp.arange(LANES)[:, None]
    lane = jnp.arange(NSA_BRANCHES * NSA_WIDTH)[None, :]
    br, hd = lane // NSA_WIDTH, (lane % NSA_WIDTH) // HEAD_DIM
    gexp = (col == hd * NSA_BRANCHES + br).astype(BF16)
    return ovl, e_pen, gexp


def _seg_matrix():
    r = jnp.arange(2 * LANES)
    return jnp.where((r[:, None] // HEAD_DIM) == (r[None, :] // HEAD_DIM), 1.0 / HEAD_DIM, 0.0).astype(BF16)


def _prep_w_in(w):
    d = w.shape[0]
    hd = HEAD_DIM
    ssm_w = d - SWA_WIDTH - NSA_WIDTH
    o_ka = SWA_WIDTH
    o_va = o_ka + SWA_KV_WIDTH
    o_u = o_va + SWA_KV_WIDTH
    o_qc = o_u + ssm_w
    o_kv = o_qc + NSA_WIDTH
    o_gc = o_kv + 6 * hd
    head = lambda off, h: w[:, off + h * hd: off + (h + 1) * hd]
    gates = w[:, o_gc:]
    cols = [head(0, 0), head(0, 2), head(0, 1), head(0, 3), w[:, o_ka:o_gc],
            gates, jnp.zeros((d, LANES - gates.shape[1]), w.dtype)]
    return jnp.concatenate(cols, axis=1).astype(BF16)


def _prep_qk_gains(qk_g):
    g = qk_g.astype(F32)
    t4 = lambda v: jnp.tile(v, 4)
    return jnp.stack([t4(g[0]), t4(g[1]), t4(g[2]), t4(g[4]), t4(g[5]),
                      t4(g[3]), t4(g[3]), t4(g[3])])


def kernel(x, norm1_g, w_in, qk_g, sinks, rel_bias, ssm_a_re, ssm_a_im, ssm_log_dt, ssm_b_re, ssm_b_im, ssm_c_re, ssm_c_im, ssm_d, glu_w, glu_b, cmp_pos, cmp_w1, cmp_w2, out_norm_g, w_out, norm2_g, w_up, w_down):
    bsz, seq, d = x.shape
    depth = w_in.shape[0]
    n = bsz * seq
    assert seq % SWA_TQ == 0 and seq // NSA_SEL_BLOCK <= LANES and d == 1024
    assert (seq // NSA_CMP_STRIDE) % LANES == 0 and seq % SEL_TK == 0 and seq >= WIN_TILES * QB
    tm = 512
    row = lambda v: v.astype(F32).reshape(1, -1)

    bias_swa, band_cmp, tsel, twin = _bias_tables(rel_bias)
    ovl, e_pen, gexp = _nsa_constants(seq)
    seg = _seg_matrix()
    swa_order = jnp.array([0, 2, 1, 3])
    swa_cols = (swa_order[:, None] * HEAD_DIM + jnp.arange(HEAD_DIM)[None, :]).reshape(-1)

    x2 = x.reshape(n, d)
    for l in range(depth):
        qkg = _prep_qk_gains(qk_g[l])
        (qa, ka, va, u, qc, kvc, ks, vs, kw, vw, gc) = _in_proj(
            x2, row(norm1_g[l]), _prep_w_in(w_in[l]), qkg, seg, tm)

        sink_rows = jnp.repeat(sinks[l].astype(F32)[swa_order], QB).reshape(-1, 1)
        o_a = _swa(qa, ka, va, bias_swa, sink_rows, bsz, seq)

        ops = _ssm_operators(ssm_a_re[l], ssm_a_im[l], ssm_log_dt[l], ssm_b_re[l], ssm_b_im[l],
                             ssm_c_re[l], ssm_c_im[l], ssm_d[l])
        y = _s5(u, ops, bsz, seq)

        pos, w1t, w1b, w2 = _prep_compress(cmp_pos[l], cmp_w1[l], cmp_w2[l])
        kk, vv = _nsa_compress(kvc, pos, w1t, w1b, w2, qkg[5:6, :LANES], bsz, seq)
        o_cmp, nsel = _nsa_cmp(qc, kk, vv, band_cmp, gc, ovl, gexp, bsz, seq)
        o_c = _nsa_attn(qc, nsel, o_cmp, gc, ks, e_pen, vs, kw, vw, tsel, twin, gexp, bsz, seq)

        gout = out_norm_g[l].astype(F32)
        gout = jnp.concatenate([gout[:SWA_WIDTH][swa_cols], gout[SWA_WIDTH:]]).reshape(1, -1)
        wout = jnp.concatenate([w_out[l][:SWA_WIDTH][swa_cols], w_out[l][SWA_WIDTH:]], axis=0).astype(BF16)
        x2 = _out_proj(x2, o_a, y, o_c, glu_w[l].astype(BF16), row(glu_b[l]), gout, wout, tm)

        x2 = _ffn(x2, row(norm2_g[l]), w_up[l].astype(BF16), w_down[l].astype(BF16), tm)
    return x2.reshape(bsz, seq, d)
```

```python
import functools
import math

import jax
import jax.numpy as jnp
from jax import lax
from jax.experimental import pallas as pl
from jax.experimental.pallas import tpu as pltpu

F32 = jnp.float32
BF16 = jnp.bfloat16
HIGHEST = lax.Precision.HIGHEST

HEAD_DIM = 64
SWA_HEADS = 4
SWA_KV_HEADS = 2
SWA_WINDOW = 128
NSA_HEADS = 4
NSA_CMP_BLOCK = 32
NSA_CMP_STRIDE = 16
NSA_SEL_BLOCK = 64
NSA_TOP_N = 16
NSA_WINDOW = 512
NSA_BRANCHES = 3
SSM_GROUP_CH = 16
SSM_STATE = 64
NUM_BUCKETS = 32
BUCKET_EXACT = NUM_BUCKETS // 2
BUCKET_MAX_DIST = 1024
EPS = 1e-6
NEG_INF = -1e30
SEL_FORCE = 1e4
SEL_PENALTY = -30000.0
LOG2E = math.log2(math.e)

LANES = 128
QB = 128
SSM_CHUNK = 16
SSM_OCT = LANES // SSM_GROUP_CH
CMP_PHASES = LANES * NSA_CMP_STRIDE // QB
VMEM_LIMIT = 56 * 1024 * 1024

SWA_WIDTH = SWA_HEADS * HEAD_DIM
SWA_KV_WIDTH = SWA_KV_HEADS * HEAD_DIM
NSA_WIDTH = NSA_HEADS * HEAD_DIM


def _cparams(*sem):
    return pltpu.CompilerParams(dimension_semantics=sem, vmem_limit_bytes=VMEM_LIMIT)


def _dot(a, b):
    return jnp.dot(a, b, preferred_element_type=F32)


def _dot_nt(a, b):
    return lax.dot_general(a, b, (((1,), (1,)), ((), ())), preferred_element_type=F32)


def _dot_split(a, b):
    hi = a.astype(BF16)
    lo = (a - hi.astype(F32)).astype(BF16)
    return _dot(hi, b) + _dot(lo, b)


def _with_ones(v):
    return jnp.concatenate([v, jnp.ones_like(v)], axis=1)


def _lane_lo(rows):
    return lax.broadcasted_iota(jnp.int32, (rows, LANES), 1) < HEAD_DIM


def _stack_heads(q):
    lo = _lane_lo(q.shape[0])
    zero = jnp.zeros_like(q[:, :LANES])
    g0 = q[:, :LANES]
    g1 = q[:, LANES:]
    return jnp.concatenate([jnp.where(lo, g0, zero), jnp.where(lo, zero, g0),
                            jnp.where(lo, g1, zero), jnp.where(lo, zero, g1)], axis=0)


def _unstack_heads(o):
    n = o.shape[0] // 4
    lo = _lane_lo(n)
    return jnp.concatenate([jnp.where(lo, o[0:n], o[n:2 * n]),
                            jnp.where(lo, o[2 * n:3 * n], o[3 * n:4 * n])], axis=1)


def _seg_rms(p, seg, gain):
    ms = _dot_split(p * p, seg)
    return p * lax.rsqrt(ms + EPS) * gain


def _in_proj_kernel(x_ref, g_ref, w_ref, qkg_ref, seg_ref,
                    qa_ref, ka_ref, va_ref, u_ref, qc_ref, kvc_ref,
                    ks_ref, vs_ref, kw_ref, vw_ref, gc_ref):
    x = x_ref[...]
    ms = jnp.mean(x * x, axis=-1, keepdims=True)
    h = (x * lax.rsqrt(ms + EPS) * g_ref[...]).astype(BF16)
    seg = seg_ref[...]
    scale = HEAD_DIM ** -0.5

    def proj(a, b):
        return _dot(h, w_ref[:, a:b])

    lo = _lane_lo(x.shape[0])

    def dup_halves(kv):
        swapped = pltpu.roll(kv, HEAD_DIM, 1)
        return jnp.where(lo, kv, swapped), jnp.where(lo, swapped, kv)

    def dup_rms(p, gain):
        m = jnp.mean(p * p, axis=-1, keepdims=True)
        return p * lax.rsqrt(m + EPS) * (gain * LOG2E)

    qa_ref[...] = (_seg_rms(proj(0, 256), seg, qkg_ref[0:1, :]) * scale).astype(BF16)
    kva = proj(256, 512)
    ka_ref[...] = _seg_rms(kva, seg, qkg_ref[1:2, :])[:, :LANES].astype(BF16)
    va_ref[...] = kva[:, LANES:].astype(BF16)
    u_ref[...] = proj(512, 1024)
    qc_ref[...] = (_seg_rms(proj(1024, 1280), seg, qkg_ref[2:3, :]) * scale).astype(BF16)
    cmp_sel = proj(1280, 1536)
    kvc_ref[...] = cmp_sel[:, :LANES]
    ks, vs = dup_halves(cmp_sel[:, LANES:])
    ks_ref[...] = dup_rms(ks, qkg_ref[3:4, :LANES]).astype(BF16)
    vs_ref[...] = vs.astype(BF16)
    win_gate = proj(1536, 1792)
    kw, vw = dup_halves(win_gate[:, :LANES])
    kw_ref[...] = dup_rms(kw, qkg_ref[4:5, :LANES]).astype(BF16)
    vw_ref[...] = vw.astype(BF16)
    gc_ref[...] = win_gate[:, LANES:]


def _in_proj(x2, g1, w, qkg, seg, tm):
    n, d = x2.shape
    widths = [(256, BF16), (128, BF16), (128, BF16), (512, F32), (256, BF16), (128, F32),
              (128, BF16), (128, BF16), (128, BF16), (128, BF16), (128, F32)]
    full = lambda a: pl.BlockSpec(a.shape, lambda i: (0,) * a.ndim)
    return pl.pallas_call(
        _in_proj_kernel,
        grid=(n // tm,),
        in_specs=[pl.BlockSpec((tm, d), lambda i: (i, 0)), full(g1), full(w), full(qkg), full(seg)],
        out_specs=[pl.BlockSpec((tm, wd), lambda i: (i, 0)) for wd, _ in widths],
        out_shape=[jax.ShapeDtypeStruct((n, wd), dt) for wd, dt in widths],
        compiler_params=_cparams("parallel"),
        name="in_proj",
    )(x2, g1, w, qkg, seg)


SWA_TQ = 512


def _swa_kernel(q_ref, kc_ref, kp_ref, vc_ref, vp_ref, bias_ref, sink_ref, o_ref):
    first = pl.program_id(1) == 0
    prev_cols = lax.broadcasted_iota(jnp.int32, (1, 2 * QB), 1) < QB
    pen = jnp.where(first & prev_cols, NEG_INF, 0.0)
    sink = sink_ref[...]
    bias = bias_ref[...]
    for s in range(SWA_TQ // QB):
        r0, r1 = s * QB, (s + 1) * QB
        qs = _stack_heads(q_ref[r0:r1, :])
        k_cur = kc_ref[r0:r1, :]
        v_cur = vc_ref[r0:r1, :]
        if s == 0:
            k_prev, v_prev = kp_ref[...], vp_ref[...]
        else:
            k_prev, v_prev = kc_ref[r0 - QB:r0, :], vc_ref[r0 - QB:r0, :]
        lg = jnp.concatenate([_dot_nt(qs, k_prev), _dot_nt(qs, k_cur)], axis=1) + bias
        if s == 0:
            lg = lg + pen
        m = jnp.maximum(jnp.max(lg, axis=-1, keepdims=True), sink)
        p = jnp.exp(lg - m)
        denom = jnp.sum(p, axis=-1, keepdims=True) + jnp.exp(sink - m)
        pv = _dot(p[:, :QB].astype(BF16), v_prev) + _dot(p[:, QB:].astype(BF16), v_cur)
        o_ref[r0:r1, :] = _unstack_heads(pv / denom)


def _swa(qa, ka, va, bias, sink_rows, bsz, seq):
    n = qa.shape[0]
    nq = seq // SWA_TQ
    per = SWA_TQ // QB
    cur = lambda b, i: (b * nq + i, 0)
    prev = lambda b, i: (jnp.maximum((b * nq + i) * per - 1, 0), 0)
    const = lambda b, i: (0, 0)
    return pl.pallas_call(
        _swa_kernel,
        grid=(bsz, nq),
        in_specs=[pl.BlockSpec((SWA_TQ, 256), cur),
                  pl.BlockSpec((SWA_TQ, LANES), cur), pl.BlockSpec((QB, LANES), prev),
                  pl.BlockSpec((SWA_TQ, LANES), cur), pl.BlockSpec((QB, LANES), prev),
                  pl.BlockSpec(bias.shape, const), pl.BlockSpec(sink_rows.shape, const)],
        out_specs=pl.BlockSpec((SWA_TQ, 256), cur),
        out_shape=jax.ShapeDtypeStruct((n, 256), F32),
        compiler_params=_cparams("parallel", "parallel"),
        name="swa",
    )(qa, ka, ka, va, va, bias, sink_rows)


def _chunk_rows(u_ref):
    return jnp.concatenate([u_ref[:, t, :] for t in range(SSM_CHUNK)], axis=1)


def _ssm_z_kernel(u_ref, pz_ref, zre_ref, zim_ref):
    z = _dot(_chunk_rows(u_ref).astype(BF16), pz_ref[0])
    half = z.shape[1] // 2
    zre_ref[...] = z[:, :half]
    zim_ref[...] = z[:, half:]


def _u_spec(tnc):
    return pl.BlockSpec((None, tnc, SSM_CHUNK, LANES), lambda j, b, r: (b, r, 0, j))


def _state_spec(tnc, sw, noct):
    return pl.BlockSpec((tnc, sw), lambda j, b, r: (r, b * noct + j))


def _ssm_z(u4, pz, tnc):
    bsz, nch = u4.shape[:2]
    noct = pz.shape[0]
    sw = pz.shape[2] // 2
    state = _state_spec(tnc, sw, noct)
    return pl.pallas_call(
        _ssm_z_kernel,
        grid=(noct, bsz, nch // tnc),
        in_specs=[_u_spec(tnc), pl.BlockSpec((1,) + pz.shape[1:], lambda j, b, r: (j, 0, 0))],
        out_specs=[state, state],
        out_shape=[jax.ShapeDtypeStruct((nch, bsz * noct * sw), F32)] * 2,
        compiler_params=_cparams("parallel", "parallel", "parallel"),
        name="ssm_chunk_state",
    )(u4, pz)


def _ssm_scan_kernel(zre_ref, zim_ref, ar_ref, ai_ref, sre_ref, sim_ref):
    a_r = ar_ref[...]
    a_i = ai_ref[...]

    def body(c, carry):
        s_r, s_i = carry
        row = pl.ds(c, 1)
        sre_ref[row, :] = s_r
        sim_ref[row, :] = s_i
        return (a_r * s_r - a_i * s_i + zre_ref[row, :], a_r * s_i + a_i * s_r + zim_ref[row, :])

    zero = jnp.zeros(a_r.shape, F32)
    lax.fori_loop(0, zre_ref.shape[0], body, (zero, zero))


def _ssm_scan(zre, zim, a_r, a_i, tl):
    nchunk, width = zre.shape
    blk = pl.BlockSpec((nchunk, tl), lambda j: (0, j))
    coef = pl.BlockSpec((1, tl), lambda j: (0, j))
    return pl.pallas_call(
        _ssm_scan_kernel,
        grid=(width // tl,),
        in_specs=[blk, blk, coef, coef],
        out_specs=[blk, blk],
        out_shape=[jax.ShapeDtypeStruct(zre.shape, F32)] * 2,
        compiler_params=_cparams("parallel"),
        name="ssm_scan",
    )(zre, zim, a_r, a_i)


def _ssm_y_kernel(u_ref, sre_ref, sim_ref, tz_ref, cre_ref, cim_ref, d_ref, y_ref):
    v = _chunk_rows(u_ref)
    vb = v.astype(BF16)
    s_re = sre_ref[...].astype(BF16)
    s_im = sim_ref[...].astype(BF16)
    w = 2 * LANES
    for tt in range(SSM_CHUNK // 2):
        c0, c1 = tt * w, (tt + 1) * w
        acc = _dot(s_re, cre_ref[0, :, c0:c1]) + _dot(s_im, cim_ref[0, :, c0:c1]) + d_ref[0, :, c0:c1] * v[:, c0:c1]
        for ss in range(tt + 1):
            acc = acc + _dot(vb[:, ss * w:(ss + 1) * w], tz_ref[0, tt - ss])
        y_ref[:, 2 * tt, :] = acc[:, :LANES]
        y_ref[:, 2 * tt + 1, :] = acc[:, LANES:]


def _ssm_y(u4, sre, sim, mm, cre, cim, dvec, tnc):
    bsz, nch = u4.shape[:2]
    noct, sw = cre.shape[:2]
    per_oct = lambda a: pl.BlockSpec((1,) + a.shape[1:], lambda j, b, r: (j,) + (0,) * (a.ndim - 1))
    state = _state_spec(tnc, sw, noct)
    return pl.pallas_call(
        _ssm_y_kernel,
        grid=(noct, bsz, nch // tnc),
        in_specs=[_u_spec(tnc), state, state, per_oct(mm), per_oct(cre), per_oct(cim), per_oct(dvec)],
        out_specs=_u_spec(tnc),
        out_shape=jax.ShapeDtypeStruct(u4.shape, F32),
        compiler_params=_cparams("parallel", "parallel", "parallel"),
        name="ssm_output",
    )(u4, sre, sim, mm, cre, cim, dvec)


def _ssm_operators(a_re, a_im, log_dt, b_re, b_im, c_re, c_im, d_skip):
    g, n = a_re.shape
    p = SSM_GROUP_CH
    t = SSM_CHUNK
    o = SSM_OCT
    noct = g // o
    a = lax.complex(a_re.astype(F32), a_im.astype(F32))
    adt = a * jnp.exp(log_dt.astype(F32))[:, None]
    b_bar = ((jnp.exp(adt) - 1.0) / a)[..., None] * lax.complex(b_re.astype(F32), b_im.astype(F32))
    cm = lax.complex(c_re.astype(F32), c_im.astype(F32))
    pw = jnp.exp(adt[None] * jnp.arange(t + 1, dtype=F32)[:, None, None].astype(jnp.complex64))
    kern = jnp.einsum('gpn,tgn,gnq->tgpq', cm, pw[:t], b_bar, precision=HIGHEST).real
    bz = (pw[:t][::-1][:, :, :, None] * b_bar[None]).transpose(1, 0, 3, 2)
    cz = (cm[None] * pw[1:, :, None, :]).transpose(1, 3, 0, 2)
    eye = jnp.eye(o, dtype=F32)
    oct_ = lambda x: x.reshape((noct, o) + x.shape[1:])
    dlag = jnp.einsum('ljapq,ab->ljaqbp', kern.reshape(t, noct, o, p, p), eye).reshape(t, noct, o * p, o * p)
    dlag = jnp.concatenate([jnp.zeros_like(dlag[:1]), dlag], axis=0)

    def pair_block(dl):
        top = jnp.concatenate([dlag[2 * dl + 1], dlag[2 * dl + 2]], axis=-1)
        bot = jnp.concatenate([dlag[2 * dl], dlag[2 * dl + 1]], axis=-1)
        return jnp.concatenate([top, bot], axis=-2)

    m_oct = jnp.stack([pair_block(dl) for dl in range(t // 2)], axis=1)
    pz_part = lambda x: jnp.einsum('jasqn,ab->jsaqbn', oct_(x), eye).reshape(noct, t * o * p, o * n)
    c_part = lambda x: jnp.einsum('jantp,ab->jantbp', oct_(x), eye).reshape(noct, o * n, t * o * p)
    pz = jnp.concatenate([pz_part(bz.real), pz_part(bz.imag)], axis=2)
    dvec = jnp.broadcast_to(d_skip.astype(F32).reshape(noct, 1, o, p), (noct, t, o, p)).reshape(noct, 1, t * o * p)
    a_chunk = pw[t].reshape(1, -1)
    return (m_oct.astype(BF16), pz.astype(BF16), c_part(cz.real).astype(BF16), c_part(-cz.imag).astype(BF16),
            dvec, a_chunk.real, a_chunk.imag)


def _s5(u, ops, bsz, seq):
    m_oct, pz, c_re, c_im, dvec, a_r, a_i = ops
    nch = seq // SSM_CHUNK
    u4 = u.reshape(bsz, nch, SSM_CHUNK, u.shape[1])
    tnc = min(nch, 256)
    zre, zim = _ssm_z(u4, pz, tnc)
    coef = lambda c: jnp.tile(c, (1, bsz))
    sre, sim = _ssm_scan(zre, zim, coef(a_r), coef(a_i), 2048)
    return _ssm_y(u4, sre, sim, m_oct, c_re, c_im, dvec, tnc).reshape(u.shape)


def _cmp_kernel(kvc_ref, pos_ref, w1t_ref, w1b_ref, w2_ref, gk_ref, kk_ref, vvt_ref):
    rows = kvc_ref.shape[0] // NSA_CMP_STRIDE
    top = jnp.zeros((rows, w1t_ref.shape[2]), F32)
    bot = top
    for tau in range(NSA_CMP_STRIDE):
        tok = kvc_ref[pl.ds(tau, rows, stride=NSA_CMP_STRIDE), :]
        top = top + _dot((tok + pos_ref[tau:tau + 1, :]).astype(BF16), w1t_ref[tau])
        bot = bot + _dot((tok + pos_ref[NSA_CMP_STRIDE + tau:NSA_CMP_STRIDE + tau + 1, :]).astype(BF16), w1b_ref[tau])
    hid = top + pltpu.roll(bot, rows - 1, 0)
    out = _dot(jax.nn.gelu(hid).astype(BF16), w2_ref[...])
    k = out[:, :LANES]
    ms = jnp.mean(k * k, axis=-1, keepdims=True)
    kk_ref[0] = (k * lax.rsqrt(ms + EPS) * gk_ref[...]).astype(BF16)
    vvt_ref[0] = out[:, LANES:].T[:HEAD_DIM].astype(BF16)


def _nsa_compress(kvc, pos, w1t, w1b, w2, gk, bsz, seq):
    rows = seq // NSA_CMP_STRIDE
    full = lambda a: pl.BlockSpec(a.shape, lambda b: (0,) * a.ndim)
    out = pl.BlockSpec((1, rows, LANES), lambda b: (b, 0, 0))
    return pl.pallas_call(
        _cmp_kernel,
        grid=(bsz,),
        in_specs=[pl.BlockSpec((seq, LANES), lambda b: (b, 0)), full(pos), full(w1t), full(w1b), full(w2), full(gk)],
        out_specs=[out, pl.BlockSpec((1, HEAD_DIM, rows), lambda b: (b, 0, 0))],
        out_shape=[jax.ShapeDtypeStruct((bsz, rows, LANES), BF16), jax.ShapeDtypeStruct((bsz, HEAD_DIM, rows), BF16)],
        compiler_params=_cparams("parallel"),
        name="nsa_compress",
    )(kvc, pos, w1t, w1b, w2, gk)


def _prep_compress(cmp_pos, cmp_w1, cmp_w2):
    eye = jnp.eye(2, dtype=F32)
    pos = cmp_pos.astype(F32).transpose(1, 0, 2).reshape(NSA_CMP_BLOCK, 2 * HEAD_DIM)
    w1 = cmp_w1.astype(F32).reshape(2, NSA_CMP_BLOCK, HEAD_DIM, -1)
    w1 = jnp.einsum('spdh,sz->psdzh', w1, eye).reshape(NSA_CMP_BLOCK, 2 * HEAD_DIM, -1).astype(BF16)
    w2 = jnp.concatenate([cmp_w2, cmp_w2], axis=-1).astype(F32)
    w2 = jnp.einsum('shd,sz->shzd', w2, eye).reshape(2 * w2.shape[1], 2 * LANES).astype(BF16)
    return pos, w1[:NSA_CMP_STRIDE], w1[NSA_CMP_STRIDE:], w2


def _gate_lanes(gc, gexp):
    return _dot_split(jax.nn.sigmoid(gc), gexp)


def _nsa_cmp_kernel(n_sel_blocks, q_ref, kk_ref, vvt_ref, band_ref, gc_ref, ovlt_ref, gexp_ref,
                    o_ref, nsel_ref):
    bi = pl.program_id(0)
    qs = _stack_heads(q_ref[...])
    band = band_ref[bi % CMP_PHASES]
    far = band[0:1, :]

    def attend(nt):
        def run():
            near = band if nt > 1 else band[LANES:]
            bias = jnp.concatenate([jnp.broadcast_to(far, ((nt - 2) * LANES, 4 * QB)), near], axis=0) if nt > 2 else near
            lg = _dot_nt(kk_ref[0, :nt * LANES, :], qs) + bias
            valid = lg > 0.5 * NEG_INF
            m = jnp.max(lg, axis=0, keepdims=True)
            p = jnp.where(valid, jnp.exp(lg - m), 0.0)
            denom = jnp.sum(p, axis=0, keepdims=True)
            p = p * (1.0 / jnp.where(denom > 0.0, denom, 1.0))
            o_t = _dot(vvt_ref[0, :, :nt * LANES], p.astype(BF16))
            p_sum = p[:, 0:QB] + p[:, QB:2 * QB] + p[:, 2 * QB:3 * QB] + p[:, 3 * QB:4 * QB]
            hi = p_sum.astype(BF16)
            lo = (p_sum - hi.astype(F32)).astype(BF16)
            ovl_t = ovlt_ref[:, :nt * LANES]
            return o_t, _dot(ovl_t, hi) + _dot(ovl_t, lo)
        return run

    o_t, imp_t = lax.switch(bi // CMP_PHASES, [attend(nt) for nt in range(1, kk_ref.shape[1] // LANES + 1)])
    o_cmp = jnp.concatenate([o_t[:, h * QB:(h + 1) * QB].T for h in range(NSA_HEADS)], axis=1)
    gates = _gate_lanes(gc_ref[...], gexp_ref[...])
    o_ref[...] = gates[:, :NSA_WIDTH] * o_cmp
    blk = lax.broadcasted_iota(jnp.int32, (LANES, QB), 0)
    cur = (bi * QB + lax.broadcasted_iota(jnp.int32, (LANES, QB), 1)) // NSA_SEL_BLOCK
    forced = (blk == 0) | (blk == cur) | (blk == cur - 1)
    st = jnp.where(forced, -3.4e38, jnp.where(blk > cur, -SEL_FORCE, imp_t))
    st = jnp.where(blk < n_sel_blocks, st, -3e38)
    jidx = blk.astype(F32)
    unsel = jnp.where(forced, 0.0, 1.0)
    for _ in range(min(NSA_TOP_N, n_sel_blocks) - 3):
        mx = jnp.max(st, axis=0, keepdims=True)
        first = jnp.min(jnp.where(st == mx, jidx, 1e9), axis=0, keepdims=True)
        pick = jidx == first
        unsel = jnp.where(pick, 0.0, unsel)
        st = jnp.where(pick, -3.4e38, st)
    nsel_ref[...] = unsel.T.astype(BF16)


def _nsa_cmp(qc, kk, vvt, band, gc, ovlt, gexp, bsz, seq):
    n = qc.shape[0]
    nb = seq // QB
    tok = lambda w: pl.BlockSpec((QB, w), lambda bi, b: (b * nb + bi, 0))
    seqblk = lambda a: pl.BlockSpec((1,) + a.shape[1:], lambda bi, b: (b, 0, 0))
    const = lambda a: pl.BlockSpec(a.shape, lambda bi, b: (0,) * a.ndim)
    return pl.pallas_call(
        functools.partial(_nsa_cmp_kernel, seq // NSA_SEL_BLOCK),
        grid=(nb, bsz),
        in_specs=[tok(256), seqblk(kk), seqblk(vvt), const(band), tok(LANES), const(ovlt), const(gexp)],
        out_specs=[tok(256), tok(LANES)],
        out_shape=[jax.ShapeDtypeStruct((n, 256), F32), jax.ShapeDtypeStruct((n, LANES), BF16)],
        compiler_params=_cparams("parallel", "parallel"),
        name="nsa_compressed_topn",
    )(qc, kk, vvt, band, gc, ovlt, gexp)


SEL_TK = 512
WIN_TILES = NSA_WINDOW // QB + 1
V_ROWS = HEAD_DIM + 16


def _lag_bias(tbl_ref, bi, first_tile, n_tiles):
    last = tbl_ref.shape[0] - 2
    return jnp.concatenate([tbl_ref[jnp.clip(bi - (first_tile + i), -1, last) + 1] for i in range(n_tiles)], axis=0)


def _value_tiles(vt_ref, first_tile, n_tiles):
    return jnp.concatenate([vt_ref[first_tile + i] for i in range(n_tiles)], axis=1)


def _finish_heads(acc):
    o_t = acc[:HEAD_DIM] / acc[HEAD_DIM:HEAD_DIM + 1]
    return jnp.concatenate([o_t[:, h * QB:(h + 1) * QB].T for h in range(NSA_HEADS)], axis=1)


def _nsa_attn_kernel(q_ref, nsel_ref, ocmp_ref, gc_ref, ks_ref, e_ref, vst_ref, kw_ref, vwt_ref,
                     tsel_ref, twin_ref, gexp_ref, o_ref, qaug_ref, lga_ref, lgb_ref, acc_ref):
    bi = pl.program_id(1)
    qs = _stack_heads(q_ref[...])
    nsel = nsel_ref[...]
    qaug_ref[...] = jnp.concatenate([qs, jnp.concatenate([nsel] * NSA_HEADS, axis=0)], axis=1)
    per = SEL_TK // QB
    n_tiles = ks_ref.shape[0] // SEL_TK

    def clamp(kt):
        return jnp.minimum(kt, n_tiles - 1)

    def sel_logits(kt, lg_ref):
        rows = pl.ds(pl.multiple_of(clamp(kt) * SEL_TK, SEL_TK), SEL_TK)
        k_aug = jnp.concatenate([ks_ref[rows, :], e_ref[rows, :]], axis=1)
        lg_ref[...] = _dot_nt(k_aug, qaug_ref[...]) + _lag_bias(tsel_ref, bi, kt * per, per)

    def softmax_pv(kt, lg_ref, m):
        ps, ms, alphas = [], [], []
        for h in range(NSA_HEADS):
            cols = slice(h * QB, (h + 1) * QB)
            lg = lg_ref[:, cols]
            m_new = jnp.maximum(m[:, cols], jnp.max(lg, axis=0, keepdims=True))
            ms.append(m_new)
            alphas.append(jnp.exp2(m[:, cols] - m_new))
            ps.append(jnp.exp2(lg - m_new).astype(BF16))
        pv = _dot(_value_tiles(vst_ref, clamp(kt) * per, per), jnp.concatenate(ps, axis=1))
        acc_ref[...] = jnp.concatenate(alphas, axis=1) * acc_ref[...] + pv
        return jnp.concatenate(ms, axis=1)

    def sel_body(j, m):
        kt = 2 * j
        sel_logits(kt + 1, lgb_ref)
        m = softmax_pv(kt, lga_ref, m)
        sel_logits(kt + 2, lga_ref)
        return softmax_pv(kt + 1, lgb_ref, m)

    acc_ref[...] = jnp.zeros(acc_ref.shape, F32)
    sel_logits(0, lga_ref)
    lax.fori_loop(0, bi // (2 * per) + 1, sel_body, jnp.full((1, 4 * QB), NEG_INF, F32))

    first = jnp.maximum(bi - (WIN_TILES - 1), 0)
    r = pl.multiple_of(first * QB, QB)
    lg = _dot_nt(kw_ref[pl.ds(r, WIN_TILES * QB), :], qs) + _lag_bias(twin_ref, bi, first, WIN_TILES)
    p = jnp.exp2(lg - jnp.max(lg, axis=0, keepdims=True)).astype(BF16)
    acc_w = _dot(_value_tiles(vwt_ref, first, WIN_TILES), p)

    gates = _gate_lanes(gc_ref[...], gexp_ref[...])
    o_ref[...] = (ocmp_ref[...] + gates[:, NSA_WIDTH:2 * NSA_WIDTH] * _finish_heads(acc_ref[...])
                  + gates[:, 2 * NSA_WIDTH:] * _finish_heads(acc_w))


def _value_rows(v, bsz, seq):
    vt = v[:, :HEAD_DIM].reshape(bsz, seq // QB, QB, HEAD_DIM).transpose(0, 1, 3, 2)
    return jnp.concatenate([vt, jnp.ones(vt.shape[:2] + (V_ROWS - HEAD_DIM, QB), vt.dtype)], axis=2)


def _nsa_attn(qc, nsel, ocmp, gc, ks, e_pen, vs, kw, vw, tsel, twin, gexp, bsz, seq):
    n = qc.shape[0]
    nb = seq // QB
    tok = lambda w: pl.BlockSpec((QB, w), lambda b, bi: (b * nb + bi, 0))
    seqblk = pl.BlockSpec((seq, LANES), lambda b, bi: (b, 0))
    valblk = pl.BlockSpec((None, nb, V_ROWS, QB), lambda b, bi: (b, 0, 0, 0))
    const = lambda a: pl.BlockSpec(a.shape, lambda b, bi: (0,) * a.ndim)
    return pl.pallas_call(
        _nsa_attn_kernel,
        grid=(bsz, nb),
        in_specs=[tok(256), tok(LANES), tok(256), tok(LANES),
                  seqblk, const(e_pen), valblk, seqblk, valblk,
                  const(tsel), const(twin), const(gexp)],
        out_specs=tok(256),
        out_shape=jax.ShapeDtypeStruct((n, 256), F32),
        scratch_shapes=[pltpu.VMEM((4 * QB, 2 * LANES), BF16), pltpu.VMEM((SEL_TK, 4 * QB), F32),
                        pltpu.VMEM((SEL_TK, 4 * QB), F32), pltpu.VMEM((V_ROWS, 4 * QB), F32)],
        compiler_params=_cparams("parallel", "arbitrary"),
        name="nsa_selected_window",
    )(qc, nsel, ocmp, gc, ks, e_pen, _value_rows(vs, bsz, seq), kw, _value_rows(vw, bsz, seq), tsel, twin, gexp)


def _rms(x, gain):
    return x * lax.rsqrt(jnp.mean(x * x, axis=-1, keepdims=True) + EPS) * gain


def _out_kernel(x_ref, oa_ref, y_ref, oc_ref, gluw_ref, glub_ref, gout_ref, wout_ref, o_ref):
    wb = y_ref.shape[1]
    ab = _dot(jax.nn.gelu(y_ref[...]).astype(BF16), gluw_ref[...]) + glub_ref[...]
    ob = ab[:, :wb] * jax.nn.sigmoid(ab[:, wb:])
    a0, a1 = SWA_WIDTH, SWA_WIDTH + wb
    acc = x_ref[...]
    acc = acc + _dot(_rms(oa_ref[...], gout_ref[:, :a0]).astype(BF16), wout_ref[:a0, :])
    acc = acc + _dot(_rms(ob, gout_ref[:, a0:a1]).astype(BF16), wout_ref[a0:a1, :])
    acc = acc + _dot(_rms(oc_ref[...], gout_ref[:, a1:]).astype(BF16), wout_ref[a1:, :])
    o_ref[...] = acc


def _out_proj(x2, oa, y, oc, gluw, glub, gout, wout, tm):
    n, d = x2.shape
    tok = lambda w: pl.BlockSpec((tm, w), lambda i: (i, 0))
    full = lambda a: pl.BlockSpec(a.shape, lambda i: (0,) * a.ndim)
    return pl.pallas_call(
        _out_kernel,
        grid=(n // tm,),
        in_specs=[tok(d), tok(oa.shape[1]), tok(y.shape[1]), tok(oc.shape[1]),
                  full(gluw), full(glub), full(gout), full(wout)],
        out_specs=tok(d),
        out_shape=jax.ShapeDtypeStruct((n, d), F32),
        compiler_params=_cparams("parallel"),
        name="out_proj",
    )(x2, oa, y, oc, gluw, glub, gout, wout)


FFN_CHUNK = 1024


def _ffn_kernel(x_ref, g_ref, wu_ref, wd_ref, o_ref):
    x = x_ref[...]
    h = _rms(x, g_ref[...]).astype(BF16)
    acc = x
    for c in range(wu_ref.shape[1] // FFN_CHUNK):
        c0, c1 = c * FFN_CHUNK, (c + 1) * FFN_CHUNK
        hid = jnp.maximum(_dot(h, wu_ref[:, c0:c1]), 0.0)
        acc = acc + _dot((hid * hid).astype(BF16), wd_ref[c0:c1, :])
    o_ref[...] = acc


def _ffn(x2, g2, wu, wd, tm):
    n, d = x2.shape
    full = lambda a: pl.BlockSpec(a.shape, lambda i: (0,) * a.ndim)
    return pl.pallas_call(
        _ffn_kernel,
        grid=(n // tm,),
        in_specs=[pl.BlockSpec((tm, d), lambda i: (i, 0)), full(g2), full(wu), full(wd)],
        out_specs=pl.BlockSpec((tm, d), lambda i: (i, 0)),
        out_shape=jax.ShapeDtypeStruct((n, d), F32),
        compiler_params=_cparams("parallel"),
        name="ffn",
    )(x2, g2, wu, wd)


def _rel_bucket(dist):
    n = jnp.maximum(dist, 0)
    nf = jnp.maximum(n, 1).astype(F32)
    large = BUCKET_EXACT + (jnp.log(nf / BUCKET_EXACT) / math.log(BUCKET_MAX_DIST / BUCKET_EXACT)
                            * (NUM_BUCKETS - BUCKET_EXACT)).astype(jnp.int32)
    return jnp.where(n < BUCKET_EXACT, n, jnp.minimum(large, NUM_BUCKETS - 1))


def _bias_rows(tbl, dist, valid):
    onehot = jax.nn.one_hot(_rel_bucket(dist), NUM_BUCKETS, dtype=F32)
    b = jnp.einsum('...qkc,ch->...hqk', onehot, tbl.astype(F32), precision=HIGHEST)
    b = jnp.where(valid[..., None, :, :], b, NEG_INF)
    return b.reshape(b.shape[:-3] + (b.shape[-3] * b.shape[-2], b.shape[-1]))


def _bias_tables(rel_bias):
    swa_order = jnp.array([0, 2, 1, 3])
    tbl_a = rel_bias[:, :SWA_HEADS][:, swa_order]
    tbl_c = rel_bias[:, SWA_HEADS:]
    i = jnp.arange(QB)[:, None]
    d_swa = i - jnp.arange(2 * QB)[None, :] + QB
    bias_swa = _bias_rows(tbl_a, d_swa, (d_swa >= 0) & (d_swa < SWA_WINDOW))
    r = jnp.arange(CMP_PHASES)[:, None, None]
    c = jnp.arange(2 * LANES)[None, None, :]
    d_cmp = r * QB + i[None] - (c - LANES) * NSA_CMP_STRIDE - (NSA_CMP_BLOCK - 1)
    band_cmp = jnp.swapaxes(_bias_rows(tbl_c, d_cmp, d_cmp >= 0), 1, 2)
    j = jnp.arange(QB)[None, :]
    n_far = -(-(BUCKET_MAX_DIST + QB) // QB)
    d_sel = jnp.arange(n_far)[:, None, None] * QB + (i - j)[None]
    tsel = _bias_rows(tbl_c, d_sel, d_sel >= 0)
    d_win = jnp.arange(WIN_TILES)[:, None, None] * QB + (i - j)[None]
    twin = _bias_rows(tbl_c, d_win, (d_win >= 0) & (d_win < NSA_WINDOW))
    future = jnp.full((1,) + tsel.shape[1:], NEG_INF, F32)
    lag_table = lambda t: jnp.swapaxes(jnp.concatenate([future, t * LOG2E]), 1, 2)
    return bias_swa, band_cmp, lag_table(tsel), lag_table(twin)


def _nsa_constants(seq):
    m = seq // NSA_CMP_STRIDE
    cs = jnp.arange(m)[:, None] * NSA_CMP_STRIDE
    ss = jnp.arange(LANES)[None, :] * NSA_SEL_BLOCK
    ovl = ((cs < ss + NSA_SEL_BLOCK) & (cs + NSA_CMP_BLOCK > ss) & (cs < seq - NSA_CMP_STRIDE)).astype(BF16)
    key_blk = jnp.arange(seq)[:, None] // NSA_SEL_BLOCK
    e_pen = jnp.where(key_blk == jnp.arange(LANES)[None, :], SEL_PENALTY, 0.0).astype(BF16)
    col = jnp.arange(LANES)[:, None]
    lane = jnp.arange(NSA_BRANCHES * NSA_WIDTH)[None, :]
    br, hd = lane // NSA_WIDTH, (lane % NSA_WIDTH) // HEAD_DIM
    gexp = (col == hd * NSA_BRANCHES + br).astype(BF16)
    return ovl.T, e_pen, gexp


def _seg_matrix():
    r = jnp.arange(2 * LANES)
    return jnp.where((r[:, None] // HEAD_DIM) == (r[None, :] // HEAD_DIM), 1.0 / HEAD_DIM, 0.0).astype(BF16)


def _prep_w_in(w):
    d = w.shape[0]
    hd = HEAD_DIM
    ssm_w = d - SWA_WIDTH - NSA_WIDTH
    o_ka = SWA_WIDTH
    o_va = o_ka + SWA_KV_WIDTH
    o_u = o_va + SWA_KV_WIDTH
    o_qc = o_u + ssm_w
    o_kv = o_qc + NSA_WIDTH
    o_gc = o_kv + 6 * hd
    head = lambda off, h: w[:, off + h * hd: off + (h + 1) * hd]
    gates = w[:, o_gc:]
    cols = [head(0, 0), head(0, 2), head(0, 1), head(0, 3), w[:, o_ka:o_gc],
            gates, jnp.zeros((d, LANES - gates.shape[1]), w.dtype)]
    return jnp.concatenate(cols, axis=1).astype(BF16)


def _prep_qk_gains(qk_g):
    g = qk_g.astype(F32)
    t4 = lambda v: jnp.tile(v, 4)
    return jnp.stack([t4(g[0]), t4(g[1]), t4(g[2]), t4(g[4]), t4(g[5]),
                      t4(g[3]), t4(g[3]), t4(g[3])])


def kernel(x, norm1_g, w_in, qk_g, sinks, rel_bias, ssm_a_re, ssm_a_im, ssm_log_dt, ssm_b_re, ssm_b_im, ssm_c_re, ssm_c_im, ssm_d, glu_w, glu_b, cmp_pos, cmp_w1, cmp_w2, out_norm_g, w_out, norm2_g, w_up, w_down):
    bsz, seq, d = x.shape
    depth = w_in.shape[0]
    n = bsz * seq
    assert seq % SWA_TQ == 0 and seq // NSA_SEL_BLOCK <= LANES and d == 1024
    assert (seq // NSA_CMP_STRIDE) % LANES == 0 and seq % SEL_TK == 0 and seq >= WIN_TILES * QB
    tm = 512
    row = lambda v: v.astype(F32).reshape(1, -1)

    bias_swa, band_cmp, tsel, twin = _bias_tables(rel_bias)
    ovlt, e_pen, gexp = _nsa_constants(seq)
    seg = _seg_matrix()
    swa_order = jnp.array([0, 2, 1, 3])
    swa_cols = (swa_order[:, None] * HEAD_DIM + jnp.arange(HEAD_DIM)[None, :]).reshape(-1)

    x2 = x.reshape(n, d)
    for l in range(depth):
        qkg = _prep_qk_gains(qk_g[l])
        (qa, ka, va, u, qc, kvc, ks, vs, kw, vw, gc) = _in_proj(
            x2, row(norm1_g[l]), _prep_w_in(w_in[l]), qkg, seg, tm)

        sink_rows = jnp.repeat(sinks[l].astype(F32)[swa_order], QB).reshape(-1, 1)
        o_a = _swa(qa, ka, va, bias_swa, sink_rows, bsz, seq)

        ops = _ssm_operators(ssm_a_re[l], ssm_a_im[l], ssm_log_dt[l], ssm_b_re[l], ssm_b_im[l],
                             ssm_c_re[l], ssm_c_im[l], ssm_d[l])
        y = _s5(u, ops, bsz, seq)

        pos, w1t, w1b, w2 = _prep_compress(cmp_pos[l], cmp_w1[l], cmp_w2[l])
        kk, vvt = _nsa_compress(kvc, pos, w1t, w1b, w2, qkg[5:6, :LANES], bsz, seq)
        o_cmp, nsel = _nsa_cmp(qc, kk, vvt, band_cmp, gc, ovlt, gexp, bsz, seq)
        o_c = _nsa_attn(qc, nsel, o_cmp, gc, ks, e_pen, vs, kw, vw, tsel, twin, gexp, bsz, seq)

        gout = out_norm_g[l].astype(F32)
        gout = jnp.concatenate([gout[:SWA_WIDTH][swa_cols], gout[SWA_WIDTH:]]).reshape(1, -1)
        wout = jnp.concatenate([w_out[l][:SWA_WIDTH][swa_cols], w_out[l][SWA_WIDTH:]], axis=0).astype(BF16)
        x2 = _out_proj(x2, o_a, y, o_c, glu_w[l].astype(BF16), row(glu_b[l]), gout, wout, tm)

        x2 = _ffn(x2, row(norm2_g[l]), w_up[l].astype(BF16), w_down[l].astype(BF16), tm)
    return x2.reshape(bsz, seq, d)
```

```python
import functools
import math

import jax
import jax.numpy as jnp
from jax import lax
from jax.experimental import pallas as pl
from jax.experimental.pallas import tpu as pltpu

F32 = jnp.float32
BF16 = jnp.bfloat16
HIGHEST = lax.Precision.HIGHEST

HEAD_DIM = 64
SWA_HEADS = 4
SWA_KV_HEADS = 2
SWA_WINDOW = 128
NSA_HEADS = 4
NSA_CMP_BLOCK = 32
NSA_CMP_STRIDE = 16
NSA_SEL_BLOCK = 64
NSA_TOP_N = 16
NSA_WINDOW = 512
NSA_BRANCHES = 3
SSM_GROUP_CH = 16
SSM_STATE = 64
NUM_BUCKETS = 32
BUCKET_EXACT = NUM_BUCKETS // 2
BUCKET_MAX_DIST = 1024
EPS = 1e-6
NEG_INF = -1e30
SEL_FORCE = 1e4
SEL_PENALTY = -30000.0
LOG2E = math.log2(math.e)

LANES = 128
QB = 128
SSM_CHUNK = 16
SSM_OCT = LANES // SSM_GROUP_CH
CMP_PHASES = LANES * NSA_CMP_STRIDE // QB
VMEM_LIMIT = 56 * 1024 * 1024

SWA_WIDTH = SWA_HEADS * HEAD_DIM
SWA_KV_WIDTH = SWA_KV_HEADS * HEAD_DIM
NSA_WIDTH = NSA_HEADS * HEAD_DIM


def _cparams(*sem):
    return pltpu.CompilerParams(dimension_semantics=sem, vmem_limit_bytes=VMEM_LIMIT)


def _dot(a, b):
    return jnp.dot(a, b, preferred_element_type=F32)


def _dot_nt(a, b):
    return lax.dot_general(a, b, (((1,), (1,)), ((), ())), preferred_element_type=F32)


def _dot_split(a, b):
    hi = a.astype(BF16)
    lo = (a - hi.astype(F32)).astype(BF16)
    return _dot(hi, b) + _dot(lo, b)


def _with_ones(v):
    return jnp.concatenate([v, jnp.ones_like(v)], axis=1)


def _lane_lo(rows):
    return lax.broadcasted_iota(jnp.int32, (rows, LANES), 1) < HEAD_DIM


def _stack_heads(q):
    lo = _lane_lo(q.shape[0])
    zero = jnp.zeros_like(q[:, :LANES])
    g0 = q[:, :LANES]
    g1 = q[:, LANES:]
    return jnp.concatenate([jnp.where(lo, g0, zero), jnp.where(lo, zero, g0),
                            jnp.where(lo, g1, zero), jnp.where(lo, zero, g1)], axis=0)


def _unstack_heads(o):
    n = o.shape[0] // 4
    lo = _lane_lo(n)
    return jnp.concatenate([jnp.where(lo, o[0:n], o[n:2 * n]),
                            jnp.where(lo, o[2 * n:3 * n], o[3 * n:4 * n])], axis=1)


def _seg_rms(p, seg, gain):
    ms = _dot_split(p * p, seg)
    return p * lax.rsqrt(ms + EPS) * gain


def _in_proj_kernel(x_ref, g_ref, w_ref, qkg_ref, seg_ref,
                    qa_ref, ka_ref, va_ref, u_ref, qc_ref, kvc_ref,
                    ks_ref, vs_ref, kw_ref, vw_ref, gc_ref):
    x = x_ref[...]
    ms = jnp.mean(x * x, axis=-1, keepdims=True)
    h = (x * lax.rsqrt(ms + EPS) * g_ref[...]).astype(BF16)
    seg = seg_ref[...]
    scale = HEAD_DIM ** -0.5

    full = _dot(h, w_ref[...])

    def proj(a, b):
        return full[:, a:b]

    lo = _lane_lo(x.shape[0])

    def dup_halves(kv):
        swapped = pltpu.roll(kv, HEAD_DIM, 1)
        return jnp.where(lo, kv, swapped), jnp.where(lo, swapped, kv)

    def dup_rms(p, gain):
        m = jnp.mean(p * p, axis=-1, keepdims=True)
        return p * lax.rsqrt(m + EPS) * (gain * LOG2E)

    qa_ref[...] = (_seg_rms(proj(0, 256), seg, qkg_ref[0:1, :]) * scale).astype(BF16)
    kva = proj(256, 512)
    ka_ref[...] = _seg_rms(kva, seg, qkg_ref[1:2, :])[:, :LANES].astype(BF16)
    va_ref[...] = kva[:, LANES:].astype(BF16)
    u_ref[...] = proj(512, 1024)
    qc_ref[...] = (_seg_rms(proj(1024, 1280), seg, qkg_ref[2:3, :]) * scale).astype(BF16)
    cmp_sel = proj(1280, 1536)
    kvc_ref[...] = cmp_sel[:, :LANES]
    ks, vs = dup_halves(cmp_sel[:, LANES:])
    ks_ref[...] = dup_rms(ks, qkg_ref[3:4, :LANES]).astype(BF16)
    vs_ref[...] = vs.astype(BF16)
    win_gate = proj(1536, 1792)
    kw, vw = dup_halves(win_gate[:, :LANES])
    kw_ref[...] = dup_rms(kw, qkg_ref[4:5, :LANES]).astype(BF16)
    vw_ref[...] = vw.astype(BF16)
    gc_ref[...] = win_gate[:, LANES:]


def _in_proj(x2, g1, w, qkg, seg, tm):
    n, d = x2.shape
    widths = [(256, BF16), (128, BF16), (128, BF16), (512, F32), (256, BF16), (128, F32),
              (128, BF16), (128, BF16), (128, BF16), (128, BF16), (128, F32)]
    full = lambda a: pl.BlockSpec(a.shape, lambda i: (0,) * a.ndim)
    return pl.pallas_call(
        _in_proj_kernel,
        grid=(n // tm,),
        in_specs=[pl.BlockSpec((tm, d), lambda i: (i, 0)), full(g1), full(w), full(qkg), full(seg)],
        out_specs=[pl.BlockSpec((tm, wd), lambda i: (i, 0)) for wd, _ in widths],
        out_shape=[jax.ShapeDtypeStruct((n, wd), dt) for wd, dt in widths],
        compiler_params=_cparams("parallel"),
        name="in_proj",
    )(x2, g1, w, qkg, seg)


SWA_TQ = 512


def _swa_kernel(q_ref, kc_ref, kp_ref, vc_ref, vp_ref, bias_ref, sink_ref, o_ref):
    first = pl.program_id(1) == 0
    prev_cols = lax.broadcasted_iota(jnp.int32, (1, 2 * QB), 1) < QB
    pen = jnp.where(first & prev_cols, NEG_INF, 0.0)
    sink = sink_ref[...]
    bias = bias_ref[...]
    for s in range(SWA_TQ // QB):
        r0, r1 = s * QB, (s + 1) * QB
        qs = _stack_heads(q_ref[r0:r1, :])
        k_cur = kc_ref[r0:r1, :]
        v_cur = vc_ref[r0:r1, :]
        if s == 0:
            k_prev, v_prev = kp_ref[...], vp_ref[...]
        else:
            k_prev, v_prev = kc_ref[r0 - QB:r0, :], vc_ref[r0 - QB:r0, :]
        lg = jnp.concatenate([_dot_nt(qs, k_prev), _dot_nt(qs, k_cur)], axis=1) + bias
        if s == 0:
            lg = lg + pen
        m = jnp.maximum(jnp.max(lg, axis=-1, keepdims=True), sink)
        p = jnp.exp(lg - m)
        denom = jnp.sum(p, axis=-1, keepdims=True) + jnp.exp(sink - m)
        pv = _dot(p[:, :QB].astype(BF16), v_prev) + _dot(p[:, QB:].astype(BF16), v_cur)
        o_ref[r0:r1, :] = _unstack_heads(pv / denom)


def _swa(qa, ka, va, bias, sink_rows, bsz, seq):
    n = qa.shape[0]
    nq = seq // SWA_TQ
    per = SWA_TQ // QB
    cur = lambda b, i: (b * nq + i, 0)
    prev = lambda b, i: (jnp.maximum((b * nq + i) * per - 1, 0), 0)
    const = lambda b, i: (0, 0)
    return pl.pallas_call(
        _swa_kernel,
        grid=(bsz, nq),
        in_specs=[pl.BlockSpec((SWA_TQ, 256), cur),
                  pl.BlockSpec((SWA_TQ, LANES), cur), pl.BlockSpec((QB, LANES), prev),
                  pl.BlockSpec((SWA_TQ, LANES), cur), pl.BlockSpec((QB, LANES), prev),
                  pl.BlockSpec(bias.shape, const), pl.BlockSpec(sink_rows.shape, const)],
        out_specs=pl.BlockSpec((SWA_TQ, 256), cur),
        out_shape=jax.ShapeDtypeStruct((n, 256), F32),
        compiler_params=_cparams("parallel", "parallel"),
        name="swa",
    )(qa, ka, ka, va, va, bias, sink_rows)


def _chunk_rows(u_ref):
    return jnp.concatenate([u_ref[:, t, :] for t in range(SSM_CHUNK)], axis=1)


def _ssm_z_kernel(u_ref, pz_ref, zre_ref, zim_ref):
    z = _dot(_chunk_rows(u_ref).astype(BF16), pz_ref[0])
    half = z.shape[1] // 2
    zre_ref[...] = z[:, :half]
    zim_ref[...] = z[:, half:]


def _u_spec(tnc):
    return pl.BlockSpec((None, tnc, SSM_CHUNK, LANES), lambda j, b, r: (b, r, 0, j))


def _state_spec(tnc, sw, noct):
    return pl.BlockSpec((tnc, sw), lambda j, b, r: (r, b * noct + j))


def _ssm_z(u4, pz, tnc):
    bsz, nch = u4.shape[:2]
    noct = pz.shape[0]
    sw = pz.shape[2] // 2
    state = _state_spec(tnc, sw, noct)
    return pl.pallas_call(
        _ssm_z_kernel,
        grid=(noct, bsz, nch // tnc),
        in_specs=[_u_spec(tnc), pl.BlockSpec((1,) + pz.shape[1:], lambda j, b, r: (j, 0, 0))],
        out_specs=[state, state],
        out_shape=[jax.ShapeDtypeStruct((nch, bsz * noct * sw), F32)] * 2,
        compiler_params=_cparams("parallel", "parallel", "parallel"),
        name="ssm_chunk_state",
    )(u4, pz)


def _ssm_scan_kernel(zre_ref, zim_ref, ar_ref, ai_ref, sre_ref, sim_ref):
    a_r = ar_ref[...]
    a_i = ai_ref[...]

    def body(c, carry):
        s_r, s_i = carry
        row = pl.ds(c, 1)
        sre_ref[row, :] = s_r
        sim_ref[row, :] = s_i
        return (a_r * s_r - a_i * s_i + zre_ref[row, :], a_r * s_i + a_i * s_r + zim_ref[row, :])

    zero = jnp.zeros(a_r.shape, F32)
    lax.fori_loop(0, zre_ref.shape[0], body, (zero, zero))


def _ssm_scan(zre, zim, a_r, a_i, tl):
    nchunk, width = zre.shape
    blk = pl.BlockSpec((nchunk, tl), lambda j: (0, j))
    coef = pl.BlockSpec((1, tl), lambda j: (0, j))
    return pl.pallas_call(
        _ssm_scan_kernel,
        grid=(width // tl,),
        in_specs=[blk, blk, coef, coef],
        out_specs=[blk, blk],
        out_shape=[jax.ShapeDtypeStruct(zre.shape, F32)] * 2,
        compiler_params=_cparams("parallel"),
        name="ssm_scan",
    )(zre, zim, a_r, a_i)


def _ssm_y_kernel(u_ref, sre_ref, sim_ref, tz_ref, cre_ref, cim_ref, d_ref, y_ref):
    v = _chunk_rows(u_ref)
    vb = v.astype(BF16)
    s_re = sre_ref[...].astype(BF16)
    s_im = sim_ref[...].astype(BF16)
    w = 2 * LANES
    for tt in range(SSM_CHUNK // 2):
        c0, c1 = tt * w, (tt + 1) * w
        acc = _dot(s_re, cre_ref[0, :, c0:c1]) + _dot(s_im, cim_ref[0, :, c0:c1]) + d_ref[0, :, c0:c1] * v[:, c0:c1]
        for ss in range(tt + 1):
            acc = acc + _dot(vb[:, ss * w:(ss + 1) * w], tz_ref[0, tt - ss])
        y_ref[:, 2 * tt, :] = acc[:, :LANES]
        y_ref[:, 2 * tt + 1, :] = acc[:, LANES:]


def _ssm_y(u4, sre, sim, mm, cre, cim, dvec, tnc):
    bsz, nch = u4.shape[:2]
    noct, sw = cre.shape[:2]
    per_oct = lambda a: pl.BlockSpec((1,) + a.shape[1:], lambda j, b, r: (j,) + (0,) * (a.ndim - 1))
    state = _state_spec(tnc, sw, noct)
    return pl.pallas_call(
        _ssm_y_kernel,
        grid=(noct, bsz, nch // tnc),
        in_specs=[_u_spec(tnc), state, state, per_oct(mm), per_oct(cre), per_oct(cim), per_oct(dvec)],
        out_specs=_u_spec(tnc),
        out_shape=jax.ShapeDtypeStruct(u4.shape, F32),
        compiler_params=_cparams("parallel", "parallel", "parallel"),
        name="ssm_output",
    )(u4, sre, sim, mm, cre, cim, dvec)


def _ssm_operators(a_re, a_im, log_dt, b_re, b_im, c_re, c_im, d_skip):
    g, n = a_re.shape
    p = SSM_GROUP_CH
    t = SSM_CHUNK
    o = SSM_OCT
    noct = g // o
    a = lax.complex(a_re.astype(F32), a_im.astype(F32))
    adt = a * jnp.exp(log_dt.astype(F32))[:, None]
    b_bar = ((jnp.exp(adt) - 1.0) / a)[..., None] * lax.complex(b_re.astype(F32), b_im.astype(F32))
    cm = lax.complex(c_re.astype(F32), c_im.astype(F32))
    pw = jnp.exp(adt[None] * jnp.arange(t + 1, dtype=F32)[:, None, None].astype(jnp.complex64))
    kern = jnp.einsum('gpn,tgn,gnq->tgpq', cm, pw[:t], b_bar, precision=HIGHEST).real
    bz = (pw[:t][::-1][:, :, :, None] * b_bar[None]).transpose(1, 0, 3, 2)
    cz = (cm[None] * pw[1:, :, None, :]).transpose(1, 3, 0, 2)
    oct_ = lambda x: x.reshape((noct, o) + x.shape[1:])

    def block_diag(x, group_axis):
        w = x.shape[-1]
        own = (jnp.arange(o * w) // w) == jnp.arange(o).reshape((o,) + (1,) * (x.ndim - 1 - group_axis))
        return jnp.where(own, jnp.tile(x, o), 0.0)

    dlag = block_diag(kern.reshape(t, noct, o, p, p).swapaxes(3, 4), 2).reshape(t, noct, o * p, o * p)
    dlag = jnp.concatenate([jnp.zeros_like(dlag[:1]), dlag], axis=0)

    def pair_block(dl):
        top = jnp.concatenate([dlag[2 * dl + 1], dlag[2 * dl + 2]], axis=-1)
        bot = jnp.concatenate([dlag[2 * dl], dlag[2 * dl + 1]], axis=-1)
        return jnp.concatenate([top, bot], axis=-2)

    m_oct = jnp.stack([pair_block(dl) for dl in range(t // 2)], axis=1)
    pz_part = lambda x: block_diag(oct_(x).swapaxes(1, 2), 2).reshape(noct, t * o * p, o * n)
    c_part = lambda x: block_diag(oct_(x), 1).reshape(noct, o * n, t * o * p)
    pz = jnp.concatenate([pz_part(bz.real), pz_part(bz.imag)], axis=2)
    dvec = jnp.broadcast_to(d_skip.astype(F32).reshape(noct, 1, o, p), (noct, t, o, p)).reshape(noct, 1, t * o * p)
    a_chunk = pw[t].reshape(1, -1)
    return (m_oct.astype(BF16), pz.astype(BF16), c_part(cz.real).astype(BF16), c_part(-cz.imag).astype(BF16),
            dvec, a_chunk.real, a_chunk.imag)


def _s5(u, ops, bsz, seq):
    m_oct, pz, c_re, c_im, dvec, a_r, a_i = ops
    nch = seq // SSM_CHUNK
    u4 = u.reshape(bsz, nch, SSM_CHUNK, u.shape[1])
    tnc = min(nch, 256)
    zre, zim = _ssm_z(u4, pz, tnc)
    coef = lambda c: jnp.tile(c, (1, bsz))
    sre, sim = _ssm_scan(zre, zim, coef(a_r), coef(a_i), 2048)
    return _ssm_y(u4, sre, sim, m_oct, c_re, c_im, dvec, tnc).reshape(u.shape)


def _cmp_kernel(kvc_ref, pos_ref, w1t_ref, w1b_ref, w2_ref, gk_ref, kk_ref, vvt_ref):
    rows = kvc_ref.shape[0] // NSA_CMP_STRIDE
    top = jnp.zeros((rows, w1t_ref.shape[2]), F32)
    bot = top
    for tau in range(NSA_CMP_STRIDE):
        tok = kvc_ref[pl.ds(tau, rows, stride=NSA_CMP_STRIDE), :]
        top = top + _dot((tok + pos_ref[tau:tau + 1, :]).astype(BF16), w1t_ref[tau])
        bot = bot + _dot((tok + pos_ref[NSA_CMP_STRIDE + tau:NSA_CMP_STRIDE + tau + 1, :]).astype(BF16), w1b_ref[tau])
    hid = top + pltpu.roll(bot, rows - 1, 0)
    out = _dot(jax.nn.gelu(hid).astype(BF16), w2_ref[...])
    k = out[:, :LANES]
    ms = jnp.mean(k * k, axis=-1, keepdims=True)
    kk_ref[0] = (k * lax.rsqrt(ms + EPS) * gk_ref[...]).astype(BF16)
    vvt_ref[0] = out[:, LANES:].T[:HEAD_DIM].astype(BF16)


def _nsa_compress(kvc, pos, w1t, w1b, w2, gk, bsz, seq):
    rows = seq // NSA_CMP_STRIDE
    full = lambda a: pl.BlockSpec(a.shape, lambda b: (0,) * a.ndim)
    out = pl.BlockSpec((1, rows, LANES), lambda b: (b, 0, 0))
    return pl.pallas_call(
        _cmp_kernel,
        grid=(bsz,),
        in_specs=[pl.BlockSpec((seq, LANES), lambda b: (b, 0)), full(pos), full(w1t), full(w1b), full(w2), full(gk)],
        out_specs=[out, pl.BlockSpec((1, HEAD_DIM, rows), lambda b: (b, 0, 0))],
        out_shape=[jax.ShapeDtypeStruct((bsz, rows, LANES), BF16), jax.ShapeDtypeStruct((bsz, HEAD_DIM, rows), BF16)],
        compiler_params=_cparams("parallel"),
        name="nsa_compress",
    )(kvc, pos, w1t, w1b, w2, gk)


def _prep_compress(cmp_pos, cmp_w1, cmp_w2):
    eye = jnp.eye(2, dtype=F32)
    pos = cmp_pos.astype(F32).transpose(1, 0, 2).reshape(NSA_CMP_BLOCK, 2 * HEAD_DIM)
    w1 = cmp_w1.astype(F32).reshape(2, NSA_CMP_BLOCK, HEAD_DIM, -1)
    w1 = jnp.einsum('spdh,sz->psdzh', w1, eye).reshape(NSA_CMP_BLOCK, 2 * HEAD_DIM, -1).astype(BF16)
    w2 = jnp.concatenate([cmp_w2, cmp_w2], axis=-1).astype(F32)
    w2 = jnp.einsum('shd,sz->shzd', w2, eye).reshape(2 * w2.shape[1], 2 * LANES).astype(BF16)
    return pos, w1[:NSA_CMP_STRIDE], w1[NSA_CMP_STRIDE:], w2


def _gate_lanes(gc, gexp):
    return _dot_split(jax.nn.sigmoid(gc), gexp)


def _nsa_cmp_kernel(n_sel_blocks, q_ref, kk_ref, vvt_ref, band_ref, gc_ref, ovlt_ref, gexp_ref,
                    o_ref, nsel_ref):
    bi = pl.program_id(0)
    qs = _stack_heads(q_ref[...])
    band = band_ref[bi % CMP_PHASES]
    far = band[0:1, :]

    def attend(nt):
        def run():
            near = band if nt > 1 else band[LANES:]
            bias = jnp.concatenate([jnp.broadcast_to(far, ((nt - 2) * LANES, 4 * QB)), near], axis=0) if nt > 2 else near
            lg = _dot_nt(kk_ref[0, :nt * LANES, :], qs) + bias
            valid = lg > 0.5 * NEG_INF
            m = jnp.max(lg, axis=0, keepdims=True)
            p = jnp.where(valid, jnp.exp(lg - m), 0.0)
            denom = jnp.sum(p, axis=0, keepdims=True)
            p = p * (1.0 / jnp.where(denom > 0.0, denom, 1.0))
            o_t = _dot(vvt_ref[0, :, :nt * LANES], p.astype(BF16))
            p_sum = p[:, 0:QB] + p[:, QB:2 * QB] + p[:, 2 * QB:3 * QB] + p[:, 3 * QB:4 * QB]
            hi = p_sum.astype(BF16)
            lo = (p_sum - hi.astype(F32)).astype(BF16)
            ovl_t = ovlt_ref[:, :nt * LANES]
            return o_t, _dot(ovl_t, hi) + _dot(ovl_t, lo)
        return run

    o_t, imp_t = lax.switch(bi // CMP_PHASES, [attend(nt) for nt in range(1, kk_ref.shape[1] // LANES + 1)])
    o_cmp = jnp.concatenate([o_t[:, h * QB:(h + 1) * QB].T for h in range(NSA_HEADS)], axis=1)
    gates = _gate_lanes(gc_ref[...], gexp_ref[...])
    o_ref[...] = gates[:, :NSA_WIDTH] * o_cmp
    blk = lax.broadcasted_iota(jnp.int32, (LANES, QB), 0)
    cur = (bi * QB + lax.broadcasted_iota(jnp.int32, (LANES, QB), 1)) // NSA_SEL_BLOCK
    forced = (blk == 0) | (blk == cur) | (blk == cur - 1)
    st = jnp.where(forced, -3.4e38, jnp.where(blk > cur, -SEL_FORCE, imp_t))
    st = jnp.where(blk < n_sel_blocks, st, -3e38)
    jidx = blk.astype(F32)
    unsel = jnp.where(forced, 0.0, 1.0)
    for _ in range(min(NSA_TOP_N, n_sel_blocks) - 3):
        mx = jnp.max(st, axis=0, keepdims=True)
        first = jnp.min(jnp.where(st == mx, jidx, 1e9), axis=0, keepdims=True)
        pick = jidx == first
        unsel = jnp.where(pick, 0.0, unsel)
        st = jnp.where(pick, -3.4e38, st)
    nsel_ref[...] = unsel.T.astype(BF16)


def _nsa_cmp(qc, kk, vvt, band, gc, ovlt, gexp, bsz, seq):
    n = qc.shape[0]
    nb = seq // QB
    tok = lambda w: pl.BlockSpec((QB, w), lambda bi, b: (b * nb + bi, 0))
    seqblk = lambda a: pl.BlockSpec((1,) + a.shape[1:], lambda bi, b: (b, 0, 0))
    const = lambda a: pl.BlockSpec(a.shape, lambda bi, b: (0,) * a.ndim)
    return pl.pallas_call(
        functools.partial(_nsa_cmp_kernel, seq // NSA_SEL_BLOCK),
        grid=(nb, bsz),
        in_specs=[tok(256), seqblk(kk), seqblk(vvt), const(band), tok(LANES), const(ovlt), const(gexp)],
        out_specs=[tok(256), tok(LANES)],
        out_shape=[jax.ShapeDtypeStruct((n, 256), F32), jax.ShapeDtypeStruct((n, LANES), BF16)],
        compiler_params=_cparams("parallel", "parallel"),
        name="nsa_compressed_topn",
    )(qc, kk, vvt, band, gc, ovlt, gexp)


SEL_TK = 512
SEL_LONG_PAIRS = 2
WIN_TILES = NSA_WINDOW // QB + 1
V_ROWS = HEAD_DIM + 16


def _lag_bias(tbl_ref, bi, first_tile, n_tiles):
    last = tbl_ref.shape[0] - 2
    return jnp.concatenate([tbl_ref[jnp.clip(bi - (first_tile + i), -1, last) + 1] for i in range(n_tiles)], axis=0)


def _value_tiles(vt_ref, first_tile, n_tiles):
    return jnp.concatenate([vt_ref[first_tile + i] for i in range(n_tiles)], axis=1)


def _finish_heads(acc):
    o_t = acc[:HEAD_DIM] / acc[HEAD_DIM:HEAD_DIM + 1]
    return jnp.concatenate([o_t[:, h * QB:(h + 1) * QB].T for h in range(NSA_HEADS)], axis=1)


def _nsa_attn_kernel(q_ref, nsel_ref, ocmp_ref, gc_ref, ks_ref, e_ref, vst_ref, kw_ref, vwt_ref,
                     tsel_ref, twin_ref, gexp_ref, o_ref, qaug_ref, lga_ref, lgb_ref, acc_ref):
    bi = pl.program_id(1)
    qs = _stack_heads(q_ref[...])
    nsel = nsel_ref[...]
    qaug_ref[...] = jnp.concatenate([qs, jnp.concatenate([nsel] * NSA_HEADS, axis=0)], axis=1)
    per = SEL_TK // QB
    n_tiles = ks_ref.shape[0] // SEL_TK

    def clamp(kt):
        return jnp.minimum(kt, n_tiles - 1)

    def sel_logits(kt, lg_ref):
        rows = pl.ds(pl.multiple_of(clamp(kt) * SEL_TK, SEL_TK), SEL_TK)
        k_aug = jnp.concatenate([ks_ref[rows, :], e_ref[rows, :]], axis=1)
        lg_ref[...] = _dot_nt(k_aug, qaug_ref[...]) + _lag_bias(tsel_ref, bi, kt * per, per)

    def softmax_pv(kt, lg_ref, m):
        ps, ms, alphas = [], [], []
        for h in range(NSA_HEADS):
            cols = slice(h * QB, (h + 1) * QB)
            lg = lg_ref[:, cols]
            m_new = jnp.maximum(m[:, cols], jnp.max(lg, axis=0, keepdims=True))
            ms.append(m_new)
            alphas.append(jnp.exp2(m[:, cols] - m_new))
            ps.append(jnp.exp2(lg - m_new).astype(BF16))
        pv = _dot(_value_tiles(vst_ref, clamp(kt) * per, per), jnp.concatenate(ps, axis=1))
        acc_ref[...] = jnp.concatenate(alphas, axis=1) * acc_ref[...] + pv
        return jnp.concatenate(ms, axis=1)

    def trip(kt, m, n_pairs):
        for i in range(n_pairs):
            sel_logits(kt + 2 * i + 1, lgb_ref)
            m = softmax_pv(kt + 2 * i, lga_ref, m)
            sel_logits(kt + 2 * i + 2, lga_ref)
            m = softmax_pv(kt + 2 * i + 1, lgb_ref, m)
        return m

    acc_ref[...] = jnp.zeros(acc_ref.shape, F32)
    sel_logits(0, lga_ref)
    n_live = bi // per + 1
    n_long = n_live // (2 * SEL_LONG_PAIRS)
    m = lax.fori_loop(0, n_long, lambda j, m: trip(2 * SEL_LONG_PAIRS * j, m, SEL_LONG_PAIRS),
                      jnp.full((1, 4 * QB), NEG_INF, F32))
    first_short = 2 * SEL_LONG_PAIRS * n_long
    lax.fori_loop(0, (n_live - first_short + 1) // 2, lambda j, m: trip(first_short + 2 * j, m, 1), m)

    first = jnp.maximum(bi - (WIN_TILES - 1), 0)
    r = pl.multiple_of(first * QB, QB)
    lg = _dot_nt(kw_ref[pl.ds(r, WIN_TILES * QB), :], qs) + _lag_bias(twin_ref, bi, first, WIN_TILES)
    p = jnp.exp2(lg - jnp.max(lg, axis=0, keepdims=True)).astype(BF16)
    acc_w = _dot(_value_tiles(vwt_ref, first, WIN_TILES), p)

    gates = _gate_lanes(gc_ref[...], gexp_ref[...])
    o_ref[...] = (ocmp_ref[...] + gates[:, NSA_WIDTH:2 * NSA_WIDTH] * _finish_heads(acc_ref[...])
                  + gates[:, 2 * NSA_WIDTH:] * _finish_heads(acc_w))


def _value_rows(v, bsz, seq):
    vt = v[:, :HEAD_DIM].reshape(bsz, seq // QB, QB, HEAD_DIM).transpose(0, 1, 3, 2)
    return jnp.concatenate([vt, jnp.ones(vt.shape[:2] + (V_ROWS - HEAD_DIM, QB), vt.dtype)], axis=2)


def _nsa_attn(qc, nsel, ocmp, gc, ks, e_pen, vs, kw, vw, tsel, twin, gexp, bsz, seq):
    n = qc.shape[0]
    nb = seq // QB
    tok = lambda w: pl.BlockSpec((QB, w), lambda b, bi: (b * nb + bi, 0))
    seqblk = pl.BlockSpec((seq, LANES), lambda b, bi: (b, 0))
    valblk = pl.BlockSpec((None, nb, V_ROWS, QB), lambda b, bi: (b, 0, 0, 0))
    const = lambda a: pl.BlockSpec(a.shape, lambda b, bi: (0,) * a.ndim)
    return pl.pallas_call(
        _nsa_attn_kernel,
        grid=(bsz, nb),
        in_specs=[tok(256), tok(LANES), tok(256), tok(LANES),
                  seqblk, const(e_pen), valblk, seqblk, valblk,
                  const(tsel), const(twin), const(gexp)],
        out_specs=tok(256),
        out_shape=jax.ShapeDtypeStruct((n, 256), F32),
        scratch_shapes=[pltpu.VMEM((4 * QB, 2 * LANES), BF16), pltpu.VMEM((SEL_TK, 4 * QB), F32),
                        pltpu.VMEM((SEL_TK, 4 * QB), F32), pltpu.VMEM((V_ROWS, 4 * QB), F32)],
        compiler_params=_cparams("parallel", "arbitrary"),
        name="nsa_selected_window",
    )(qc, nsel, ocmp, gc, ks, e_pen, _value_rows(vs, bsz, seq), kw, _value_rows(vw, bsz, seq), tsel, twin, gexp)


def _rms(x, gain):
    return x * lax.rsqrt(jnp.mean(x * x, axis=-1, keepdims=True) + EPS) * gain


def _out_kernel(x_ref, oa_ref, y_ref, oc_ref, gluw_ref, glub_ref, gout_ref, wout_ref, o_ref):
    wb = y_ref.shape[1]
    ab = _dot(jax.nn.gelu(y_ref[...]).astype(BF16), gluw_ref[...]) + glub_ref[...]
    ob = ab[:, :wb] * jax.nn.sigmoid(ab[:, wb:])
    a0, a1 = SWA_WIDTH, SWA_WIDTH + wb
    mixed = jnp.concatenate([_rms(oa_ref[...], gout_ref[:, :a0]).astype(BF16),
                             _rms(ob, gout_ref[:, a0:a1]).astype(BF16),
                             _rms(oc_ref[...], gout_ref[:, a1:]).astype(BF16)], axis=1)
    o_ref[...] = x_ref[...] + _dot(mixed, wout_ref[...])


def _out_proj(x2, oa, y, oc, gluw, glub, gout, wout, tm):
    n, d = x2.shape
    tok = lambda w: pl.BlockSpec((tm, w), lambda i: (i, 0))
    full = lambda a: pl.BlockSpec(a.shape, lambda i: (0,) * a.ndim)
    return pl.pallas_call(
        _out_kernel,
        grid=(n // tm,),
        in_specs=[tok(d), tok(oa.shape[1]), tok(y.shape[1]), tok(oc.shape[1]),
                  full(gluw), full(glub), full(gout), full(wout)],
        out_specs=tok(d),
        out_shape=jax.ShapeDtypeStruct((n, d), F32),
        compiler_params=_cparams("parallel"),
        name="out_proj",
    )(x2, oa, y, oc, gluw, glub, gout, wout)


FFN_CHUNK = 1024


def _ffn_kernel(x_ref, g_ref, wu_ref, wd_ref, o_ref):
    x = x_ref[...]
    h = _rms(x, g_ref[...]).astype(BF16)
    acc = x
    for c in range(wu_ref.shape[1] // FFN_CHUNK):
        c0, c1 = c * FFN_CHUNK, (c + 1) * FFN_CHUNK
        hid = jnp.maximum(_dot(h, wu_ref[:, c0:c1]), 0.0)
        acc = acc + _dot((hid * hid).astype(BF16), wd_ref[c0:c1, :])
    o_ref[...] = acc


def _ffn(x2, g2, wu, wd, tm):
    n, d = x2.shape
    full = lambda a: pl.BlockSpec(a.shape, lambda i: (0,) * a.ndim)
    return pl.pallas_call(
        _ffn_kernel,
        grid=(n // tm,),
        in_specs=[pl.BlockSpec((tm, d), lambda i: (i, 0)), full(g2), full(wu), full(wd)],
        out_specs=pl.BlockSpec((tm, d), lambda i: (i, 0)),
        out_shape=jax.ShapeDtypeStruct((n, d), F32),
        compiler_params=_cparams("parallel"),
        name="ffn",
    )(x2, g2, wu, wd)


def _rel_bucket(dist):
    n = jnp.maximum(dist, 0)
    nf = jnp.maximum(n, 1).astype(F32)
    large = BUCKET_EXACT + (jnp.log(nf / BUCKET_EXACT) / math.log(BUCKET_MAX_DIST / BUCKET_EXACT)
                            * (NUM_BUCKETS - BUCKET_EXACT)).astype(jnp.int32)
    return jnp.where(n < BUCKET_EXACT, n, jnp.minimum(large, NUM_BUCKETS - 1))


def _bias_rows(tbl, dist, valid):
    onehot = jax.nn.one_hot(_rel_bucket(dist), NUM_BUCKETS, dtype=F32)
    b = jnp.einsum('...qkc,ch->...hqk', onehot, tbl.astype(F32), precision=HIGHEST)
    b = jnp.where(valid[..., None, :, :], b, NEG_INF)
    return b.reshape(b.shape[:-3] + (b.shape[-3] * b.shape[-2], b.shape[-1]))


def _bias_tables(rel_bias):
    swa_order = jnp.array([0, 2, 1, 3])
    tbl_a = rel_bias[:, :SWA_HEADS][:, swa_order]
    tbl_c = rel_bias[:, SWA_HEADS:]
    i = jnp.arange(QB)[:, None]
    d_swa = i - jnp.arange(2 * QB)[None, :] + QB
    bias_swa = _bias_rows(tbl_a, d_swa, (d_swa >= 0) & (d_swa < SWA_WINDOW))
    r = jnp.arange(CMP_PHASES)[:, None, None]
    c = jnp.arange(2 * LANES)[None, None, :]
    d_cmp = r * QB + i[None] - (c - LANES) * NSA_CMP_STRIDE - (NSA_CMP_BLOCK - 1)
    band_cmp = jnp.swapaxes(_bias_rows(tbl_c, d_cmp, d_cmp >= 0), 1, 2)
    j = jnp.arange(QB)[None, :]
    n_far = -(-(BUCKET_MAX_DIST + QB) // QB)
    d_sel = jnp.arange(n_far)[:, None, None] * QB + (i - j)[None]
    tsel = _bias_rows(tbl_c, d_sel, d_sel >= 0)
    d_win = jnp.arange(WIN_TILES)[:, None, None] * QB + (i - j)[None]
    twin = _bias_rows(tbl_c, d_win, (d_win >= 0) & (d_win < NSA_WINDOW))
    future = jnp.full((1,) + tsel.shape[1:], NEG_INF, F32)
    lag_table = lambda t: jnp.swapaxes(jnp.concatenate([future, t * LOG2E]), 1, 2)
    return bias_swa, band_cmp, lag_table(tsel), lag_table(twin)


def _nsa_constants(seq):
    m = seq // NSA_CMP_STRIDE
    cs = jnp.arange(m)[:, None] * NSA_CMP_STRIDE
    ss = jnp.arange(LANES)[None, :] * NSA_SEL_BLOCK
    ovl = ((cs < ss + NSA_SEL_BLOCK) & (cs + NSA_CMP_BLOCK > ss) & (cs < seq - NSA_CMP_STRIDE)).astype(BF16)
    key_blk = jnp.arange(seq)[:, None] // NSA_SEL_BLOCK
    e_pen = jnp.where(key_blk == jnp.arange(LANES)[None, :], SEL_PENALTY, 0.0).astype(BF16)
    col = jnp.arange(LANES)[:, None]
    lane = jnp.arange(NSA_BRANCHES * NSA_WIDTH)[None, :]
    br, hd = lane // NSA_WIDTH, (lane % NSA_WIDTH) // HEAD_DIM
    gexp = (col == hd * NSA_BRANCHES + br).astype(BF16)
    return ovl.T, e_pen, gexp


def _seg_matrix():
    r = jnp.arange(2 * LANES)
    return jnp.where((r[:, None] // HEAD_DIM) == (r[None, :] // HEAD_DIM), 1.0 / HEAD_DIM, 0.0).astype(BF16)


def _prep_w_in(w):
    d = w.shape[0]
    hd = HEAD_DIM
    ssm_w = d - SWA_WIDTH - NSA_WIDTH
    o_ka = SWA_WIDTH
    o_va = o_ka + SWA_KV_WIDTH
    o_u = o_va + SWA_KV_WIDTH
    o_qc = o_u + ssm_w
    o_kv = o_qc + NSA_WIDTH
    o_gc = o_kv + 6 * hd
    head = lambda off, h: w[:, off + h * hd: off + (h + 1) * hd]
    gates = w[:, o_gc:]
    cols = [head(0, 0), head(0, 2), head(0, 1), head(0, 3), w[:, o_ka:o_gc],
            gates, jnp.zeros((d, LANES - gates.shape[1]), w.dtype)]
    return jnp.concatenate(cols, axis=1).astype(BF16)


def _prep_qk_gains(qk_g):
    g = qk_g.astype(F32)
    t4 = lambda v: jnp.tile(v, 4)
    return jnp.stack([t4(g[0]), t4(g[1]), t4(g[2]), t4(g[4]), t4(g[5]),
                      t4(g[3]), t4(g[3]), t4(g[3])])


def kernel(x, norm1_g, w_in, qk_g, sinks, rel_bias, ssm_a_re, ssm_a_im, ssm_log_dt, ssm_b_re, ssm_b_im, ssm_c_re, ssm_c_im, ssm_d, glu_w, glu_b, cmp_pos, cmp_w1, cmp_w2, out_norm_g, w_out, norm2_g, w_up, w_down):
    bsz, seq, d = x.shape
    depth = w_in.shape[0]
    n = bsz * seq
    assert seq % SWA_TQ == 0 and seq // NSA_SEL_BLOCK <= LANES and d == 1024
    assert (seq // NSA_CMP_STRIDE) % LANES == 0 and seq % SEL_TK == 0 and seq >= WIN_TILES * QB
    tm = 512
    row = lambda v: v.astype(F32).reshape(1, -1)

    bias_swa, band_cmp, tsel, twin = _bias_tables(rel_bias)
    ovlt, e_pen, gexp = _nsa_constants(seq)
    seg = _seg_matrix()
    swa_order = jnp.array([0, 2, 1, 3])
    swa_cols = (swa_order[:, None] * HEAD_DIM + jnp.arange(HEAD_DIM)[None, :]).reshape(-1)

    qkg_all = jax.vmap(_prep_qk_gains)(qk_g)
    w_in_all = jax.vmap(_prep_w_in)(w_in)
    ssm_all = jax.vmap(_ssm_operators)(ssm_a_re, ssm_a_im, ssm_log_dt, ssm_b_re, ssm_b_im, ssm_c_re, ssm_c_im, ssm_d)
    cmp_all = jax.vmap(_prep_compress)(cmp_pos, cmp_w1, cmp_w2)
    sink_all = jnp.repeat(sinks.astype(F32)[:, swa_order], QB, axis=1)[..., None]
    gout_all = jnp.concatenate([out_norm_g[:, :SWA_WIDTH][:, swa_cols], out_norm_g[:, SWA_WIDTH:]], axis=1).astype(F32)
    w_out_all = jnp.concatenate([w_out[:, :SWA_WIDTH][:, swa_cols], w_out[:, SWA_WIDTH:]], axis=1).astype(BF16)
    glu_w_all, w_up_all, w_down_all = glu_w.astype(BF16), w_up.astype(BF16), w_down.astype(BF16)

    x2 = x.reshape(n, d)
    for l in range(depth):
        qkg = qkg_all[l]
        (qa, ka, va, u, qc, kvc, ks, vs, kw, vw, gc) = _in_proj(x2, row(norm1_g[l]), w_in_all[l], qkg, seg, tm)

        o_a = _swa(qa, ka, va, bias_swa, sink_all[l], bsz, seq)

        y = _s5(u, [op[l] for op in ssm_all], bsz, seq)

        pos, w1t, w1b, w2 = [c[l] for c in cmp_all]
        kk, vvt = _nsa_compress(kvc, pos, w1t, w1b, w2, qkg[5:6, :LANES], bsz, seq)
        o_cmp, nsel = _nsa_cmp(qc, kk, vvt, band_cmp, gc, ovlt, gexp, bsz, seq)
        o_c = _nsa_attn(qc, nsel, o_cmp, gc, ks, e_pen, vs, kw, vw, tsel, twin, gexp, bsz, seq)

        x2 = _out_proj(x2, o_a, y, o_c, glu_w_all[l], row(glu_b[l]), gout_all[l:l + 1], w_out_all[l], tm)

        x2 = _ffn(x2, row(norm2_g[l]), w_up_all[l], w_down_all[l], tm)
    return x2.reshape(bsz, seq, d)
```

```python
import functools
import math

import jax
import jax.numpy as jnp
from jax import lax
from jax.experimental import pallas as pl
from jax.experimental.pallas import tpu as pltpu

F32 = jnp.float32
BF16 = jnp.bfloat16
HIGHEST = lax.Precision.HIGHEST

HEAD_DIM = 64
SWA_HEADS = 4
SWA_KV_HEADS = 2
SWA_WINDOW = 128
NSA_HEADS = 4
NSA_CMP_BLOCK = 32
NSA_CMP_STRIDE = 16
NSA_SEL_BLOCK = 64
NSA_TOP_N = 16
NSA_WINDOW = 512
NSA_BRANCHES = 3
SSM_GROUP_CH = 16
SSM_STATE = 64
NUM_BUCKETS = 32
BUCKET_EXACT = NUM_BUCKETS // 2
BUCKET_MAX_DIST = 1024
EPS = 1e-6
NEG_INF = -1e30
SEL_FORCE = 1e4
SEL_PENALTY = -30000.0
LOG2E = math.log2(math.e)

LANES = 128
QB = 128
SSM_CHUNK = 16
SSM_OCT = LANES // SSM_GROUP_CH
CMP_PHASES = LANES * NSA_CMP_STRIDE // QB
VMEM_LIMIT = 56 * 1024 * 1024

SWA_WIDTH = SWA_HEADS * HEAD_DIM
SWA_KV_WIDTH = SWA_KV_HEADS * HEAD_DIM
NSA_WIDTH = NSA_HEADS * HEAD_DIM


def _cparams(*sem):
    return pltpu.CompilerParams(dimension_semantics=sem, vmem_limit_bytes=VMEM_LIMIT)


def _dot(a, b):
    return jnp.dot(a, b, preferred_element_type=F32)


def _dot_nt(a, b):
    return lax.dot_general(a, b, (((1,), (1,)), ((), ())), preferred_element_type=F32)


def _dot_split(a, b):
    hi = a.astype(BF16)
    lo = (a - hi.astype(F32)).astype(BF16)
    return _dot(hi, b) + _dot(lo, b)


def _with_ones(v):
    return jnp.concatenate([v, jnp.ones_like(v)], axis=1)


def _lane_lo(rows):
    return lax.broadcasted_iota(jnp.int32, (rows, LANES), 1) < HEAD_DIM


def _stack_heads(q):
    lo = _lane_lo(q.shape[0])
    zero = jnp.zeros_like(q[:, :LANES])
    g0 = q[:, :LANES]
    g1 = q[:, LANES:]
    return jnp.concatenate([jnp.where(lo, g0, zero), jnp.where(lo, zero, g0),
                            jnp.where(lo, g1, zero), jnp.where(lo, zero, g1)], axis=0)


def _unstack_heads(o):
    n = o.shape[0] // 4
    lo = _lane_lo(n)
    return jnp.concatenate([jnp.where(lo, o[0:n], o[n:2 * n]),
                            jnp.where(lo, o[2 * n:3 * n], o[3 * n:4 * n])], axis=1)


def _seg_rms(p, seg, gain):
    ms = _dot_split(p * p, seg)
    return p * lax.rsqrt(ms + EPS) * gain


def _in_proj_kernel(x_ref, g_ref, w_ref, qkg_ref, seg_ref,
                    qa_ref, ka_ref, va_ref, u_ref, qc_ref, kvc_ref,
                    ks_ref, vs_ref, kw_ref, vw_ref, gc_ref):
    x = x_ref[...]
    ms = jnp.mean(x * x, axis=-1, keepdims=True)
    h = (x * lax.rsqrt(ms + EPS) * g_ref[...]).astype(BF16)
    seg = seg_ref[...]
    scale = HEAD_DIM ** -0.5

    full = _dot(h, w_ref[...])

    def proj(a, b):
        return full[:, a:b]

    lo = _lane_lo(x.shape[0])

    def dup_halves(kv):
        swapped = pltpu.roll(kv, HEAD_DIM, 1)
        return jnp.where(lo, kv, swapped), jnp.where(lo, swapped, kv)

    def dup_rms(p, gain):
        m = jnp.mean(p * p, axis=-1, keepdims=True)
        return p * lax.rsqrt(m + EPS) * (gain * LOG2E)

    qa_ref[...] = (_seg_rms(proj(0, 256), seg, qkg_ref[0:1, :]) * scale).astype(BF16)
    kva = proj(256, 512)
    ka_ref[...] = _seg_rms(kva, seg, qkg_ref[1:2, :])[:, :LANES].astype(BF16)
    va_ref[...] = kva[:, LANES:].astype(BF16)
    u_ref[...] = proj(512, 1024)
    qc_ref[...] = (_seg_rms(proj(1024, 1280), seg, qkg_ref[2:3, :]) * scale).astype(BF16)
    cmp_sel = proj(1280, 1536)
    kvc_ref[...] = cmp_sel[:, :LANES]
    ks, vs = dup_halves(cmp_sel[:, LANES:])
    ks_ref[...] = dup_rms(ks, qkg_ref[3:4, :LANES]).astype(BF16)
    vs_ref[...] = vs.astype(BF16)
    win_gate = proj(1536, 1792)
    kw, vw = dup_halves(win_gate[:, :LANES])
    kw_ref[...] = dup_rms(kw, qkg_ref[4:5, :LANES]).astype(BF16)
    vw_ref[...] = vw.astype(BF16)
    gc_ref[...] = win_gate[:, LANES:]


def _in_proj(x2, g1, w, qkg, seg, tm):
    n, d = x2.shape
    widths = [(256, BF16), (128, BF16), (128, BF16), (512, F32), (256, BF16), (128, F32),
              (128, BF16), (128, BF16), (128, BF16), (128, BF16), (128, F32)]
    full = lambda a: pl.BlockSpec(a.shape, lambda i: (0,) * a.ndim)
    return pl.pallas_call(
        _in_proj_kernel,
        grid=(n // tm,),
        in_specs=[pl.BlockSpec((tm, d), lambda i: (i, 0)), full(g1), full(w), full(qkg), full(seg)],
        out_specs=[pl.BlockSpec((tm, wd), lambda i: (i, 0)) for wd, _ in widths],
        out_shape=[jax.ShapeDtypeStruct((n, wd), dt) for wd, dt in widths],
        compiler_params=_cparams("parallel"),
        name="in_proj",
    )(x2, g1, w, qkg, seg)


SWA_TQ = 512


def _swa_kernel(q_ref, kc_ref, kp_ref, vc_ref, vp_ref, bias_ref, sink_ref, o_ref):
    first = pl.program_id(1) == 0
    prev_cols = lax.broadcasted_iota(jnp.int32, (1, 2 * QB), 1) < QB
    pen = jnp.where(first & prev_cols, NEG_INF, 0.0)
    sink = sink_ref[...]
    bias = bias_ref[...]
    for s in range(SWA_TQ // QB):
        r0, r1 = s * QB, (s + 1) * QB
        qs = _stack_heads(q_ref[r0:r1, :])
        k_cur = kc_ref[r0:r1, :]
        v_cur = vc_ref[r0:r1, :]
        if s == 0:
            k_prev, v_prev = kp_ref[...], vp_ref[...]
        else:
            k_prev, v_prev = kc_ref[r0 - QB:r0, :], vc_ref[r0 - QB:r0, :]
        lg = jnp.concatenate([_dot_nt(qs, k_prev), _dot_nt(qs, k_cur)], axis=1) + bias
        if s == 0:
            lg = lg + pen
        m = jnp.maximum(jnp.max(lg, axis=-1, keepdims=True), sink)
        p = jnp.exp(lg - m)
        denom = jnp.sum(p, axis=-1, keepdims=True) + jnp.exp(sink - m)
        pv = _dot(p[:, :QB].astype(BF16), v_prev) + _dot(p[:, QB:].astype(BF16), v_cur)
        o_ref[r0:r1, :] = _unstack_heads(pv / denom)


def _swa(qa, ka, va, bias, sink_rows, bsz, seq):
    n = qa.shape[0]
    nq = seq // SWA_TQ
    per = SWA_TQ // QB
    cur = lambda b, i: (b * nq + i, 0)
    prev = lambda b, i: (jnp.maximum((b * nq + i) * per - 1, 0), 0)
    const = lambda b, i: (0, 0)
    return pl.pallas_call(
        _swa_kernel,
        grid=(bsz, nq),
        in_specs=[pl.BlockSpec((SWA_TQ, 256), cur),
                  pl.BlockSpec((SWA_TQ, LANES), cur), pl.BlockSpec((QB, LANES), prev),
                  pl.BlockSpec((SWA_TQ, LANES), cur), pl.BlockSpec((QB, LANES), prev),
                  pl.BlockSpec(bias.shape, const), pl.BlockSpec(sink_rows.shape, const)],
        out_specs=pl.BlockSpec((SWA_TQ, 256), cur),
        out_shape=jax.ShapeDtypeStruct((n, 256), F32),
        compiler_params=_cparams("parallel", "parallel"),
        name="swa",
    )(qa, ka, ka, va, va, bias, sink_rows)


def _chunk_rows(u_ref):
    return jnp.concatenate([u_ref[:, t, :] for t in range(SSM_CHUNK)], axis=1)


def _ssm_z_kernel(u_ref, pz_ref, zre_ref, zim_ref):
    z = _dot(_chunk_rows(u_ref).astype(BF16), pz_ref[0])
    half = z.shape[1] // 2
    zre_ref[...] = z[:, :half]
    zim_ref[...] = z[:, half:]


def _u_spec(tnc):
    return pl.BlockSpec((None, tnc, SSM_CHUNK, LANES), lambda j, b, r: (b, r, 0, j))


def _state_spec(tnc, sw, noct):
    return pl.BlockSpec((tnc, sw), lambda j, b, r: (r, b * noct + j))


def _ssm_z(u4, pz, tnc):
    bsz, nch = u4.shape[:2]
    noct = pz.shape[0]
    sw = pz.shape[2] // 2
    state = _state_spec(tnc, sw, noct)
    return pl.pallas_call(
        _ssm_z_kernel,
        grid=(noct, bsz, nch // tnc),
        in_specs=[_u_spec(tnc), pl.BlockSpec((1,) + pz.shape[1:], lambda j, b, r: (j, 0, 0))],
        out_specs=[state, state],
        out_shape=[jax.ShapeDtypeStruct((nch, bsz * noct * sw), F32)] * 2,
        compiler_params=_cparams("parallel", "parallel", "parallel"),
        name="ssm_chunk_state",
    )(u4, pz)


def _ssm_scan_kernel(zre_ref, zim_ref, ar_ref, ai_ref, sre_ref, sim_ref):
    a_r = ar_ref[...]
    a_i = ai_ref[...]

    def body(c, carry):
        s_r, s_i = carry
        row = pl.ds(c, 1)
        sre_ref[row, :] = s_r
        sim_ref[row, :] = s_i
        return (a_r * s_r - a_i * s_i + zre_ref[row, :], a_r * s_i + a_i * s_r + zim_ref[row, :])

    zero = jnp.zeros(a_r.shape, F32)
    lax.fori_loop(0, zre_ref.shape[0], body, (zero, zero))


def _ssm_scan(zre, zim, a_r, a_i, tl):
    nchunk, width = zre.shape
    blk = pl.BlockSpec((nchunk, tl), lambda j: (0, j))
    coef = pl.BlockSpec((1, tl), lambda j: (0, j))
    return pl.pallas_call(
        _ssm_scan_kernel,
        grid=(width // tl,),
        in_specs=[blk, blk, coef, coef],
        out_specs=[blk, blk],
        out_shape=[jax.ShapeDtypeStruct(zre.shape, F32)] * 2,
        compiler_params=_cparams("parallel"),
        name="ssm_scan",
    )(zre, zim, a_r, a_i)


def _ssm_y_kernel(u_ref, sre_ref, sim_ref, tz_ref, c_ref, d_ref, y_ref):
    v = _chunk_rows(u_ref)
    vb = v.astype(BF16)
    state = jnp.concatenate([sre_ref[...], sim_ref[...]], axis=1).astype(BF16)
    w = 2 * LANES
    n_pairs = SSM_CHUNK // 2
    for tt in range(n_pairs):
        c0, c1 = tt * w, (tt + 1) * w
        acc = _dot(vb[:, :c1], tz_ref[0, (n_pairs - 1 - tt) * w:, :])
        acc = acc + _dot(state, c_ref[0, :, c0:c1]) + d_ref[0, :, c0:c1] * v[:, c0:c1]
        y_ref[:, 2 * tt, :] = acc[:, :LANES]
        y_ref[:, 2 * tt + 1, :] = acc[:, LANES:]


def _ssm_y(u4, sre, sim, mm, c_cat, dvec, tnc):
    bsz, nch = u4.shape[:2]
    noct = c_cat.shape[0]
    sw = c_cat.shape[1] // 2
    per_oct = lambda a: pl.BlockSpec((1,) + a.shape[1:], lambda j, b, r: (j,) + (0,) * (a.ndim - 1))
    state = _state_spec(tnc, sw, noct)
    return pl.pallas_call(
        _ssm_y_kernel,
        grid=(noct, bsz, nch // tnc),
        in_specs=[_u_spec(tnc), state, state, per_oct(mm), per_oct(c_cat), per_oct(dvec)],
        out_specs=_u_spec(tnc),
        out_shape=jax.ShapeDtypeStruct(u4.shape, F32),
        compiler_params=_cparams("parallel", "parallel", "parallel"),
        name="ssm_output",
    )(u4, sre, sim, mm, c_cat, dvec)


def _ssm_operators(a_re, a_im, log_dt, b_re, b_im, c_re, c_im, d_skip):
    g, n = a_re.shape
    p = SSM_GROUP_CH
    t = SSM_CHUNK
    o = SSM_OCT
    noct = g // o
    a = lax.complex(a_re.astype(F32), a_im.astype(F32))
    adt = a * jnp.exp(log_dt.astype(F32))[:, None]
    b_bar = ((jnp.exp(adt) - 1.0) / a)[..., None] * lax.complex(b_re.astype(F32), b_im.astype(F32))
    cm = lax.complex(c_re.astype(F32), c_im.astype(F32))
    pw = jnp.exp(adt[None] * jnp.arange(t + 1, dtype=F32)[:, None, None].astype(jnp.complex64))
    kern = jnp.einsum('gpn,tgn,gnq->tgpq', cm, pw[:t], b_bar, precision=HIGHEST).real
    bz = (pw[:t][::-1][:, :, :, None] * b_bar[None]).transpose(1, 0, 3, 2)
    cz = (cm[None] * pw[1:, :, None, :]).transpose(1, 3, 0, 2)
    oct_ = lambda x: x.reshape((noct, o) + x.shape[1:])

    def placement(reps, w):
        src = jnp.arange(reps * w)
        dst = (src // w)[None, :] * (o * w) + jnp.arange(o)[:, None] * w + (src % w)[None, :]
        return jax.nn.one_hot(dst, reps * o * w, dtype=F32)

    dlag = jnp.einsum('ljapq,apm->ljaqm', kern.reshape(t, noct, o, p, p), placement(1, p)).reshape(t, noct, o * p, o * p)
    dlag = jnp.concatenate([jnp.zeros_like(dlag[:1]), dlag], axis=0)

    def pair_block(dl):
        top = jnp.concatenate([dlag[2 * dl + 1], dlag[2 * dl + 2]], axis=-1)
        bot = jnp.concatenate([dlag[2 * dl], dlag[2 * dl + 1]], axis=-1)
        return jnp.concatenate([top, bot], axis=-2)

    m_oct = jnp.concatenate([pair_block(dl) for dl in reversed(range(t // 2))], axis=1)
    pz_part = lambda x: jnp.einsum('jasqn,anm->jsaqm', oct_(x), placement(1, n)).reshape(noct, t * o * p, o * n)
    c_part = lambda x: jnp.einsum('janx,axm->janm', oct_(x).reshape(noct, o, n, t * p),
                                  placement(t, p)).reshape(noct, o * n, t * o * p)
    pz = jnp.concatenate([pz_part(bz.real), pz_part(bz.imag)], axis=2)
    dvec = jnp.broadcast_to(d_skip.astype(F32).reshape(noct, 1, o, p), (noct, t, o, p)).reshape(noct, 1, t * o * p)
    a_chunk = pw[t].reshape(1, -1)
    c_cat = jnp.concatenate([c_part(cz.real), c_part(-cz.imag)], axis=1)
    return m_oct.astype(BF16), pz.astype(BF16), c_cat.astype(BF16), dvec, a_chunk.real, a_chunk.imag


def _s5(u, ops, bsz, seq):
    m_oct, pz, c_cat, dvec, a_r, a_i = ops
    nch = seq // SSM_CHUNK
    u4 = u.reshape(bsz, nch, SSM_CHUNK, u.shape[1])
    tnc = min(nch, 256)
    zre, zim = _ssm_z(u4, pz, tnc)
    coef = lambda c: jnp.tile(c, (1, bsz))
    sre, sim = _ssm_scan(zre, zim, coef(a_r), coef(a_i), 2048)
    return _ssm_y(u4, sre, sim, m_oct, c_cat, dvec, tnc).reshape(u.shape)


def _cmp_kernel(kvc_ref, pos_ref, w1t_ref, w1b_ref, w2_ref, gk_ref, kk_ref, vvt_ref):
    rows = kvc_ref.shape[0] // NSA_CMP_STRIDE
    top = jnp.zeros((rows, w1t_ref.shape[2]), F32)
    bot = top
    for tau in range(NSA_CMP_STRIDE):
        tok = kvc_ref[pl.ds(tau, rows, stride=NSA_CMP_STRIDE), :]
        top = top + _dot((tok + pos_ref[tau:tau + 1, :]).astype(BF16), w1t_ref[tau])
        bot = bot + _dot((tok + pos_ref[NSA_CMP_STRIDE + tau:NSA_CMP_STRIDE + tau + 1, :]).astype(BF16), w1b_ref[tau])
    hid = top + pltpu.roll(bot, rows - 1, 0)
    out = _dot(jax.nn.gelu(hid).astype(BF16), w2_ref[...])
    k = out[:, :LANES]
    ms = jnp.mean(k * k, axis=-1, keepdims=True)
    kk_ref[0] = (k * lax.rsqrt(ms + EPS) * gk_ref[...]).astype(BF16)
    vvt_ref[0] = out[:, LANES:].T[:HEAD_DIM].astype(BF16)


def _nsa_compress(kvc, pos, w1t, w1b, w2, gk, bsz, seq):
    rows = seq // NSA_CMP_STRIDE
    full = lambda a: pl.BlockSpec(a.shape, lambda b: (0,) * a.ndim)
    out = pl.BlockSpec((1, rows, LANES), lambda b: (b, 0, 0))
    return pl.pallas_call(
        _cmp_kernel,
        grid=(bsz,),
        in_specs=[pl.BlockSpec((seq, LANES), lambda b: (b, 0)), full(pos), full(w1t), full(w1b), full(w2), full(gk)],
        out_specs=[out, pl.BlockSpec((1, HEAD_DIM, rows), lambda b: (b, 0, 0))],
        out_shape=[jax.ShapeDtypeStruct((bsz, rows, LANES), BF16), jax.ShapeDtypeStruct((bsz, HEAD_DIM, rows), BF16)],
        compiler_params=_cparams("parallel"),
        name="nsa_compress",
    )(kvc, pos, w1t, w1b, w2, gk)


def _prep_compress(cmp_pos, cmp_w1, cmp_w2):
    eye = jnp.eye(2, dtype=F32)
    pos = cmp_pos.astype(F32).transpose(1, 0, 2).reshape(NSA_CMP_BLOCK, 2 * HEAD_DIM)
    w1 = cmp_w1.astype(F32).reshape(2, NSA_CMP_BLOCK, HEAD_DIM, -1)
    w1 = jnp.einsum('spdh,sz->psdzh', w1, eye).reshape(NSA_CMP_BLOCK, 2 * HEAD_DIM, -1).astype(BF16)
    w2 = jnp.concatenate([cmp_w2, cmp_w2], axis=-1).astype(F32)
    w2 = jnp.einsum('shd,sz->shzd', w2, eye).reshape(2 * w2.shape[1], 2 * LANES).astype(BF16)
    return pos, w1[:NSA_CMP_STRIDE], w1[NSA_CMP_STRIDE:], w2


def _gate_lanes(gc, gexp):
    return _dot_split(jax.nn.sigmoid(gc), gexp)


def _nsa_cmp_kernel(n_sel_blocks, q_ref, kk_ref, vvt_ref, band_ref, gc_ref, ovlt_ref, gexp_ref,
                    o_ref, nsel_ref):
    bi = pl.program_id(0)
    qs = _stack_heads(q_ref[...])
    band = band_ref[bi % CMP_PHASES]
    far = band[0:1, :]

    def attend(nt):
        def run():
            near = band if nt > 1 else band[LANES:]
            bias = jnp.concatenate([jnp.broadcast_to(far, ((nt - 2) * LANES, 4 * QB)), near], axis=0) if nt > 2 else near
            lg = _dot_nt(kk_ref[0, :nt * LANES, :], qs) + bias
            valid = lg > 0.5 * NEG_INF
            m = jnp.max(lg, axis=0, keepdims=True)
            p = jnp.where(valid, jnp.exp(lg - m), 0.0)
            denom = jnp.sum(p, axis=0, keepdims=True)
            p = p * (1.0 / jnp.where(denom > 0.0, denom, 1.0))
            o_t = _dot(vvt_ref[0, :, :nt * LANES], p.astype(BF16))
            p_sum = p[:, 0:QB] + p[:, QB:2 * QB] + p[:, 2 * QB:3 * QB] + p[:, 3 * QB:4 * QB]
            hi = p_sum.astype(BF16)
            lo = (p_sum - hi.astype(F32)).astype(BF16)
            ovl_t = ovlt_ref[:, :nt * LANES]
            return o_t, _dot(ovl_t, hi) + _dot(ovl_t, lo)
        return run

    o_t, imp_t = lax.switch(bi // CMP_PHASES, [attend(nt) for nt in range(1, kk_ref.shape[1] // LANES + 1)])
    o_cmp = jnp.concatenate([o_t[:, h * QB:(h + 1) * QB].T for h in range(NSA_HEADS)], axis=1)
    gates = _gate_lanes(gc_ref[...], gexp_ref[...])
    o_ref[...] = gates[:, :NSA_WIDTH] * o_cmp
    blk = lax.broadcasted_iota(jnp.int32, (LANES, QB), 0)
    cur = (bi * QB + lax.broadcasted_iota(jnp.int32, (LANES, QB), 1)) // NSA_SEL_BLOCK
    forced = (blk == 0) | (blk == cur) | (blk == cur - 1)
    st = jnp.where(forced, -3.4e38, jnp.where(blk > cur, -SEL_FORCE, imp_t))
    st = jnp.where(blk < n_sel_blocks, st, -3e38)
    jidx = blk.astype(F32)
    unsel = jnp.where(forced, 0.0, 1.0)
    for _ in range(min(NSA_TOP_N, n_sel_blocks) - 3):
        mx = jnp.max(st, axis=0, keepdims=True)
        first = jnp.min(jnp.where(st == mx, jidx, 1e9), axis=0, keepdims=True)
        pick = jidx == first
        unsel = jnp.where(pick, 0.0, unsel)
        st = jnp.where(pick, -3.4e38, st)
    nsel_ref[...] = unsel.T.astype(BF16)


def _nsa_cmp(qc, kk, vvt, band, gc, ovlt, gexp, bsz, seq):
    n = qc.shape[0]
    nb = seq // QB
    tok = lambda w: pl.BlockSpec((QB, w), lambda bi, b: (b * nb + bi, 0))
    seqblk = lambda a: pl.BlockSpec((1,) + a.shape[1:], lambda bi, b: (b, 0, 0))
    const = lambda a: pl.BlockSpec(a.shape, lambda bi, b: (0,) * a.ndim)
    return pl.pallas_call(
        functools.partial(_nsa_cmp_kernel, seq // NSA_SEL_BLOCK),
        grid=(nb, bsz),
        in_specs=[tok(256), seqblk(kk), seqblk(vvt), const(band), tok(LANES), const(ovlt), const(gexp)],
        out_specs=[tok(256), tok(LANES)],
        out_shape=[jax.ShapeDtypeStruct((n, 256), F32), jax.ShapeDtypeStruct((n, LANES), BF16)],
        compiler_params=_cparams("parallel", "parallel"),
        name="nsa_compressed_topn",
    )(qc, kk, vvt, band, gc, ovlt, gexp)


SEL_TK = 512
SEL_LONG_PAIRS = 2
WIN_TILES = NSA_WINDOW // QB + 1
V_ROWS = HEAD_DIM + 16


def _lag_bias(tbl_ref, bi, first_tile, n_tiles):
    last = tbl_ref.shape[0] - 2
    return jnp.concatenate([tbl_ref[jnp.clip(bi - (first_tile + i), -1, last) + 1] for i in range(n_tiles)], axis=0)


def _value_tiles(vt_ref, first_tile, n_tiles):
    return jnp.concatenate([vt_ref[first_tile + i] for i in range(n_tiles)], axis=1)


def _finish_heads(acc):
    o_t = acc[:HEAD_DIM] / acc[HEAD_DIM:HEAD_DIM + 1]
    return jnp.concatenate([o_t[:, h * QB:(h + 1) * QB].T for h in range(NSA_HEADS)], axis=1)


def _nsa_attn_kernel(q_ref, nsel_ref, ocmp_ref, gc_ref, ks_ref, e_ref, vst_ref, kw_ref, vwt_ref,
                     tsel_ref, twin_ref, gexp_ref, o_ref, qaug_ref, lga_ref, lgb_ref, acc_ref):
    bi = pl.program_id(1)
    qs = _stack_heads(q_ref[...])
    nsel = nsel_ref[...]
    qaug_ref[...] = jnp.concatenate([qs, jnp.concatenate([nsel] * NSA_HEADS, axis=0)], axis=1)
    per = SEL_TK // QB
    n_tiles = ks_ref.shape[0] // SEL_TK

    def clamp(kt):
        return jnp.minimum(kt, n_tiles - 1)

    def sel_logits(kt, lg_ref):
        rows = pl.ds(pl.multiple_of(clamp(kt) * SEL_TK, SEL_TK), SEL_TK)
        k_aug = jnp.concatenate([ks_ref[rows, :], e_ref[rows, :]], axis=1)
        lg_ref[...] = _dot_nt(k_aug, qaug_ref[...]) + _lag_bias(tsel_ref, bi, kt * per, per)

    def softmax_pv(kt, lg_ref, m):
        ps, ms, alphas = [], [], []
        for h in range(NSA_HEADS):
            cols = slice(h * QB, (h + 1) * QB)
            lg = lg_ref[:, cols]
            m_new = jnp.maximum(m[:, cols], jnp.max(lg, axis=0, keepdims=True))
            ms.append(m_new)
            alphas.append(jnp.exp2(m[:, cols] - m_new))
            ps.append(jnp.exp2(lg - m_new).astype(BF16))
        pv = _dot(_value_tiles(vst_ref, clamp(kt) * per, per), jnp.concatenate(ps, axis=1))
        acc_ref[...] = jnp.concatenate(alphas, axis=1) * acc_ref[...] + pv
        return jnp.concatenate(ms, axis=1)

    def trip(kt, m, n_pairs):
        for i in range(n_pairs):
            sel_logits(kt + 2 * i + 1, lgb_ref)
            m = softmax_pv(kt + 2 * i, lga_ref, m)
            sel_logits(kt + 2 * i + 2, lga_ref)
            m = softmax_pv(kt + 2 * i + 1, lgb_ref, m)
        return m

    acc_ref[...] = jnp.zeros(acc_ref.shape, F32)
    sel_logits(0, lga_ref)
    n_live = bi // per + 1
    n_long = n_live // (2 * SEL_LONG_PAIRS)
    m = lax.fori_loop(0, n_long, lambda j, m: trip(2 * SEL_LONG_PAIRS * j, m, SEL_LONG_PAIRS),
                      jnp.full((1, 4 * QB), NEG_INF, F32))
    first_short = 2 * SEL_LONG_PAIRS * n_long
    lax.fori_loop(0, (n_live - first_short + 1) // 2, lambda j, m: trip(first_short + 2 * j, m, 1), m)

    first = jnp.maximum(bi - (WIN_TILES - 1), 0)
    r = pl.multiple_of(first * QB, QB)
    lg = _dot_nt(kw_ref[pl.ds(r, WIN_TILES * QB), :], qs) + _lag_bias(twin_ref, bi, first, WIN_TILES)
    p = jnp.exp2(lg - jnp.max(lg, axis=0, keepdims=True)).astype(BF16)
    acc_w = _dot(_value_tiles(vwt_ref, first, WIN_TILES), p)

    gates = _gate_lanes(gc_ref[...], gexp_ref[...])
    o_ref[...] = (ocmp_ref[...] + gates[:, NSA_WIDTH:2 * NSA_WIDTH] * _finish_heads(acc_ref[...])
                  + gates[:, 2 * NSA_WIDTH:] * _finish_heads(acc_w))


def _value_rows(v, bsz, seq):
    vt = v[:, :HEAD_DIM].reshape(bsz, seq // QB, QB, HEAD_DIM).transpose(0, 1, 3, 2)
    return jnp.concatenate([vt, jnp.ones(vt.shape[:2] + (V_ROWS - HEAD_DIM, QB), vt.dtype)], axis=2)


def _nsa_attn(qc, nsel, ocmp, gc, ks, e_pen, vs, kw, vw, tsel, twin, gexp, bsz, seq):
    n = qc.shape[0]
    nb = seq // QB
    tok = lambda w: pl.BlockSpec((QB, w), lambda b, bi: (b * nb + bi, 0))
    seqblk = pl.BlockSpec((seq, LANES), lambda b, bi: (b, 0))
    valblk = pl.BlockSpec((None, nb, V_ROWS, QB), lambda b, bi: (b, 0, 0, 0))
    const = lambda a: pl.BlockSpec(a.shape, lambda b, bi: (0,) * a.ndim)
    return pl.pallas_call(
        _nsa_attn_kernel,
        grid=(bsz, nb),
        in_specs=[tok(256), tok(LANES), tok(256), tok(LANES),
                  seqblk, const(e_pen), valblk, seqblk, valblk,
                  const(tsel), const(twin), const(gexp)],
        out_specs=tok(256),
        out_shape=jax.ShapeDtypeStruct((n, 256), F32),
        scratch_shapes=[pltpu.VMEM((4 * QB, 2 * LANES), BF16), pltpu.VMEM((SEL_TK, 4 * QB), F32),
                        pltpu.VMEM((SEL_TK, 4 * QB), F32), pltpu.VMEM((V_ROWS, 4 * QB), F32)],
        compiler_params=_cparams("parallel", "arbitrary"),
        name="nsa_selected_window",
    )(qc, nsel, ocmp, gc, ks, e_pen, _value_rows(vs, bsz, seq), kw, _value_rows(vw, bsz, seq), tsel, twin, gexp)


def _rms(x, gain):
    return x * lax.rsqrt(jnp.mean(x * x, axis=-1, keepdims=True) + EPS) * gain


FFN_CHUNK = 1024


def _out_ffn_kernel(x_ref, oa_ref, y_ref, oc_ref, gluw_ref, glub_ref, gout_ref, wout_ref, g2_ref, wu_ref, wd_ref, o_ref):
    wb = y_ref.shape[1]
    ab = _dot(jax.nn.gelu(y_ref[...]).astype(BF16), gluw_ref[...]) + glub_ref[...]
    ob = ab[:, :wb] * jax.nn.sigmoid(ab[:, wb:])
    a0, a1 = SWA_WIDTH, SWA_WIDTH + wb
    mixed = jnp.concatenate([_rms(oa_ref[...], gout_ref[:, :a0]).astype(BF16),
                             _rms(ob, gout_ref[:, a0:a1]).astype(BF16),
                             _rms(oc_ref[...], gout_ref[:, a1:]).astype(BF16)], axis=1)
    x = x_ref[...] + _dot(mixed, wout_ref[...])
    h = _rms(x, g2_ref[...]).astype(BF16)
    acc = x
    for c in range(wu_ref.shape[1] // FFN_CHUNK):
        c0, c1 = c * FFN_CHUNK, (c + 1) * FFN_CHUNK
        hid = jnp.maximum(_dot(h, wu_ref[:, c0:c1]), 0.0)
        acc = acc + _dot((hid * hid).astype(BF16), wd_ref[c0:c1, :])
    o_ref[...] = acc


def _out_ffn(x2, oa, y, oc, gluw, glub, gout, wout, g2, wu, wd, tm):
    n, d = x2.shape
    tok = lambda w: pl.BlockSpec((tm, w), lambda i: (i, 0))
    full = lambda a: pl.BlockSpec(a.shape, lambda i: (0,) * a.ndim, pipeline_mode=pl.Buffered(1))
    return pl.pallas_call(
        _out_ffn_kernel,
        grid=(n // tm,),
        in_specs=[tok(d), tok(oa.shape[1]), tok(y.shape[1]), tok(oc.shape[1]),
                  full(gluw), full(glub), full(gout), full(wout), full(g2), full(wu), full(wd)],
        out_specs=tok(d),
        out_shape=jax.ShapeDtypeStruct((n, d), F32),
        compiler_params=_cparams("parallel"),
        name="out_proj_ffn",
    )(x2, oa, y, oc, gluw, glub, gout, wout, g2, wu, wd)


def _rel_bucket(dist):
    n = jnp.maximum(dist, 0)
    nf = jnp.maximum(n, 1).astype(F32)
    large = BUCKET_EXACT + (jnp.log(nf / BUCKET_EXACT) / math.log(BUCKET_MAX_DIST / BUCKET_EXACT)
                            * (NUM_BUCKETS - BUCKET_EXACT)).astype(jnp.int32)
    return jnp.where(n < BUCKET_EXACT, n, jnp.minimum(large, NUM_BUCKETS - 1))


def _bias_rows(tbl, dist, valid):
    onehot = jax.nn.one_hot(_rel_bucket(dist), NUM_BUCKETS, dtype=F32)
    b = jnp.einsum('...qkc,ch->...hqk', onehot, tbl.astype(F32), precision=HIGHEST)
    b = jnp.where(valid[..., None, :, :], b, NEG_INF)
    return b.reshape(b.shape[:-3] + (b.shape[-3] * b.shape[-2], b.shape[-1]))


def _bias_tables(rel_bias):
    swa_order = jnp.array([0, 2, 1, 3])
    tbl_a = rel_bias[:, :SWA_HEADS][:, swa_order]
    tbl_c = rel_bias[:, SWA_HEADS:]
    i = jnp.arange(QB)[:, None]
    d_swa = i - jnp.arange(2 * QB)[None, :] + QB
    bias_swa = _bias_rows(tbl_a, d_swa, (d_swa >= 0) & (d_swa < SWA_WINDOW))
    r = jnp.arange(CMP_PHASES)[:, None, None]
    c = jnp.arange(2 * LANES)[None, None, :]
    d_cmp = r * QB + i[None] - (c - LANES) * NSA_CMP_STRIDE - (NSA_CMP_BLOCK - 1)
    band_cmp = jnp.swapaxes(_bias_rows(tbl_c, d_cmp, d_cmp >= 0), 1, 2)
    j = jnp.arange(QB)[None, :]
    n_far = -(-(BUCKET_MAX_DIST + QB) // QB)
    d_sel = jnp.arange(n_far)[:, None, None] * QB + (i - j)[None]
    tsel = _bias_rows(tbl_c, d_sel, d_sel >= 0)
    d_win = jnp.arange(WIN_TILES)[:, None, None] * QB + (i - j)[None]
    twin = _bias_rows(tbl_c, d_win, (d_win >= 0) & (d_win < NSA_WINDOW))
    future = jnp.full((1,) + tsel.shape[1:], NEG_INF, F32)
    lag_table = lambda t: jnp.swapaxes(jnp.concatenate([future, t * LOG2E]), 1, 2)
    return bias_swa, band_cmp, lag_table(tsel), lag_table(twin)


def _nsa_constants(seq):
    m = seq // NSA_CMP_STRIDE
    cs = jnp.arange(m)[:, None] * NSA_CMP_STRIDE
    ss = jnp.arange(LANES)[None, :] * NSA_SEL_BLOCK
    ovl = ((cs < ss + NSA_SEL_BLOCK) & (cs + NSA_CMP_BLOCK > ss) & (cs < seq - NSA_CMP_STRIDE)).astype(BF16)
    key_blk = jnp.arange(seq)[:, None] // NSA_SEL_BLOCK
    e_pen = jnp.where(key_blk == jnp.arange(LANES)[None, :], SEL_PENALTY, 0.0).astype(BF16)
    col = jnp.arange(LANES)[:, None]
    lane = jnp.arange(NSA_BRANCHES * NSA_WIDTH)[None, :]
    br, hd = lane // NSA_WIDTH, (lane % NSA_WIDTH) // HEAD_DIM
    gexp = (col == hd * NSA_BRANCHES + br).astype(BF16)
    return ovl.T, e_pen, gexp


def _seg_matrix():
    r = jnp.arange(2 * LANES)
    return jnp.where((r[:, None] // HEAD_DIM) == (r[None, :] // HEAD_DIM), 1.0 / HEAD_DIM, 0.0).astype(BF16)


def _prep_w_in(w):
    d = w.shape[0]
    hd = HEAD_DIM
    ssm_w = d - SWA_WIDTH - NSA_WIDTH
    o_ka = SWA_WIDTH
    o_va = o_ka + SWA_KV_WIDTH
    o_u = o_va + SWA_KV_WIDTH
    o_qc = o_u + ssm_w
    o_kv = o_qc + NSA_WIDTH
    o_gc = o_kv + 6 * hd
    head = lambda off, h: w[:, off + h * hd: off + (h + 1) * hd]
    gates = w[:, o_gc:]
    cols = [head(0, 0), head(0, 2), head(0, 1), head(0, 3), w[:, o_ka:o_gc],
            gates, jnp.zeros((d, LANES - gates.shape[1]), w.dtype)]
    return jnp.concatenate(cols, axis=1).astype(BF16)


def _prep_qk_gains(qk_g):
    g = qk_g.astype(F32)
    t4 = lambda v: jnp.tile(v, 4)
    return jnp.stack([t4(g[0]), t4(g[1]), t4(g[2]), t4(g[4]), t4(g[5]),
                      t4(g[3]), t4(g[3]), t4(g[3])])


def kernel(x, norm1_g, w_in, qk_g, sinks, rel_bias, ssm_a_re, ssm_a_im, ssm_log_dt, ssm_b_re, ssm_b_im, ssm_c_re, ssm_c_im, ssm_d, glu_w, glu_b, cmp_pos, cmp_w1, cmp_w2, out_norm_g, w_out, norm2_g, w_up, w_down):
    bsz, seq, d = x.shape
    depth = w_in.shape[0]
    n = bsz * seq
    assert seq % SWA_TQ == 0 and seq // NSA_SEL_BLOCK <= LANES and d == 1024
    assert (seq // NSA_CMP_STRIDE) % LANES == 0 and seq % SEL_TK == 0 and seq >= WIN_TILES * QB
    tm = 512
    row = lambda v: v.astype(F32).reshape(1, -1)

    bias_swa, band_cmp, tsel, twin = _bias_tables(rel_bias)
    ovlt, e_pen, gexp = _nsa_constants(seq)
    seg = _seg_matrix()
    swa_order = jnp.array([0, 2, 1, 3])
    swa_cols = (swa_order[:, None] * HEAD_DIM + jnp.arange(HEAD_DIM)[None, :]).reshape(-1)

    qkg_all = jax.vmap(_prep_qk_gains)(qk_g)
    w_in_all = jax.vmap(_prep_w_in)(w_in)
    ssm_all = jax.vmap(_ssm_operators)(ssm_a_re, ssm_a_im, ssm_log_dt, ssm_b_re, ssm_b_im, ssm_c_re, ssm_c_im, ssm_d)
    cmp_all = jax.vmap(_prep_compress)(cmp_pos, cmp_w1, cmp_w2)
    sink_all = jnp.repeat(sinks.astype(F32)[:, swa_order], QB, axis=1)[..., None]
    gout_all = jnp.concatenate([out_norm_g[:, :SWA_WIDTH][:, swa_cols], out_norm_g[:, SWA_WIDTH:]], axis=1).astype(F32)
    w_out_all = jnp.concatenate([w_out[:, :SWA_WIDTH][:, swa_cols], w_out[:, SWA_WIDTH:]], axis=1).astype(BF16)
    glu_w_all, w_up_all, w_down_all = glu_w.astype(BF16), w_up.astype(BF16), w_down.astype(BF16)

    x2 = x.reshape(n, d)
    for l in range(depth):
        qkg = qkg_all[l]
        (qa, ka, va, u, qc, kvc, ks, vs, kw, vw, gc) = _in_proj(x2, row(norm1_g[l]), w_in_all[l], qkg, seg, tm)

        o_a = _swa(qa, ka, va, bias_swa, sink_all[l], bsz, seq)

        y = _s5(u, [op[l] for op in ssm_all], bsz, seq)

        pos, w1t, w1b, w2 = [c[l] for c in cmp_all]
        kk, vvt = _nsa_compress(kvc, pos, w1t, w1b, w2, qkg[5:6, :LANES], bsz, seq)
        o_cmp, nsel = _nsa_cmp(qc, kk, vvt, band_cmp, gc, ovlt, gexp, bsz, seq)
        o_c = _nsa_attn(qc, nsel, o_cmp, gc, ks, e_pen, vs, kw, vw, tsel, twin, gexp, bsz, seq)

        x2 = _out_ffn(x2, o_a, y, o_c, glu_w_all[l], row(glu_b[l]), gout_all[l:l + 1], w_out_all[l],
                      row(norm2_g[l]), w_up_all[l], w_down_all[l], tm)
    return x2.reshape(bsz, seq, d)
```

```python
import functools
import math

import jax
import jax.numpy as jnp
from jax import lax
from jax.experimental import pallas as pl
from jax.experimental.pallas import tpu as pltpu

F32 = jnp.float32
BF16 = jnp.bfloat16
HIGHEST = lax.Precision.HIGHEST

HEAD_DIM = 64
SWA_HEADS = 4
SWA_KV_HEADS = 2
SWA_WINDOW = 128
NSA_HEADS = 4
NSA_CMP_BLOCK = 32
NSA_CMP_STRIDE = 16
NSA_SEL_BLOCK = 64
NSA_TOP_N = 16
NSA_WINDOW = 512
NSA_BRANCHES = 3
SSM_GROUP_CH = 16
SSM_STATE = 64
NUM_BUCKETS = 32
BUCKET_EXACT = NUM_BUCKETS // 2
BUCKET_MAX_DIST = 1024
EPS = 1e-6
NEG_INF = -1e30
SEL_FORCE = 1e4
SEL_PENALTY = -30000.0
LOG2E = math.log2(math.e)

LANES = 128
QB = 128
SSM_CHUNK = 16
SSM_OCT = LANES // SSM_GROUP_CH
CMP_PHASES = LANES * NSA_CMP_STRIDE // QB
VMEM_LIMIT = 56 * 1024 * 1024

SWA_WIDTH = SWA_HEADS * HEAD_DIM
SWA_KV_WIDTH = SWA_KV_HEADS * HEAD_DIM
NSA_WIDTH = NSA_HEADS * HEAD_DIM


def _cparams(*sem):
    return pltpu.CompilerParams(dimension_semantics=sem, vmem_limit_bytes=VMEM_LIMIT)


def _dot(a, b):
    return jnp.dot(a, b, preferred_element_type=F32)


def _dot_nt(a, b):
    return lax.dot_general(a, b, (((1,), (1,)), ((), ())), preferred_element_type=F32)


def _dot_split(a, b):
    hi = a.astype(BF16)
    lo = (a - hi.astype(F32)).astype(BF16)
    return _dot(hi, b) + _dot(lo, b)


def _with_ones(v):
    return jnp.concatenate([v, jnp.ones_like(v)], axis=1)


def _lane_lo(rows):
    return lax.broadcasted_iota(jnp.int32, (rows, LANES), 1) < HEAD_DIM


def _stack_heads(q):
    lo = _lane_lo(q.shape[0])
    zero = jnp.zeros_like(q[:, :LANES])
    g0 = q[:, :LANES]
    g1 = q[:, LANES:]
    return jnp.concatenate([jnp.where(lo, g0, zero), jnp.where(lo, zero, g0),
                            jnp.where(lo, g1, zero), jnp.where(lo, zero, g1)], axis=0)


def _unstack_heads(o):
    n = o.shape[0] // 4
    lo = _lane_lo(n)
    return jnp.concatenate([jnp.where(lo, o[0:n], o[n:2 * n]),
                            jnp.where(lo, o[2 * n:3 * n], o[3 * n:4 * n])], axis=1)


def _seg_rms(p, seg, gain):
    ms = _dot_split(p * p, seg)
    return p * lax.rsqrt(ms + EPS) * gain


def _in_proj_kernel(x_ref, g_ref, w_ref, qkg_ref, seg_ref,
                    qa_ref, ka_ref, va_ref, u_ref, qc_ref, kvc_ref,
                    ks_ref, vs_ref, kw_ref, vw_ref, gc_ref):
    x = x_ref[...]
    ms = jnp.mean(x * x, axis=-1, keepdims=True)
    h = (x * lax.rsqrt(ms + EPS) * g_ref[...]).astype(BF16)
    seg = seg_ref[...]
    scale = HEAD_DIM ** -0.5

    full = _dot(h, w_ref[...])

    def proj(a, b):
        return full[:, a:b]

    lo = _lane_lo(x.shape[0])

    def dup_halves(kv):
        swapped = pltpu.roll(kv, HEAD_DIM, 1)
        return jnp.where(lo, kv, swapped), jnp.where(lo, swapped, kv)

    def dup_rms(p, gain):
        m = jnp.mean(p * p, axis=-1, keepdims=True)
        return p * lax.rsqrt(m + EPS) * (gain * LOG2E)

    qa_ref[...] = (_seg_rms(proj(0, 256), seg, qkg_ref[0:1, :]) * scale).astype(BF16)
    kva = proj(256, 512)
    ka_ref[...] = _seg_rms(kva, seg, qkg_ref[1:2, :])[:, :LANES].astype(BF16)
    va_ref[...] = kva[:, LANES:].astype(BF16)
    u_ref[...] = proj(512, 1024)
    qc_ref[...] = (_seg_rms(proj(1024, 1280), seg, qkg_ref[2:3, :]) * scale).astype(BF16)
    cmp_sel = proj(1280, 1536)
    kvc_ref[...] = cmp_sel[:, :LANES]
    ks, vs = dup_halves(cmp_sel[:, LANES:])
    ks_ref[...] = dup_rms(ks, qkg_ref[3:4, :LANES]).astype(BF16)
    vs_ref[...] = vs.astype(BF16)
    win_gate = proj(1536, 1792)
    kw, vw = dup_halves(win_gate[:, :LANES])
    kw_ref[...] = dup_rms(kw, qkg_ref[4:5, :LANES]).astype(BF16)
    vw_ref[...] = vw.astype(BF16)
    gc_ref[...] = win_gate[:, LANES:]


def _in_proj(x2, g1, w, qkg, seg, tm):
    n, d = x2.shape
    widths = [(256, BF16), (128, BF16), (128, BF16), (512, F32), (256, BF16), (128, F32),
              (128, BF16), (128, BF16), (128, BF16), (128, BF16), (128, F32)]
    full = lambda a: pl.BlockSpec(a.shape, lambda i: (0,) * a.ndim)
    return pl.pallas_call(
        _in_proj_kernel,
        grid=(n // tm,),
        in_specs=[pl.BlockSpec((tm, d), lambda i: (i, 0)), full(g1), full(w), full(qkg), full(seg)],
        out_specs=[pl.BlockSpec((tm, wd), lambda i: (i, 0)) for wd, _ in widths],
        out_shape=[jax.ShapeDtypeStruct((n, wd), dt) for wd, dt in widths],
        compiler_params=_cparams("parallel"),
        name="in_proj",
    )(x2, g1, w, qkg, seg)


SWA_TQ = 512


def _swa_kernel(q_ref, kc_ref, kp_ref, vc_ref, vp_ref, bias_ref, sink_ref, o_ref):
    first = pl.program_id(1) == 0
    prev_cols = lax.broadcasted_iota(jnp.int32, (1, 2 * QB), 1) < QB
    pen = jnp.where(first & prev_cols, NEG_INF, 0.0)
    sink = sink_ref[...]
    bias = bias_ref[...]
    for s in range(SWA_TQ // QB):
        r0, r1 = s * QB, (s + 1) * QB
        qs = _stack_heads(q_ref[r0:r1, :])
        k_cur = kc_ref[r0:r1, :]
        v_cur = vc_ref[r0:r1, :]
        if s == 0:
            k_prev, v_prev = kp_ref[...], vp_ref[...]
        else:
            k_prev, v_prev = kc_ref[r0 - QB:r0, :], vc_ref[r0 - QB:r0, :]
        lg = jnp.concatenate([_dot_nt(qs, k_prev), _dot_nt(qs, k_cur)], axis=1) + bias
        if s == 0:
            lg = lg + pen
        m = jnp.maximum(jnp.max(lg, axis=-1, keepdims=True), sink)
        p = jnp.exp(lg - m)
        denom = jnp.sum(p, axis=-1, keepdims=True) + jnp.exp(sink - m)
        pv = _dot(p[:, :QB].astype(BF16), v_prev) + _dot(p[:, QB:].astype(BF16), v_cur)
        o_ref[r0:r1, :] = _unstack_heads(pv / denom)


def _swa(qa, ka, va, bias, sink_rows, bsz, seq):
    n = qa.shape[0]
    nq = seq // SWA_TQ
    per = SWA_TQ // QB
    cur = lambda b, i: (b * nq + i, 0)
    prev = lambda b, i: (jnp.maximum((b * nq + i) * per - 1, 0), 0)
    const = lambda b, i: (0, 0)
    return pl.pallas_call(
        _swa_kernel,
        grid=(bsz, nq),
        in_specs=[pl.BlockSpec((SWA_TQ, 256), cur),
                  pl.BlockSpec((SWA_TQ, LANES), cur), pl.BlockSpec((QB, LANES), prev),
                  pl.BlockSpec((SWA_TQ, LANES), cur), pl.BlockSpec((QB, LANES), prev),
                  pl.BlockSpec(bias.shape, const), pl.BlockSpec(sink_rows.shape, const)],
        out_specs=pl.BlockSpec((SWA_TQ, 256), cur),
        out_shape=jax.ShapeDtypeStruct((n, 256), F32),
        compiler_params=_cparams("parallel", "parallel"),
        name="swa",
    )(qa, ka, ka, va, va, bias, sink_rows)


def _chunk_rows(u_ref):
    return jnp.concatenate([u_ref[:, t, :] for t in range(SSM_CHUNK)], axis=1)


def _ssm_z_kernel(u_ref, pz_ref, zre_ref, zim_ref):
    z = _dot(_chunk_rows(u_ref).astype(BF16), pz_ref[0])
    half = z.shape[1] // 2
    zre_ref[...] = z[:, :half]
    zim_ref[...] = z[:, half:]


def _u_spec(tnc):
    return pl.BlockSpec((None, tnc, SSM_CHUNK, LANES), lambda j, b, r: (b, r, 0, j))


def _state_spec(tnc, sw, noct):
    return pl.BlockSpec((tnc, sw), lambda j, b, r: (r, b * noct + j))


def _ssm_z(u4, pz, tnc):
    bsz, nch = u4.shape[:2]
    noct = pz.shape[0]
    sw = pz.shape[2] // 2
    state = _state_spec(tnc, sw, noct)
    return pl.pallas_call(
        _ssm_z_kernel,
        grid=(noct, bsz, nch // tnc),
        in_specs=[_u_spec(tnc), pl.BlockSpec((1,) + pz.shape[1:], lambda j, b, r: (j, 0, 0))],
        out_specs=[state, state],
        out_shape=[jax.ShapeDtypeStruct((nch, bsz * noct * sw), F32)] * 2,
        compiler_params=_cparams("parallel", "parallel", "parallel"),
        name="ssm_chunk_state",
    )(u4, pz)


def _ssm_scan_kernel(zre_ref, zim_ref, ar_ref, ai_ref, sre_ref, sim_ref):
    a_r = ar_ref[...]
    a_i = ai_ref[...]

    def body(c, carry):
        s_r, s_i = carry
        row = pl.ds(c, 1)
        sre_ref[row, :] = s_r
        sim_ref[row, :] = s_i
        return (a_r * s_r - a_i * s_i + zre_ref[row, :], a_r * s_i + a_i * s_r + zim_ref[row, :])

    zero = jnp.zeros(a_r.shape, F32)
    lax.fori_loop(0, zre_ref.shape[0], body, (zero, zero))


def _ssm_scan(zre, zim, a_r, a_i, tl):
    nchunk, width = zre.shape
    blk = pl.BlockSpec((nchunk, tl), lambda j: (0, j))
    coef = pl.BlockSpec((1, tl), lambda j: (0, j))
    return pl.pallas_call(
        _ssm_scan_kernel,
        grid=(width // tl,),
        in_specs=[blk, blk, coef, coef],
        out_specs=[blk, blk],
        out_shape=[jax.ShapeDtypeStruct(zre.shape, F32)] * 2,
        compiler_params=_cparams("parallel"),
        name="ssm_scan",
    )(zre, zim, a_r, a_i)


def _ssm_y_kernel(u_ref, sre_ref, sim_ref, tz_ref, c_ref, d_ref, y_ref):
    v = _chunk_rows(u_ref)
    vb = v.astype(BF16)
    state = jnp.concatenate([sre_ref[...], sim_ref[...]], axis=1).astype(BF16)
    w = 2 * LANES
    n_pairs = SSM_CHUNK // 2
    for tt in range(n_pairs):
        c0, c1 = tt * w, (tt + 1) * w
        acc = _dot(vb[:, :c1], tz_ref[0, (n_pairs - 1 - tt) * w:, :])
        acc = acc + _dot(state, c_ref[0, :, c0:c1]) + d_ref[0, :, c0:c1] * v[:, c0:c1]
        y_ref[:, 2 * tt, :] = acc[:, :LANES]
        y_ref[:, 2 * tt + 1, :] = acc[:, LANES:]


def _ssm_y(u4, sre, sim, mm, c_cat, dvec, tnc):
    bsz, nch = u4.shape[:2]
    noct = c_cat.shape[0]
    sw = c_cat.shape[1] // 2
    per_oct = lambda a: pl.BlockSpec((1,) + a.shape[1:], lambda j, b, r: (j,) + (0,) * (a.ndim - 1))
    state = _state_spec(tnc, sw, noct)
    return pl.pallas_call(
        _ssm_y_kernel,
        grid=(noct, bsz, nch // tnc),
        in_specs=[_u_spec(tnc), state, state, per_oct(mm), per_oct(c_cat), per_oct(dvec)],
        out_specs=_u_spec(tnc),
        out_shape=jax.ShapeDtypeStruct(u4.shape, F32),
        compiler_params=_cparams("parallel", "parallel", "parallel"),
        name="ssm_output",
    )(u4, sre, sim, mm, c_cat, dvec)


def _ssm_operators(a_re, a_im, log_dt, b_re, b_im, c_re, c_im, d_skip):
    g, n = a_re.shape
    p = SSM_GROUP_CH
    t = SSM_CHUNK
    o = SSM_OCT
    noct = g // o
    a = lax.complex(a_re.astype(F32), a_im.astype(F32))
    adt = a * jnp.exp(log_dt.astype(F32))[:, None]
    b_bar = ((jnp.exp(adt) - 1.0) / a)[..., None] * lax.complex(b_re.astype(F32), b_im.astype(F32))
    cm = lax.complex(c_re.astype(F32), c_im.astype(F32))
    pw = jnp.exp(adt[None] * jnp.arange(t + 1, dtype=F32)[:, None, None].astype(jnp.complex64))
    kern = jnp.einsum('gpn,tgn,gnq->tgpq', cm, pw[:t], b_bar, precision=HIGHEST).real
    bz = (pw[:t][::-1][:, :, :, None] * b_bar[None]).transpose(1, 0, 3, 2)
    cz = (cm[None] * pw[1:, :, None, :]).transpose(1, 3, 0, 2)
    oct_ = lambda x: x.reshape((noct, o) + x.shape[1:])

    def placement(reps, w):
        src = jnp.arange(reps * w)
        dst = (src // w)[None, :] * (o * w) + jnp.arange(o)[:, None] * w + (src % w)[None, :]
        return jax.nn.one_hot(dst, reps * o * w, dtype=F32)

    dlag = jnp.einsum('ljapq,apm->ljaqm', kern.reshape(t, noct, o, p, p), placement(1, p)).reshape(t, noct, o * p, o * p)
    dlag = jnp.concatenate([jnp.zeros_like(dlag[:1]), dlag], axis=0)

    def pair_block(dl):
        top = jnp.concatenate([dlag[2 * dl + 1], dlag[2 * dl + 2]], axis=-1)
        bot = jnp.concatenate([dlag[2 * dl], dlag[2 * dl + 1]], axis=-1)
        return jnp.concatenate([top, bot], axis=-2)

    m_oct = jnp.concatenate([pair_block(dl) for dl in reversed(range(t // 2))], axis=1)
    pz_part = lambda x: jnp.einsum('jasqn,anm->jsaqm', oct_(x), placement(1, n)).reshape(noct, t * o * p, o * n)
    c_part = lambda x: jnp.einsum('janx,axm->janm', oct_(x).reshape(noct, o, n, t * p),
                                  placement(t, p)).reshape(noct, o * n, t * o * p)
    pz = jnp.concatenate([pz_part(bz.real), pz_part(bz.imag)], axis=2)
    dvec = jnp.broadcast_to(d_skip.astype(F32).reshape(noct, 1, o, p), (noct, t, o, p)).reshape(noct, 1, t * o * p)
    a_chunk = pw[t].reshape(1, -1)
    c_cat = jnp.concatenate([c_part(cz.real), c_part(-cz.imag)], axis=1)
    return m_oct.astype(BF16), pz.astype(BF16), c_cat.astype(BF16), dvec, a_chunk.real, a_chunk.imag


def _s5(u, ops, bsz, seq):
    m_oct, pz, c_cat, dvec, a_r, a_i = ops
    nch = seq // SSM_CHUNK
    u4 = u.reshape(bsz, nch, SSM_CHUNK, u.shape[1])
    tnc = min(nch, 256)
    zre, zim = _ssm_z(u4, pz, tnc)
    coef = lambda c: jnp.tile(c, (1, bsz))
    sre, sim = _ssm_scan(zre, zim, coef(a_r), coef(a_i), 2048)
    return _ssm_y(u4, sre, sim, m_oct, c_cat, dvec, tnc).reshape(u.shape)


def _cmp_kernel(kvc_ref, pos_ref, w1t_ref, w1b_ref, w2_ref, gk_ref, kk_ref, vvt_ref):
    rows = kvc_ref.shape[0] // NSA_CMP_STRIDE
    top = jnp.zeros((rows, w1t_ref.shape[2]), F32)
    bot = top
    for tau in range(NSA_CMP_STRIDE):
        tok = kvc_ref[pl.ds(tau, rows, stride=NSA_CMP_STRIDE), :]
        top = top + _dot((tok + pos_ref[tau:tau + 1, :]).astype(BF16), w1t_ref[tau])
        bot = bot + _dot((tok + pos_ref[NSA_CMP_STRIDE + tau:NSA_CMP_STRIDE + tau + 1, :]).astype(BF16), w1b_ref[tau])
    hid = top + pltpu.roll(bot, rows - 1, 0)
    out = _dot(jax.nn.gelu(hid).astype(BF16), w2_ref[...])
    k = out[:, :LANES]
    ms = jnp.mean(k * k, axis=-1, keepdims=True)
    kk_ref[0] = (k * lax.rsqrt(ms + EPS) * gk_ref[...]).astype(BF16)
    vvt_ref[0] = out[:, LANES:].T[:HEAD_DIM].astype(BF16)


def _nsa_compress(kvc, pos, w1t, w1b, w2, gk, bsz, seq):
    rows = seq // NSA_CMP_STRIDE
    full = lambda a: pl.BlockSpec(a.shape, lambda b: (0,) * a.ndim)
    out = pl.BlockSpec((1, rows, LANES), lambda b: (b, 0, 0))
    return pl.pallas_call(
        _cmp_kernel,
        grid=(bsz,),
        in_specs=[pl.BlockSpec((seq, LANES), lambda b: (b, 0)), full(pos), full(w1t), full(w1b), full(w2), full(gk)],
        out_specs=[out, pl.BlockSpec((1, HEAD_DIM, rows), lambda b: (b, 0, 0))],
        out_shape=[jax.ShapeDtypeStruct((bsz, rows, LANES), BF16), jax.ShapeDtypeStruct((bsz, HEAD_DIM, rows), BF16)],
        compiler_params=_cparams("parallel"),
        name="nsa_compress",
    )(kvc, pos, w1t, w1b, w2, gk)


def _prep_compress(cmp_pos, cmp_w1, cmp_w2):
    eye = jnp.eye(2, dtype=F32)
    pos = cmp_pos.astype(F32).transpose(1, 0, 2).reshape(NSA_CMP_BLOCK, 2 * HEAD_DIM)
    w1 = cmp_w1.astype(F32).reshape(2, NSA_CMP_BLOCK, HEAD_DIM, -1)
    w1 = jnp.einsum('spdh,sz->psdzh', w1, eye).reshape(NSA_CMP_BLOCK, 2 * HEAD_DIM, -1).astype(BF16)
    w2 = jnp.concatenate([cmp_w2, cmp_w2], axis=-1).astype(F32)
    w2 = jnp.einsum('shd,sz->shzd', w2, eye).reshape(2 * w2.shape[1], 2 * LANES).astype(BF16)
    return pos, w1[:NSA_CMP_STRIDE], w1[NSA_CMP_STRIDE:], w2


def _gate_lanes(gc, gexp):
    return _dot_split(jax.nn.sigmoid(gc), gexp)


def _nsa_cmp_kernel(n_sel_blocks, q_ref, kk_ref, vvt_ref, band_ref, gc_ref, ovlt_ref, gexp_ref,
                    o_ref, nsel_ref):
    bi = pl.program_id(0)
    qs = _stack_heads(q_ref[...])
    band = band_ref[bi % CMP_PHASES]
    far = band[0:1, :]

    def attend(nt):
        def run():
            near = band if nt > 1 else band[LANES:]
            bias = jnp.concatenate([jnp.broadcast_to(far, ((nt - 2) * LANES, 4 * QB)), near], axis=0) if nt > 2 else near
            lg = _dot_nt(kk_ref[0, :nt * LANES, :], qs) + bias
            valid = lg > 0.5 * NEG_INF
            m = jnp.max(lg, axis=0, keepdims=True)
            p = jnp.where(valid, jnp.exp(lg - m), 0.0)
            denom = jnp.sum(p, axis=0, keepdims=True)
            p = p * (1.0 / jnp.where(denom > 0.0, denom, 1.0))
            o_t = _dot(vvt_ref[0, :, :nt * LANES], p.astype(BF16))
            p_sum = p[:, 0:QB] + p[:, QB:2 * QB] + p[:, 2 * QB:3 * QB] + p[:, 3 * QB:4 * QB]
            hi = p_sum.astype(BF16)
            lo = (p_sum - hi.astype(F32)).astype(BF16)
            ovl_t = ovlt_ref[:, :nt * LANES]
            return o_t, _dot(ovl_t, hi) + _dot(ovl_t, lo)
        return run

    o_t, imp_t = lax.switch(bi // CMP_PHASES, [attend(nt) for nt in range(1, kk_ref.shape[1] // LANES + 1)])
    o_cmp = jnp.concatenate([o_t[:, h * QB:(h + 1) * QB].T for h in range(NSA_HEADS)], axis=1)
    gates = _gate_lanes(gc_ref[...], gexp_ref[...])
    o_ref[...] = gates[:, :NSA_WIDTH] * o_cmp
    blk = lax.broadcasted_iota(jnp.int32, (LANES, QB), 0)
    cur = (bi * QB + lax.broadcasted_iota(jnp.int32, (LANES, QB), 1)) // NSA_SEL_BLOCK
    forced = (blk == 0) | (blk == cur) | (blk == cur - 1)
    st = jnp.where(forced, -3.4e38, jnp.where(blk > cur, -SEL_FORCE, imp_t))
    st = jnp.where(blk < n_sel_blocks, st, -3e38)
    jidx = blk.astype(F32)
    unsel = jnp.where(forced, 0.0, 1.0)
    for _ in range(min(NSA_TOP_N, n_sel_blocks) - 3):
        mx = jnp.max(st, axis=0, keepdims=True)
        first = jnp.min(jnp.where(st == mx, jidx, 1e9), axis=0, keepdims=True)
        pick = jidx == first
        unsel = jnp.where(pick, 0.0, unsel)
        st = jnp.where(pick, -3.4e38, st)
    nsel_ref[...] = unsel.T.astype(BF16)


def _nsa_cmp(qc, kk, vvt, band, gc, ovlt, gexp, bsz, seq):
    n = qc.shape[0]
    nb = seq // QB
    tok = lambda w: pl.BlockSpec((QB, w), lambda bi, b: (b * nb + bi, 0))
    seqblk = lambda a: pl.BlockSpec((1,) + a.shape[1:], lambda bi, b: (b, 0, 0))
    const = lambda a: pl.BlockSpec(a.shape, lambda bi, b: (0,) * a.ndim)
    return pl.pallas_call(
        functools.partial(_nsa_cmp_kernel, seq // NSA_SEL_BLOCK),
        grid=(nb, bsz),
        in_specs=[tok(256), seqblk(kk), seqblk(vvt), const(band), tok(LANES), const(ovlt), const(gexp)],
        out_specs=[tok(256), tok(LANES)],
        out_shape=[jax.ShapeDtypeStruct((n, 256), F32), jax.ShapeDtypeStruct((n, LANES), BF16)],
        compiler_params=_cparams("parallel", "parallel"),
        name="nsa_compressed_topn",
    )(qc, kk, vvt, band, gc, ovlt, gexp)


SEL_TK = 512
SEL_LONG_PAIRS = 2
WIN_TILES = NSA_WINDOW // QB + 1
V_ROWS = HEAD_DIM + 16


def _lag_bias(tbl_ref, bi, first_tile, n_tiles):
    last = tbl_ref.shape[0] - 2
    return jnp.concatenate([tbl_ref[jnp.clip(bi - (first_tile + i), -1, last) + 1] for i in range(n_tiles)], axis=0)


def _value_tiles(vt_ref, first_tile, n_tiles):
    return jnp.concatenate([vt_ref[first_tile + i] for i in range(n_tiles)], axis=1)


def _finish_heads(acc):
    o_t = acc[:HEAD_DIM] / acc[HEAD_DIM:HEAD_DIM + 1]
    return jnp.concatenate([o_t[:, h * QB:(h + 1) * QB].T for h in range(NSA_HEADS)], axis=1)


ATTN_QBLOCKS = 2
ATTN_LANES = ATTN_QBLOCKS * NSA_HEADS * QB


def _block_bias(tbl_ref, bi0, first_tile, n_tiles):
    return jnp.concatenate([_lag_bias(tbl_ref, bi0 + g, first_tile, n_tiles) for g in range(ATTN_QBLOCKS)], axis=1)


def _nsa_attn_kernel(q_ref, nsel_ref, ocmp_ref, gc_ref, ks_ref, e_ref, vst_ref, kw_ref, vwt_ref,
                     tsel_ref, twin_ref, gexp_ref, o_ref, qaug_ref, lga_ref, lgb_ref, acc_ref):
    bi0 = pl.program_id(1) * ATTN_QBLOCKS
    blocks = [slice(g * QB, (g + 1) * QB) for g in range(ATTN_QBLOCKS)]
    qs = jnp.concatenate([_stack_heads(q_ref[rows, :]) for rows in blocks], axis=0)
    nsel = jnp.concatenate([nsel_ref[rows, :] for rows in blocks for _ in range(NSA_HEADS)], axis=0)
    qaug_ref[...] = jnp.concatenate([qs, nsel], axis=1)
    per = SEL_TK // QB
    n_tiles = ks_ref.shape[0] // SEL_TK

    def clamp(kt):
        return jnp.minimum(kt, n_tiles - 1)

    def sel_logits(kt, lg_ref):
        rows = pl.ds(pl.multiple_of(clamp(kt) * SEL_TK, SEL_TK), SEL_TK)
        k_aug = jnp.concatenate([ks_ref[rows, :], e_ref[rows, :]], axis=1)
        lg_ref[...] = _dot_nt(k_aug, qaug_ref[...]) + _block_bias(tsel_ref, bi0, kt * per, per)

    def softmax_pv(kt, lg_ref, m):
        ps, ms, alphas = [], [], []
        for c in range(ATTN_LANES // QB):
            cols = slice(c * QB, (c + 1) * QB)
            lg = lg_ref[:, cols]
            m_new = jnp.maximum(m[:, cols], jnp.max(lg, axis=0, keepdims=True))
            ms.append(m_new)
            alphas.append(jnp.exp2(m[:, cols] - m_new))
            ps.append(jnp.exp2(lg - m_new).astype(BF16))
        pv = _dot(_value_tiles(vst_ref, clamp(kt) * per, per), jnp.concatenate(ps, axis=1))
        acc_ref[...] = jnp.concatenate(alphas, axis=1) * acc_ref[...] + pv
        return jnp.concatenate(ms, axis=1)

    def trip(kt, m, n_pairs):
        for i in range(n_pairs):
            sel_logits(kt + 2 * i + 1, lgb_ref)
            m = softmax_pv(kt + 2 * i, lga_ref, m)
            sel_logits(kt + 2 * i + 2, lga_ref)
            m = softmax_pv(kt + 2 * i + 1, lgb_ref, m)
        return m

    acc_ref[...] = jnp.zeros(acc_ref.shape, F32)
    sel_logits(0, lga_ref)
    n_live = (bi0 + ATTN_QBLOCKS - 1) // per + 1
    n_long = n_live // (2 * SEL_LONG_PAIRS)
    m = lax.fori_loop(0, n_long, lambda j, m: trip(2 * SEL_LONG_PAIRS * j, m, SEL_LONG_PAIRS),
                      jnp.full((1, ATTN_LANES), NEG_INF, F32))
    first_short = 2 * SEL_LONG_PAIRS * n_long
    lax.fori_loop(0, (n_live - first_short + 1) // 2, lambda j, m: trip(first_short + 2 * j, m, 1), m)

    n_win = WIN_TILES + ATTN_QBLOCKS - 1
    first = jnp.maximum(bi0 - (WIN_TILES - 1), 0)
    r = pl.multiple_of(first * QB, QB)
    lg = _dot_nt(kw_ref[pl.ds(r, n_win * QB), :], qs) + _block_bias(twin_ref, bi0, first, n_win)
    p = jnp.exp2(lg - jnp.max(lg, axis=0, keepdims=True)).astype(BF16)
    acc_w = _dot(_value_tiles(vwt_ref, first, n_win), p)

    def finish(acc):
        w = NSA_HEADS * QB
        return jnp.concatenate([_finish_heads(acc[:, g * w:(g + 1) * w]) for g in range(ATTN_QBLOCKS)], axis=0)

    gates = _gate_lanes(gc_ref[...], gexp_ref[...])
    o_ref[...] = (ocmp_ref[...] + gates[:, NSA_WIDTH:2 * NSA_WIDTH] * finish(acc_ref[...])
                  + gates[:, 2 * NSA_WIDTH:] * finish(acc_w))


def _value_rows(v, bsz, seq):
    vt = v[:, :HEAD_DIM].reshape(bsz, seq // QB, QB, HEAD_DIM).transpose(0, 1, 3, 2)
    return jnp.concatenate([vt, jnp.ones(vt.shape[:2] + (V_ROWS - HEAD_DIM, QB), vt.dtype)], axis=2)


def _nsa_attn(qc, nsel, ocmp, gc, ks, e_pen, vs, kw, vw, tsel, twin, gexp, bsz, seq):
    n = qc.shape[0]
    nb = seq // QB
    steps = nb // ATTN_QBLOCKS
    tok = lambda w: pl.BlockSpec((ATTN_QBLOCKS * QB, w), lambda b, i: (b * steps + i, 0))
    seqblk = pl.BlockSpec((seq, LANES), lambda b, i: (b, 0))
    valblk = pl.BlockSpec((None, nb, V_ROWS, QB), lambda b, i: (b, 0, 0, 0))
    const = lambda a: pl.BlockSpec(a.shape, lambda b, i: (0,) * a.ndim)
    return pl.pallas_call(
        _nsa_attn_kernel,
        grid=(bsz, steps),
        in_specs=[tok(256), tok(LANES), tok(256), tok(LANES),
                  seqblk, const(e_pen), valblk, seqblk, valblk,
                  const(tsel), const(twin), const(gexp)],
        out_specs=tok(256),
        out_shape=jax.ShapeDtypeStruct((n, 256), F32),
        scratch_shapes=[pltpu.VMEM((ATTN_LANES, 2 * LANES), BF16), pltpu.VMEM((SEL_TK, ATTN_LANES), F32),
                        pltpu.VMEM((SEL_TK, ATTN_LANES), F32), pltpu.VMEM((V_ROWS, ATTN_LANES), F32)],
        compiler_params=_cparams("parallel", "arbitrary"),
        name="nsa_selected_window",
    )(qc, nsel, ocmp, gc, ks, e_pen, _value_rows(vs, bsz, seq), kw, _value_rows(vw, bsz, seq), tsel, twin, gexp)


def _rms(x, gain):
    return x * lax.rsqrt(jnp.mean(x * x, axis=-1, keepdims=True) + EPS) * gain


FFN_CHUNK = 1024


def _out_ffn_kernel(x_ref, oa_ref, y_ref, oc_ref, gluw_ref, glub_ref, gout_ref, wout_ref, g2_ref, wu_ref, wd_ref, o_ref):
    wb = y_ref.shape[1]
    ab = _dot(jax.nn.gelu(y_ref[...]).astype(BF16), gluw_ref[...]) + glub_ref[...]
    ob = ab[:, :wb] * jax.nn.sigmoid(ab[:, wb:])
    a0, a1 = SWA_WIDTH, SWA_WIDTH + wb
    mixed = jnp.concatenate([_rms(oa_ref[...], gout_ref[:, :a0]).astype(BF16),
                             _rms(ob, gout_ref[:, a0:a1]).astype(BF16),
                             _rms(oc_ref[...], gout_ref[:, a1:]).astype(BF16)], axis=1)
    x = x_ref[...] + _dot(mixed, wout_ref[...])
    h = _rms(x, g2_ref[...]).astype(BF16)
    acc = x
    for c in range(wu_ref.shape[1] // FFN_CHUNK):
        c0, c1 = c * FFN_CHUNK, (c + 1) * FFN_CHUNK
        hid = jnp.maximum(_dot(h, wu_ref[:, c0:c1]), 0.0)
        acc = acc + _dot((hid * hid).astype(BF16), wd_ref[c0:c1, :])
    o_ref[...] = acc


def _out_ffn(x2, oa, y, oc, gluw, glub, gout, wout, g2, wu, wd, tm):
    n, d = x2.shape
    tok = lambda w: pl.BlockSpec((tm, w), lambda i: (i, 0))
    full = lambda a: pl.BlockSpec(a.shape, lambda i: (0,) * a.ndim, pipeline_mode=pl.Buffered(1))
    return pl.pallas_call(
        _out_ffn_kernel,
        grid=(n // tm,),
        in_specs=[tok(d), tok(oa.shape[1]), tok(y.shape[1]), tok(oc.shape[1]),
                  full(gluw), full(glub), full(gout), full(wout), full(g2), full(wu), full(wd)],
        out_specs=tok(d),
        out_shape=jax.ShapeDtypeStruct((n, d), F32),
        compiler_params=_cparams("parallel"),
        name="out_proj_ffn",
    )(x2, oa, y, oc, gluw, glub, gout, wout, g2, wu, wd)


def _rel_bucket(dist):
    n = jnp.maximum(dist, 0)
    nf = jnp.maximum(n, 1).astype(F32)
    large = BUCKET_EXACT + (jnp.log(nf / BUCKET_EXACT) / math.log(BUCKET_MAX_DIST / BUCKET_EXACT)
                            * (NUM_BUCKETS - BUCKET_EXACT)).astype(jnp.int32)
    return jnp.where(n < BUCKET_EXACT, n, jnp.minimum(large, NUM_BUCKETS - 1))


def _bias_rows(tbl, dist, valid):
    onehot = jax.nn.one_hot(_rel_bucket(dist), NUM_BUCKETS, dtype=F32)
    b = jnp.einsum('...qkc,ch->...hqk', onehot, tbl.astype(F32), precision=HIGHEST)
    b = jnp.where(valid[..., None, :, :], b, NEG_INF)
    return b.reshape(b.shape[:-3] + (b.shape[-3] * b.shape[-2], b.shape[-1]))


def _bias_tables(rel_bias):
    swa_order = jnp.array([0, 2, 1, 3])
    tbl_a = rel_bias[:, :SWA_HEADS][:, swa_order]
    tbl_c = rel_bias[:, SWA_HEADS:]
    i = jnp.arange(QB)[:, None]
    d_swa = i - jnp.arange(2 * QB)[None, :] + QB
    bias_swa = _bias_rows(tbl_a, d_swa, (d_swa >= 0) & (d_swa < SWA_WINDOW))
    r = jnp.arange(CMP_PHASES)[:, None, None]
    c = jnp.arange(2 * LANES)[None, None, :]
    d_cmp = r * QB + i[None] - (c - LANES) * NSA_CMP_STRIDE - (NSA_CMP_BLOCK - 1)
    band_cmp = jnp.swapaxes(_bias_rows(tbl_c, d_cmp, d_cmp >= 0), 1, 2)
    j = jnp.arange(QB)[None, :]
    n_far = -(-(BUCKET_MAX_DIST + QB) // QB)
    d_sel = jnp.arange(n_far)[:, None, None] * QB + (i - j)[None]
    tsel = _bias_rows(tbl_c, d_sel, d_sel >= 0)
    d_win = jnp.arange(WIN_TILES)[:, None, None] * QB + (i - j)[None]
    twin = _bias_rows(tbl_c, d_win, (d_win >= 0) & (d_win < NSA_WINDOW))
    future = jnp.full((1,) + tsel.shape[1:], NEG_INF, F32)
    lag_table = lambda t: jnp.swapaxes(jnp.concatenate([future, t * LOG2E]), 1, 2)
    return bias_swa, band_cmp, lag_table(tsel), lag_table(jnp.concatenate([twin, future / LOG2E]))


def _nsa_constants(seq):
    m = seq // NSA_CMP_STRIDE
    cs = jnp.arange(m)[:, None] * NSA_CMP_STRIDE
    ss = jnp.arange(LANES)[None, :] * NSA_SEL_BLOCK
    ovl = ((cs < ss + NSA_SEL_BLOCK) & (cs + NSA_CMP_BLOCK > ss) & (cs < seq - NSA_CMP_STRIDE)).astype(BF16)
    key_blk = jnp.arange(seq)[:, None] // NSA_SEL_BLOCK
    e_pen = jnp.where(key_blk == jnp.arange(LANES)[None, :], SEL_PENALTY, 0.0).astype(BF16)
    col = jnp.arange(LANES)[:, None]
    lane = jnp.arange(NSA_BRANCHES * NSA_WIDTH)[None, :]
    br, hd = lane // NSA_WIDTH, (lane % NSA_WIDTH) // HEAD_DIM
    gexp = (col == hd * NSA_BRANCHES + br).astype(BF16)
    return ovl.T, e_pen, gexp


def _seg_matrix():
    r = jnp.arange(2 * LANES)
    return jnp.where((r[:, None] // HEAD_DIM) == (r[None, :] // HEAD_DIM), 1.0 / HEAD_DIM, 0.0).astype(BF16)


def _prep_w_in(w):
    d = w.shape[0]
    hd = HEAD_DIM
    ssm_w = d - SWA_WIDTH - NSA_WIDTH
    o_ka = SWA_WIDTH
    o_va = o_ka + SWA_KV_WIDTH
    o_u = o_va + SWA_KV_WIDTH
    o_qc = o_u + ssm_w
    o_kv = o_qc + NSA_WIDTH
    o_gc = o_kv + 6 * hd
    head = lambda off, h: w[:, off + h * hd: off + (h + 1) * hd]
    gates = w[:, o_gc:]
    cols = [head(0, 0), head(0, 2), head(0, 1), head(0, 3), w[:, o_ka:o_gc],
            gates, jnp.zeros((d, LANES - gates.shape[1]), w.dtype)]
    return jnp.concatenate(cols, axis=1).astype(BF16)


def _prep_qk_gains(qk_g):
    g = qk_g.astype(F32)
    t4 = lambda v: jnp.tile(v, 4)
    return jnp.stack([t4(g[0]), t4(g[1]), t4(g[2]), t4(g[4]), t4(g[5]),
                      t4(g[3]), t4(g[3]), t4(g[3])])


def kernel(x, norm1_g, w_in, qk_g, sinks, rel_bias, ssm_a_re, ssm_a_im, ssm_log_dt, ssm_b_re, ssm_b_im, ssm_c_re, ssm_c_im, ssm_d, glu_w, glu_b, cmp_pos, cmp_w1, cmp_w2, out_norm_g, w_out, norm2_g, w_up, w_down):
    bsz, seq, d = x.shape
    depth = w_in.shape[0]
    n = bsz * seq
    assert seq % SWA_TQ == 0 and seq // NSA_SEL_BLOCK <= LANES and d == 1024
    assert (seq // NSA_CMP_STRIDE) % LANES == 0 and seq % SEL_TK == 0
    assert seq >= (WIN_TILES + ATTN_QBLOCKS - 1) * QB and (seq // QB) % ATTN_QBLOCKS == 0
    tm = 512
    row = lambda v: v.astype(F32).reshape(1, -1)

    bias_swa, band_cmp, tsel, twin = _bias_tables(rel_bias)
    ovlt, e_pen, gexp = _nsa_constants(seq)
    seg = _seg_matrix()
    swa_order = jnp.array([0, 2, 1, 3])
    swa_cols = (swa_order[:, None] * HEAD_DIM + jnp.arange(HEAD_DIM)[None, :]).reshape(-1)

    qkg_all = jax.vmap(_prep_qk_gains)(qk_g)
    w_in_all = jax.vmap(_prep_w_in)(w_in)
    ssm_all = jax.vmap(_ssm_operators)(ssm_a_re, ssm_a_im, ssm_log_dt, ssm_b_re, ssm_b_im, ssm_c_re, ssm_c_im, ssm_d)
    cmp_all = jax.vmap(_prep_compress)(cmp_pos, cmp_w1, cmp_w2)
    sink_all = jnp.repeat(sinks.astype(F32)[:, swa_order], QB, axis=1)[..., None]
    gout_all = jnp.concatenate([out_norm_g[:, :SWA_WIDTH][:, swa_cols], out_norm_g[:, SWA_WIDTH:]], axis=1).astype(F32)
    w_out_all = jnp.concatenate([w_out[:, :SWA_WIDTH][:, swa_cols], w_out[:, SWA_WIDTH:]], axis=1).astype(BF16)
    glu_w_all, w_up_all, w_down_all = glu_w.astype(BF16), w_up.astype(BF16), w_down.astype(BF16)

    x2 = x.reshape(n, d)
    for l in range(depth):
        qkg = qkg_all[l]
        (qa, ka, va, u, qc, kvc, ks, vs, kw, vw, gc) = _in_proj(x2, row(norm1_g[l]), w_in_all[l], qkg, seg, tm)

        o_a = _swa(qa, ka, va, bias_swa, sink_all[l], bsz, seq)

        y = _s5(u, [op[l] for op in ssm_all], bsz, seq)

        pos, w1t, w1b, w2 = [c[l] for c in cmp_all]
        kk, vvt = _nsa_compress(kvc, pos, w1t, w1b, w2, qkg[5:6, :LANES], bsz, seq)
        o_cmp, nsel = _nsa_cmp(qc, kk, vvt, band_cmp, gc, ovlt, gexp, bsz, seq)
        o_c = _nsa_attn(qc, nsel, o_cmp, gc, ks, e_pen, vs, kw, vw, tsel, twin, gexp, bsz, seq)

        x2 = _out_ffn(x2, o_a, y, o_c, glu_w_all[l], row(glu_b[l]), gout_all[l:l + 1], w_out_all[l],
                      row(norm2_g[l]), w_up_all[l], w_down_all[l], tm)
    return x2.reshape(bsz, seq, d)
```

```python
import functools
import math

import jax
import jax.numpy as jnp
from jax import lax
from jax.experimental import pallas as pl
from jax.experimental.pallas import tpu as pltpu

F32 = jnp.float32
BF16 = jnp.bfloat16
HIGHEST = lax.Precision.HIGHEST

HEAD_DIM = 64
SWA_HEADS = 4
SWA_KV_HEADS = 2
SWA_WINDOW = 128
NSA_HEADS = 4
NSA_CMP_BLOCK = 32
NSA_CMP_STRIDE = 16
NSA_SEL_BLOCK = 64
NSA_TOP_N = 16
NSA_WINDOW = 512
NSA_BRANCHES = 3
SSM_GROUP_CH = 16
SSM_STATE = 64
NUM_BUCKETS = 32
BUCKET_EXACT = NUM_BUCKETS // 2
BUCKET_MAX_DIST = 1024
EPS = 1e-6
NEG_INF = -1e30
SEL_FORCE = 1e4
SEL_PENALTY = -30000.0
LOG2E = math.log2(math.e)

LANES = 128
QB = 128
SSM_CHUNK = 16
SSM_OCT = LANES // SSM_GROUP_CH
CMP_PHASES = LANES * NSA_CMP_STRIDE // QB
CMP_QBLOCKS = 2
VMEM_LIMIT = 56 * 1024 * 1024

SWA_WIDTH = SWA_HEADS * HEAD_DIM
SWA_KV_WIDTH = SWA_KV_HEADS * HEAD_DIM
NSA_WIDTH = NSA_HEADS * HEAD_DIM


def _cparams(*sem):
    return pltpu.CompilerParams(dimension_semantics=sem, vmem_limit_bytes=VMEM_LIMIT)


def _dot(a, b):
    return jnp.dot(a, b, preferred_element_type=F32)


def _dot_nt(a, b):
    return lax.dot_general(a, b, (((1,), (1,)), ((), ())), preferred_element_type=F32)


def _dot_split(a, b):
    hi = a.astype(BF16)
    lo = (a - hi.astype(F32)).astype(BF16)
    return _dot(hi, b) + _dot(lo, b)


def _with_ones(v):
    return jnp.concatenate([v, jnp.ones_like(v)], axis=1)


def _lane_lo(rows):
    return lax.broadcasted_iota(jnp.int32, (rows, LANES), 1) < HEAD_DIM


def _stack_heads(q):
    lo = _lane_lo(q.shape[0])
    zero = jnp.zeros_like(q[:, :LANES])
    g0 = q[:, :LANES]
    g1 = q[:, LANES:]
    return jnp.concatenate([jnp.where(lo, g0, zero), jnp.where(lo, zero, g0),
                            jnp.where(lo, g1, zero), jnp.where(lo, zero, g1)], axis=0)


def _unstack_heads(o):
    n = o.shape[0] // 4
    lo = _lane_lo(n)
    return jnp.concatenate([jnp.where(lo, o[0:n], o[n:2 * n]),
                            jnp.where(lo, o[2 * n:3 * n], o[3 * n:4 * n])], axis=1)


def _seg_rms(p, seg, gain):
    ms = _dot_split(p * p, seg)
    return p * lax.rsqrt(ms + EPS) * gain


def _in_proj_kernel(x_ref, g_ref, w_ref, qkg_ref, seg_ref,
                    qa_ref, ka_ref, va_ref, u_ref, qc_ref, kvc_ref,
                    ks_ref, vs_ref, kw_ref, vw_ref, gc_ref):
    x = x_ref[...]
    ms = jnp.mean(x * x, axis=-1, keepdims=True)
    h = (x * lax.rsqrt(ms + EPS) * g_ref[...]).astype(BF16)
    seg = seg_ref[...]
    scale = HEAD_DIM ** -0.5

    full = _dot(h, w_ref[...])

    def proj(a, b):
        return full[:, a:b]

    lo = _lane_lo(x.shape[0])

    def dup_halves(kv):
        swapped = pltpu.roll(kv, HEAD_DIM, 1)
        return jnp.where(lo, kv, swapped), jnp.where(lo, swapped, kv)

    def dup_rms(p, gain):
        m = jnp.mean(p * p, axis=-1, keepdims=True)
        return p * lax.rsqrt(m + EPS) * (gain * LOG2E)

    qa_ref[...] = (_seg_rms(proj(0, 256), seg, qkg_ref[0:1, :]) * scale).astype(BF16)
    kva = proj(256, 512)
    ka_ref[...] = _seg_rms(kva, seg, qkg_ref[1:2, :])[:, :LANES].astype(BF16)
    va_ref[...] = kva[:, LANES:].astype(BF16)
    u_ref[...] = proj(512, 1024)
    qc_ref[...] = (_seg_rms(proj(1024, 1280), seg, qkg_ref[2:3, :]) * scale).astype(BF16)
    cmp_sel = proj(1280, 1536)
    kvc_ref[...] = cmp_sel[:, :LANES]
    ks, vs = dup_halves(cmp_sel[:, LANES:])
    ks_ref[...] = dup_rms(ks, qkg_ref[3:4, :LANES]).astype(BF16)
    vs_ref[...] = vs.astype(BF16)
    win_gate = proj(1536, 1792)
    kw, vw = dup_halves(win_gate[:, :LANES])
    kw_ref[...] = dup_rms(kw, qkg_ref[4:5, :LANES]).astype(BF16)
    vw_ref[...] = vw.astype(BF16)
    gc_ref[...] = win_gate[:, LANES:]


def _in_proj(x2, g1, w, qkg, seg, tm):
    n, d = x2.shape
    widths = [(256, BF16), (128, BF16), (128, BF16), (512, F32), (256, BF16), (128, F32),
              (128, BF16), (128, BF16), (128, BF16), (128, BF16), (128, F32)]
    full = lambda a: pl.BlockSpec(a.shape, lambda i: (0,) * a.ndim)
    return pl.pallas_call(
        _in_proj_kernel,
        grid=(n // tm,),
        in_specs=[pl.BlockSpec((tm, d), lambda i: (i, 0)), full(g1), full(w), full(qkg), full(seg)],
        out_specs=[pl.BlockSpec((tm, wd), lambda i: (i, 0)) for wd, _ in widths],
        out_shape=[jax.ShapeDtypeStruct((n, wd), dt) for wd, dt in widths],
        compiler_params=_cparams("parallel"),
        name="in_proj",
    )(x2, g1, w, qkg, seg)


SWA_TQ = 512


def _swa_kernel(q_ref, kc_ref, kp_ref, vc_ref, vp_ref, bias_ref, sink_ref, o_ref):
    first = pl.program_id(1) == 0
    prev_cols = lax.broadcasted_iota(jnp.int32, (1, 2 * QB), 1) < QB
    pen = jnp.where(first & prev_cols, NEG_INF, 0.0)
    sink = sink_ref[...]
    bias = bias_ref[...]
    for s in range(SWA_TQ // QB):
        r0, r1 = s * QB, (s + 1) * QB
        qs = _stack_heads(q_ref[r0:r1, :])
        k_cur = kc_ref[r0:r1, :]
        v_cur = vc_ref[r0:r1, :]
        if s == 0:
            k_prev, v_prev = kp_ref[...], vp_ref[...]
        else:
            k_prev, v_prev = kc_ref[r0 - QB:r0, :], vc_ref[r0 - QB:r0, :]
        lg = jnp.concatenate([_dot_nt(qs, k_prev), _dot_nt(qs, k_cur)], axis=1) + bias
        if s == 0:
            lg = lg + pen
        m = jnp.maximum(jnp.max(lg, axis=-1, keepdims=True), sink)
        p = jnp.exp(lg - m)
        denom = jnp.sum(p, axis=-1, keepdims=True) + jnp.exp(sink - m)
        pv = _dot(p[:, :QB].astype(BF16), v_prev) + _dot(p[:, QB:].astype(BF16), v_cur)
        o_ref[r0:r1, :] = _unstack_heads(pv / denom)


def _swa(qa, ka, va, bias, sink_rows, bsz, seq):
    n = qa.shape[0]
    nq = seq // SWA_TQ
    per = SWA_TQ // QB
    cur = lambda b, i: (b * nq + i, 0)
    prev = lambda b, i: (jnp.maximum((b * nq + i) * per - 1, 0), 0)
    const = lambda b, i: (0, 0)
    return pl.pallas_call(
        _swa_kernel,
        grid=(bsz, nq),
        in_specs=[pl.BlockSpec((SWA_TQ, 256), cur),
                  pl.BlockSpec((SWA_TQ, LANES), cur), pl.BlockSpec((QB, LANES), prev),
                  pl.BlockSpec((SWA_TQ, LANES), cur), pl.BlockSpec((QB, LANES), prev),
                  pl.BlockSpec(bias.shape, const), pl.BlockSpec(sink_rows.shape, const)],
        out_specs=pl.BlockSpec((SWA_TQ, 256), cur),
        out_shape=jax.ShapeDtypeStruct((n, 256), F32),
        compiler_params=_cparams("parallel", "parallel"),
        name="swa",
    )(qa, ka, ka, va, va, bias, sink_rows)


def _chunk_rows(u_ref):
    return jnp.concatenate([u_ref[:, t, :] for t in range(SSM_CHUNK)], axis=1)


def _ssm_z_kernel(u_ref, pz_ref, zre_ref, zim_ref):
    z = _dot(_chunk_rows(u_ref).astype(BF16), pz_ref[0])
    half = z.shape[1] // 2
    zre_ref[...] = z[:, :half]
    zim_ref[...] = z[:, half:]


def _u_spec(tnc):
    return pl.BlockSpec((None, tnc, SSM_CHUNK, LANES), lambda j, b, r: (b, r, 0, j))


def _state_spec(tnc, sw, noct):
    return pl.BlockSpec((tnc, sw), lambda j, b, r: (r, b * noct + j))


def _ssm_z(u4, pz, tnc):
    bsz, nch = u4.shape[:2]
    noct = pz.shape[0]
    sw = pz.shape[2] // 2
    state = _state_spec(tnc, sw, noct)
    return pl.pallas_call(
        _ssm_z_kernel,
        grid=(noct, bsz, nch // tnc),
        in_specs=[_u_spec(tnc), pl.BlockSpec((1,) + pz.shape[1:], lambda j, b, r: (j, 0, 0))],
        out_specs=[state, state],
        out_shape=[jax.ShapeDtypeStruct((nch, bsz * noct * sw), F32)] * 2,
        compiler_params=_cparams("parallel", "parallel", "parallel"),
        name="ssm_chunk_state",
    )(u4, pz)


def _ssm_scan_kernel(zre_ref, zim_ref, ar_ref, ai_ref, sre_ref, sim_ref):
    a_r = ar_ref[...]
    a_i = ai_ref[...]

    def body(c, carry):
        s_r, s_i = carry
        row = pl.ds(c, 1)
        sre_ref[row, :] = s_r
        sim_ref[row, :] = s_i
        return (a_r * s_r - a_i * s_i + zre_ref[row, :], a_r * s_i + a_i * s_r + zim_ref[row, :])

    zero = jnp.zeros(a_r.shape, F32)
    lax.fori_loop(0, zre_ref.shape[0], body, (zero, zero))


def _ssm_scan(zre, zim, a_r, a_i, tl):
    nchunk, width = zre.shape
    blk = pl.BlockSpec((nchunk, tl), lambda j: (0, j))
    coef = pl.BlockSpec((1, tl), lambda j: (0, j))
    return pl.pallas_call(
        _ssm_scan_kernel,
        grid=(width // tl,),
        in_specs=[blk, blk, coef, coef],
        out_specs=[blk, blk],
        out_shape=[jax.ShapeDtypeStruct(zre.shape, F32)] * 2,
        compiler_params=_cparams("parallel"),
        name="ssm_scan",
    )(zre, zim, a_r, a_i)


def _ssm_y_kernel(u_ref, sre_ref, sim_ref, tz_ref, c_ref, d_ref, y_ref):
    v = _chunk_rows(u_ref)
    vb = v.astype(BF16)
    state = jnp.concatenate([sre_ref[...], sim_ref[...]], axis=1).astype(BF16)
    w = 2 * LANES
    n_pairs = SSM_CHUNK // 2
    for tt in range(n_pairs):
        c0, c1 = tt * w, (tt + 1) * w
        acc = _dot(vb[:, :c1], tz_ref[0, (n_pairs - 1 - tt) * w:, :])
        acc = acc + _dot(state, c_ref[0, :, c0:c1]) + d_ref[0, :, c0:c1] * v[:, c0:c1]
        y_ref[:, 2 * tt, :] = acc[:, :LANES]
        y_ref[:, 2 * tt + 1, :] = acc[:, LANES:]


def _ssm_y(u4, sre, sim, mm, c_cat, dvec, tnc):
    bsz, nch = u4.shape[:2]
    noct = c_cat.shape[0]
    sw = c_cat.shape[1] // 2
    per_oct = lambda a: pl.BlockSpec((1,) + a.shape[1:], lambda j, b, r: (j,) + (0,) * (a.ndim - 1))
    state = _state_spec(tnc, sw, noct)
    return pl.pallas_call(
        _ssm_y_kernel,
        grid=(noct, bsz, nch // tnc),
        in_specs=[_u_spec(tnc), state, state, per_oct(mm), per_oct(c_cat), per_oct(dvec)],
        out_specs=_u_spec(tnc),
        out_shape=jax.ShapeDtypeStruct(u4.shape, F32),
        compiler_params=_cparams("parallel", "parallel", "parallel"),
        name="ssm_output",
    )(u4, sre, sim, mm, c_cat, dvec)


def _ssm_operators(a_re, a_im, log_dt, b_re, b_im, c_re, c_im, d_skip):
    g, n = a_re.shape
    p = SSM_GROUP_CH
    t = SSM_CHUNK
    o = SSM_OCT
    noct = g // o
    a = lax.complex(a_re.astype(F32), a_im.astype(F32))
    adt = a * jnp.exp(log_dt.astype(F32))[:, None]
    b_bar = ((jnp.exp(adt) - 1.0) / a)[..., None] * lax.complex(b_re.astype(F32), b_im.astype(F32))
    cm = lax.complex(c_re.astype(F32), c_im.astype(F32))
    pw = jnp.exp(adt[None] * jnp.arange(t + 1, dtype=F32)[:, None, None].astype(jnp.complex64))
    kern = jnp.einsum('gpn,tgn,gnq->tgpq', cm, pw[:t], b_bar, precision=HIGHEST).real
    bz = (pw[:t][::-1][:, :, :, None] * b_bar[None]).transpose(1, 0, 3, 2)
    cz = (cm[None] * pw[1:, :, None, :]).transpose(1, 3, 0, 2)
    oct_ = lambda x: x.reshape((noct, o) + x.shape[1:])

    def placement(reps, w):
        src = jnp.arange(reps * w)
        dst = (src // w)[None, :] * (o * w) + jnp.arange(o)[:, None] * w + (src % w)[None, :]
        return jax.nn.one_hot(dst, reps * o * w, dtype=F32)

    dlag = jnp.einsum('ljapq,apm->ljaqm', kern.reshape(t, noct, o, p, p), placement(1, p)).reshape(t, noct, o * p, o * p)
    dlag = jnp.concatenate([jnp.zeros_like(dlag[:1]), dlag], axis=0)

    def pair_block(dl):
        top = jnp.concatenate([dlag[2 * dl + 1], dlag[2 * dl + 2]], axis=-1)
        bot = jnp.concatenate([dlag[2 * dl], dlag[2 * dl + 1]], axis=-1)
        return jnp.concatenate([top, bot], axis=-2)

    m_oct = jnp.concatenate([pair_block(dl) for dl in reversed(range(t // 2))], axis=1)
    pz_part = lambda x: jnp.einsum('jasqn,anm->jsaqm', oct_(x), placement(1, n)).reshape(noct, t * o * p, o * n)
    c_part = lambda x: jnp.einsum('janx,axm->janm', oct_(x).reshape(noct, o, n, t * p),
                                  placement(t, p)).reshape(noct, o * n, t * o * p)
    pz = jnp.concatenate([pz_part(bz.real), pz_part(bz.imag)], axis=2)
    dvec = jnp.broadcast_to(d_skip.astype(F32).reshape(noct, 1, o, p), (noct, t, o, p)).reshape(noct, 1, t * o * p)
    a_chunk = pw[t].reshape(1, -1)
    c_cat = jnp.concatenate([c_part(cz.real), c_part(-cz.imag)], axis=1)
    return m_oct.astype(BF16), pz.astype(BF16), c_cat.astype(BF16), dvec, a_chunk.real, a_chunk.imag


def _s5(u, ops, bsz, seq):
    m_oct, pz, c_cat, dvec, a_r, a_i = ops
    nch = seq // SSM_CHUNK
    u4 = u.reshape(bsz, nch, SSM_CHUNK, u.shape[1])
    tnc = min(nch, 256)
    zre, zim = _ssm_z(u4, pz, tnc)
    coef = lambda c: jnp.tile(c, (1, bsz))
    sre, sim = _ssm_scan(zre, zim, coef(a_r), coef(a_i), 2048)
    return _ssm_y(u4, sre, sim, m_oct, c_cat, dvec, tnc).reshape(u.shape)


def _cmp_kernel(kvc_ref, pos_ref, w1t_ref, w1b_ref, w2_ref, gk_ref, kk_ref, vvt_ref):
    rows = kvc_ref.shape[0] // NSA_CMP_STRIDE
    top = jnp.zeros((rows, w1t_ref.shape[2]), F32)
    bot = top
    for tau in range(NSA_CMP_STRIDE):
        tok = kvc_ref[pl.ds(tau, rows, stride=NSA_CMP_STRIDE), :]
        top = top + _dot((tok + pos_ref[tau:tau + 1, :]).astype(BF16), w1t_ref[tau])
        bot = bot + _dot((tok + pos_ref[NSA_CMP_STRIDE + tau:NSA_CMP_STRIDE + tau + 1, :]).astype(BF16), w1b_ref[tau])
    hid = top + pltpu.roll(bot, rows - 1, 0)
    out = _dot(jax.nn.gelu(hid).astype(BF16), w2_ref[...])
    k = out[:, :LANES]
    ms = jnp.mean(k * k, axis=-1, keepdims=True)
    kk_ref[0] = (k * lax.rsqrt(ms + EPS) * gk_ref[...]).astype(BF16)
    vvt_ref[0] = out[:, LANES:].T[:HEAD_DIM].astype(BF16)


def _nsa_compress(kvc, pos, w1t, w1b, w2, gk, bsz, seq):
    rows = seq // NSA_CMP_STRIDE
    full = lambda a: pl.BlockSpec(a.shape, lambda b: (0,) * a.ndim)
    out = pl.BlockSpec((1, rows, LANES), lambda b: (b, 0, 0))
    return pl.pallas_call(
        _cmp_kernel,
        grid=(bsz,),
        in_specs=[pl.BlockSpec((seq, LANES), lambda b: (b, 0)), full(pos), full(w1t), full(w1b), full(w2), full(gk)],
        out_specs=[out, pl.BlockSpec((1, HEAD_DIM, rows), lambda b: (b, 0, 0))],
        out_shape=[jax.ShapeDtypeStruct((bsz, rows, LANES), BF16), jax.ShapeDtypeStruct((bsz, HEAD_DIM, rows), BF16)],
        compiler_params=_cparams("parallel"),
        name="nsa_compress",
    )(kvc, pos, w1t, w1b, w2, gk)


def _prep_compress(cmp_pos, cmp_w1, cmp_w2):
    eye = jnp.eye(2, dtype=F32)
    pos = cmp_pos.astype(F32).transpose(1, 0, 2).reshape(NSA_CMP_BLOCK, 2 * HEAD_DIM)
    w1 = cmp_w1.astype(F32).reshape(2, NSA_CMP_BLOCK, HEAD_DIM, -1)
    w1 = jnp.einsum('spdh,sz->psdzh', w1, eye).reshape(NSA_CMP_BLOCK, 2 * HEAD_DIM, -1).astype(BF16)
    w2 = jnp.concatenate([cmp_w2, cmp_w2], axis=-1).astype(F32)
    w2 = jnp.einsum('shd,sz->shzd', w2, eye).reshape(2 * w2.shape[1], 2 * LANES).astype(BF16)
    return pos, w1[:NSA_CMP_STRIDE], w1[NSA_CMP_STRIDE:], w2


def _gate_lanes(gc, gexp):
    return _dot_split(jax.nn.sigmoid(gc), gexp)


def _nsa_cmp_kernel(n_sel_blocks, q_ref, kk_ref, vvt_ref, band_ref, gc_ref, ovlt_ref, gexp_ref,
                    o_ref, nsel_ref):
    bi0 = pl.program_id(0) * CMP_QBLOCKS
    qs = jnp.concatenate([_stack_heads(q_ref[g * QB:(g + 1) * QB, :]) for g in range(CMP_QBLOCKS)], axis=0)
    band = jnp.concatenate([band_ref[(bi0 + g) % CMP_PHASES] for g in range(CMP_QBLOCKS)], axis=1)
    far = band[0:1, :]
    width = 4 * QB

    def attend(nt):
        def run():
            near = band if nt > 1 else band[LANES:]
            bias = jnp.concatenate([jnp.broadcast_to(far, ((nt - 2) * LANES, band.shape[1])), near], axis=0) if nt > 2 else near
            lg = _dot_nt(kk_ref[0, :nt * LANES, :], qs) + bias
            valid = lg > 0.5 * NEG_INF
            m = jnp.max(lg, axis=0, keepdims=True)
            p = jnp.where(valid, jnp.exp(lg - m), 0.0)
            denom = jnp.sum(p, axis=0, keepdims=True)
            p = p * (1.0 / jnp.where(denom > 0.0, denom, 1.0))
            o_t = _dot(vvt_ref[0, :, :nt * LANES], p.astype(BF16))
            p_sum = jnp.concatenate([sum(p[:, g * width + h * QB:g * width + (h + 1) * QB] for h in range(NSA_HEADS))
                                     for g in range(CMP_QBLOCKS)], axis=1)
            hi = p_sum.astype(BF16)
            lo = (p_sum - hi.astype(F32)).astype(BF16)
            ovl_t = ovlt_ref[:, :nt * LANES]
            return o_t, _dot(ovl_t, hi) + _dot(ovl_t, lo)
        return run

    o_t, imp_t = lax.switch(bi0 // CMP_PHASES, [attend(nt) for nt in range(1, kk_ref.shape[1] // LANES + 1)])
    o_cmp = jnp.concatenate([jnp.concatenate([o_t[:, g * width + h * QB:g * width + (h + 1) * QB].T
                                              for h in range(NSA_HEADS)], axis=1) for g in range(CMP_QBLOCKS)], axis=0)
    gates = _gate_lanes(gc_ref[...], gexp_ref[...])
    o_ref[...] = gates[:, :NSA_WIDTH] * o_cmp
    nq = CMP_QBLOCKS * QB
    blk = lax.broadcasted_iota(jnp.int32, (LANES, nq), 0)
    cur = (bi0 * QB + lax.broadcasted_iota(jnp.int32, (LANES, nq), 1)) // NSA_SEL_BLOCK
    forced = (blk == 0) | (blk == cur) | (blk == cur - 1)
    st = jnp.where(forced, -3.4e38, jnp.where(blk > cur, -SEL_FORCE, imp_t))
    st = jnp.where(blk < n_sel_blocks, st, -3e38)
    jidx = blk.astype(F32)
    unsel = jnp.where(forced, 0.0, 1.0)
    for _ in range(min(NSA_TOP_N, n_sel_blocks) - 3):
        mx = jnp.max(st, axis=0, keepdims=True)
        first = jnp.min(jnp.where(st == mx, jidx, 1e9), axis=0, keepdims=True)
        pick = jidx == first
        unsel = jnp.where(pick, 0.0, unsel)
        st = jnp.where(pick, -3.4e38, st)
    nsel_ref[...] = unsel.T.astype(BF16)


def _nsa_cmp(qc, kk, vvt, band, gc, ovlt, gexp, bsz, seq):
    n = qc.shape[0]
    steps = seq // (CMP_QBLOCKS * QB)
    tok = lambda w: pl.BlockSpec((CMP_QBLOCKS * QB, w), lambda i, b: (b * steps + i, 0))
    seqblk = lambda a: pl.BlockSpec((1,) + a.shape[1:], lambda i, b: (b, 0, 0))
    const = lambda a: pl.BlockSpec(a.shape, lambda i, b: (0,) * a.ndim)
    return pl.pallas_call(
        functools.partial(_nsa_cmp_kernel, seq // NSA_SEL_BLOCK),
        grid=(steps, bsz),
        in_specs=[tok(256), seqblk(kk), seqblk(vvt), const(band), tok(LANES), const(ovlt), const(gexp)],
        out_specs=[tok(256), tok(LANES)],
        out_shape=[jax.ShapeDtypeStruct((n, 256), F32), jax.ShapeDtypeStruct((n, LANES), BF16)],
        compiler_params=_cparams("parallel", "parallel"),
        name="nsa_compressed_topn",
    )(qc, kk, vvt, band, gc, ovlt, gexp)


SEL_TK = 512
SEL_LONG_PAIRS = 2
WIN_TILES = NSA_WINDOW // QB + 1
V_ROWS = HEAD_DIM + 16


def _lag_bias(tbl_ref, bi, first_tile, n_tiles):
    last = tbl_ref.shape[0] - 2
    return jnp.concatenate([tbl_ref[jnp.clip(bi - (first_tile + i), -1, last) + 1] for i in range(n_tiles)], axis=0)


def _value_tiles(vt_ref, first_tile, n_tiles):
    return jnp.concatenate([vt_ref[first_tile + i] for i in range(n_tiles)], axis=1)


def _finish_heads(acc):
    o_t = acc[:HEAD_DIM] / acc[HEAD_DIM:HEAD_DIM + 1]
    return jnp.concatenate([o_t[:, h * QB:(h + 1) * QB].T for h in range(NSA_HEADS)], axis=1)


ATTN_QBLOCKS = 2
ATTN_LANES = ATTN_QBLOCKS * NSA_HEADS * QB


def _block_bias(tbl_ref, bi0, first_tile, n_tiles):
    return jnp.concatenate([_lag_bias(tbl_ref, bi0 + g, first_tile, n_tiles) for g in range(ATTN_QBLOCKS)], axis=1)


def _nsa_attn_kernel(q_ref, nsel_ref, ocmp_ref, gc_ref, ks_ref, e_ref, vst_ref, kw_ref, vwt_ref,
                     tsel_ref, twin_ref, gexp_ref, o_ref, qaug_ref, lga_ref, lgb_ref, acc_ref):
    bi0 = pl.program_id(1) * ATTN_QBLOCKS
    blocks = [slice(g * QB, (g + 1) * QB) for g in range(ATTN_QBLOCKS)]
    qs = jnp.concatenate([_stack_heads(q_ref[rows, :]) for rows in blocks], axis=0)
    nsel = jnp.concatenate([nsel_ref[rows, :] for rows in blocks for _ in range(NSA_HEADS)], axis=0)
    qaug_ref[...] = jnp.concatenate([qs, nsel], axis=1)
    per = SEL_TK // QB
    n_tiles = ks_ref.shape[0] // SEL_TK

    def clamp(kt):
        return jnp.minimum(kt, n_tiles - 1)

    def sel_logits(kt, lg_ref):
        rows = pl.ds(pl.multiple_of(clamp(kt) * SEL_TK, SEL_TK), SEL_TK)
        k_aug = jnp.concatenate([ks_ref[rows, :], e_ref[rows, :]], axis=1)
        lg_ref[...] = _dot_nt(k_aug, qaug_ref[...]) + _block_bias(tsel_ref, bi0, kt * per, per)

    def softmax_pv(kt, lg_ref, m):
        ps, ms, alphas = [], [], []
        for c in range(ATTN_LANES // QB):
            cols = slice(c * QB, (c + 1) * QB)
            lg = lg_ref[:, cols]
            m_new = jnp.maximum(m[:, cols], jnp.max(lg, axis=0, keepdims=True))
            ms.append(m_new)
            alphas.append(jnp.exp2(m[:, cols] - m_new))
            ps.append(jnp.exp2(lg - m_new).astype(BF16))
        pv = _dot(_value_tiles(vst_ref, clamp(kt) * per, per), jnp.concatenate(ps, axis=1))
        acc_ref[...] = jnp.concatenate(alphas, axis=1) * acc_ref[...] + pv
        return jnp.concatenate(ms, axis=1)

    def trip(kt, m, n_pairs):
        for i in range(n_pairs):
            sel_logits(kt + 2 * i + 1, lgb_ref)
            m = softmax_pv(kt + 2 * i, lga_ref, m)
            sel_logits(kt + 2 * i + 2, lga_ref)
            m = softmax_pv(kt + 2 * i + 1, lgb_ref, m)
        return m

    acc_ref[...] = jnp.zeros(acc_ref.shape, F32)
    sel_logits(0, lga_ref)
    n_live = (bi0 + ATTN_QBLOCKS - 1) // per + 1
    n_long = n_live // (2 * SEL_LONG_PAIRS)
    m = lax.fori_loop(0, n_long, lambda j, m: trip(2 * SEL_LONG_PAIRS * j, m, SEL_LONG_PAIRS),
                      jnp.full((1, ATTN_LANES), NEG_INF, F32))
    first_short = 2 * SEL_LONG_PAIRS * n_long
    lax.fori_loop(0, (n_live - first_short + 1) // 2, lambda j, m: trip(first_short + 2 * j, m, 1), m)

    n_win = WIN_TILES + ATTN_QBLOCKS - 1
    first = jnp.maximum(bi0 - (WIN_TILES - 1), 0)
    r = pl.multiple_of(first * QB, QB)
    lg = _dot_nt(kw_ref[pl.ds(r, n_win * QB), :], qs) + _block_bias(twin_ref, bi0, first, n_win)
    p = jnp.exp2(lg - jnp.max(lg, axis=0, keepdims=True)).astype(BF16)
    acc_w = _dot(_value_tiles(vwt_ref, first, n_win), p)

    def finish(acc):
        w = NSA_HEADS * QB
        return jnp.concatenate([_finish_heads(acc[:, g * w:(g + 1) * w]) for g in range(ATTN_QBLOCKS)], axis=0)

    gates = _gate_lanes(gc_ref[...], gexp_ref[...])
    o_ref[...] = (ocmp_ref[...] + gates[:, NSA_WIDTH:2 * NSA_WIDTH] * finish(acc_ref[...])
                  + gates[:, 2 * NSA_WIDTH:] * finish(acc_w))


def _value_rows(v, bsz, seq):
    vt = v[:, :HEAD_DIM].reshape(bsz, seq // QB, QB, HEAD_DIM).transpose(0, 1, 3, 2)
    return jnp.concatenate([vt, jnp.ones(vt.shape[:2] + (V_ROWS - HEAD_DIM, QB), vt.dtype)], axis=2)


def _nsa_attn(qc, nsel, ocmp, gc, ks, e_pen, vs, kw, vw, tsel, twin, gexp, bsz, seq):
    n = qc.shape[0]
    nb = seq // QB
    steps = nb // ATTN_QBLOCKS
    tok = lambda w: pl.BlockSpec((ATTN_QBLOCKS * QB, w), lambda b, i: (b * steps + i, 0))
    seqblk = pl.BlockSpec((seq, LANES), lambda b, i: (b, 0))
    valblk = pl.BlockSpec((None, nb, V_ROWS, QB), lambda b, i: (b, 0, 0, 0))
    const = lambda a: pl.BlockSpec(a.shape, lambda b, i: (0,) * a.ndim)
    return pl.pallas_call(
        _nsa_attn_kernel,
        grid=(bsz, steps),
        in_specs=[tok(256), tok(LANES), tok(256), tok(LANES),
                  seqblk, const(e_pen), valblk, seqblk, valblk,
                  const(tsel), const(twin), const(gexp)],
        out_specs=tok(256),
        out_shape=jax.ShapeDtypeStruct((n, 256), F32),
        scratch_shapes=[pltpu.VMEM((ATTN_LANES, 2 * LANES), BF16), pltpu.VMEM((SEL_TK, ATTN_LANES), F32),
                        pltpu.VMEM((SEL_TK, ATTN_LANES), F32), pltpu.VMEM((V_ROWS, ATTN_LANES), F32)],
        compiler_params=_cparams("parallel", "arbitrary"),
        name="nsa_selected_window",
    )(qc, nsel, ocmp, gc, ks, e_pen, _value_rows(vs, bsz, seq), kw, _value_rows(vw, bsz, seq), tsel, twin, gexp)


def _rms(x, gain):
    return x * lax.rsqrt(jnp.mean(x * x, axis=-1, keepdims=True) + EPS) * gain


FFN_CHUNK = 1024


def _out_ffn_kernel(x_ref, oa_ref, y_ref, oc_ref, gluw_ref, glub_ref, gout_ref, wout_ref, g2_ref, wu_ref, wd_ref, o_ref):
    wb = y_ref.shape[1]
    ab = _dot(jax.nn.gelu(y_ref[...]).astype(BF16), gluw_ref[...]) + glub_ref[...]
    ob = ab[:, :wb] * jax.nn.sigmoid(ab[:, wb:])
    a0, a1 = SWA_WIDTH, SWA_WIDTH + wb
    mixed = jnp.concatenate([_rms(oa_ref[...], gout_ref[:, :a0]).astype(BF16),
                             _rms(ob, gout_ref[:, a0:a1]).astype(BF16),
                             _rms(oc_ref[...], gout_ref[:, a1:]).astype(BF16)], axis=1)
    x = x_ref[...] + _dot(mixed, wout_ref[...])
    h = _rms(x, g2_ref[...]).astype(BF16)
    acc = x
    for c in range(wu_ref.shape[1] // FFN_CHUNK):
        c0, c1 = c * FFN_CHUNK, (c + 1) * FFN_CHUNK
        hid = jnp.maximum(_dot(h, wu_ref[:, c0:c1]), 0.0)
        acc = acc + _dot((hid * hid).astype(BF16), wd_ref[c0:c1, :])
    o_ref[...] = acc


def _out_ffn(x2, oa, y, oc, gluw, glub, gout, wout, g2, wu, wd, tm):
    n, d = x2.shape
    tok = lambda w: pl.BlockSpec((tm, w), lambda i: (i, 0))
    full = lambda a: pl.BlockSpec(a.shape, lambda i: (0,) * a.ndim, pipeline_mode=pl.Buffered(1))
    return pl.pallas_call(
        _out_ffn_kernel,
        grid=(n // tm,),
        in_specs=[tok(d), tok(oa.shape[1]), tok(y.shape[1]), tok(oc.shape[1]),
                  full(gluw), full(glub), full(gout), full(wout), full(g2), full(wu), full(wd)],
        out_specs=tok(d),
        out_shape=jax.ShapeDtypeStruct((n, d), F32),
        compiler_params=_cparams("parallel"),
        name="out_proj_ffn",
    )(x2, oa, y, oc, gluw, glub, gout, wout, g2, wu, wd)


def _rel_bucket(dist):
    n = jnp.maximum(dist, 0)
    nf = jnp.maximum(n, 1).astype(F32)
    large = BUCKET_EXACT + (jnp.log(nf / BUCKET_EXACT) / math.log(BUCKET_MAX_DIST / BUCKET_EXACT)
                            * (NUM_BUCKETS - BUCKET_EXACT)).astype(jnp.int32)
    return jnp.where(n < BUCKET_EXACT, n, jnp.minimum(large, NUM_BUCKETS - 1))


def _bias_rows(tbl, dist, valid):
    onehot = jax.nn.one_hot(_rel_bucket(dist), NUM_BUCKETS, dtype=F32)
    b = jnp.einsum('...qkc,ch->...hqk', onehot, tbl.astype(F32), precision=HIGHEST)
    b = jnp.where(valid[..., None, :, :], b, NEG_INF)
    return b.reshape(b.shape[:-3] + (b.shape[-3] * b.shape[-2], b.shape[-1]))


def _bias_tables(rel_bias):
    swa_order = jnp.array([0, 2, 1, 3])
    tbl_a = rel_bias[:, :SWA_HEADS][:, swa_order]
    tbl_c = rel_bias[:, SWA_HEADS:]
    i = jnp.arange(QB)[:, None]
    d_swa = i - jnp.arange(2 * QB)[None, :] + QB
    bias_swa = _bias_rows(tbl_a, d_swa, (d_swa >= 0) & (d_swa < SWA_WINDOW))
    r = jnp.arange(CMP_PHASES)[:, None, None]
    c = jnp.arange(2 * LANES)[None, None, :]
    d_cmp = r * QB + i[None] - (c - LANES) * NSA_CMP_STRIDE - (NSA_CMP_BLOCK - 1)
    band_cmp = jnp.swapaxes(_bias_rows(tbl_c, d_cmp, d_cmp >= 0), 1, 2)
    j = jnp.arange(QB)[None, :]
    n_far = -(-(BUCKET_MAX_DIST + QB) // QB)
    d_sel = jnp.arange(n_far)[:, None, None] * QB + (i - j)[None]
    tsel = _bias_rows(tbl_c, d_sel, d_sel >= 0)
    d_win = jnp.arange(WIN_TILES)[:, None, None] * QB + (i - j)[None]
    twin = _bias_rows(tbl_c, d_win, (d_win >= 0) & (d_win < NSA_WINDOW))
    future = jnp.full((1,) + tsel.shape[1:], NEG_INF, F32)
    lag_table = lambda t: jnp.swapaxes(jnp.concatenate([future, t * LOG2E]), 1, 2)
    return bias_swa, band_cmp, lag_table(tsel), lag_table(jnp.concatenate([twin, future / LOG2E]))


def _nsa_constants(seq):
    m = seq // NSA_CMP_STRIDE
    cs = jnp.arange(m)[:, None] * NSA_CMP_STRIDE
    ss = jnp.arange(LANES)[None, :] * NSA_SEL_BLOCK
    ovl = ((cs < ss + NSA_SEL_BLOCK) & (cs + NSA_CMP_BLOCK > ss) & (cs < seq - NSA_CMP_STRIDE)).astype(BF16)
    key_blk = jnp.arange(seq)[:, None] // NSA_SEL_BLOCK
    e_pen = jnp.where(key_blk == jnp.arange(LANES)[None, :], SEL_PENALTY, 0.0).astype(BF16)
    col = jnp.arange(LANES)[:, None]
    lane = jnp.arange(NSA_BRANCHES * NSA_WIDTH)[None, :]
    br, hd = lane // NSA_WIDTH, (lane % NSA_WIDTH) // HEAD_DIM
    gexp = (col == hd * NSA_BRANCHES + br).astype(BF16)
    return ovl.T, e_pen, gexp


def _seg_matrix():
    r = jnp.arange(2 * LANES)
    return jnp.where((r[:, None] // HEAD_DIM) == (r[None, :] // HEAD_DIM), 1.0 / HEAD_DIM, 0.0).astype(BF16)


def _prep_w_in(w):
    d = w.shape[0]
    hd = HEAD_DIM
    ssm_w = d - SWA_WIDTH - NSA_WIDTH
    o_ka = SWA_WIDTH
    o_va = o_ka + SWA_KV_WIDTH
    o_u = o_va + SWA_KV_WIDTH
    o_qc = o_u + ssm_w
    o_kv = o_qc + NSA_WIDTH
    o_gc = o_kv + 6 * hd
    head = lambda off, h: w[:, off + h * hd: off + (h + 1) * hd]
    gates = w[:, o_gc:]
    cols = [head(0, 0), head(0, 2), head(0, 1), head(0, 3), w[:, o_ka:o_gc],
            gates, jnp.zeros((d, LANES - gates.shape[1]), w.dtype)]
    return jnp.concatenate(cols, axis=1).astype(BF16)


def _prep_qk_gains(qk_g):
    g = qk_g.astype(F32)
    t4 = lambda v: jnp.tile(v, 4)
    return jnp.stack([t4(g[0]), t4(g[1]), t4(g[2]), t4(g[4]), t4(g[5]),
                      t4(g[3]), t4(g[3]), t4(g[3])])


def kernel(x, norm1_g, w_in, qk_g, sinks, rel_bias, ssm_a_re, ssm_a_im, ssm_log_dt, ssm_b_re, ssm_b_im, ssm_c_re, ssm_c_im, ssm_d, glu_w, glu_b, cmp_pos, cmp_w1, cmp_w2, out_norm_g, w_out, norm2_g, w_up, w_down):
    bsz, seq, d = x.shape
    depth = w_in.shape[0]
    n = bsz * seq
    assert seq % SWA_TQ == 0 and seq // NSA_SEL_BLOCK <= LANES and d == 1024
    assert (seq // NSA_CMP_STRIDE) % LANES == 0 and seq % SEL_TK == 0
    assert seq >= (WIN_TILES + ATTN_QBLOCKS - 1) * QB and (seq // QB) % ATTN_QBLOCKS == 0
    tm = 512
    row = lambda v: v.astype(F32).reshape(1, -1)

    bias_swa, band_cmp, tsel, twin = _bias_tables(rel_bias)
    ovlt, e_pen, gexp = _nsa_constants(seq)
    seg = _seg_matrix()
    swa_order = jnp.array([0, 2, 1, 3])
    swa_cols = (swa_order[:, None] * HEAD_DIM + jnp.arange(HEAD_DIM)[None, :]).reshape(-1)

    qkg_all = jax.vmap(_prep_qk_gains)(qk_g)
    w_in_all = jax.vmap(_prep_w_in)(w_in)
    ssm_all = jax.vmap(_ssm_operators)(ssm_a_re, ssm_a_im, ssm_log_dt, ssm_b_re, ssm_b_im, ssm_c_re, ssm_c_im, ssm_d)
    cmp_all = jax.vmap(_prep_compress)(cmp_pos, cmp_w1, cmp_w2)
    sink_all = jnp.repeat(sinks.astype(F32)[:, swa_order], QB, axis=1)[..., None]
    gout_all = jnp.concatenate([out_norm_g[:, :SWA_WIDTH][:, swa_cols], out_norm_g[:, SWA_WIDTH:]], axis=1).astype(F32)
    w_out_all = jnp.concatenate([w_out[:, :SWA_WIDTH][:, swa_cols], w_out[:, SWA_WIDTH:]], axis=1).astype(BF16)
    glu_w_all, w_up_all, w_down_all = glu_w.astype(BF16), w_up.astype(BF16), w_down.astype(BF16)

    x2 = x.reshape(n, d)
    for l in range(depth):
        qkg = qkg_all[l]
        (qa, ka, va, u, qc, kvc, ks, vs, kw, vw, gc) = _in_proj(x2, row(norm1_g[l]), w_in_all[l], qkg, seg, tm)

        o_a = _swa(qa, ka, va, bias_swa, sink_all[l], bsz, seq)

        y = _s5(u, [op[l] for op in ssm_all], bsz, seq)

        pos, w1t, w1b, w2 = [c[l] for c in cmp_all]
        kk, vvt = _nsa_compress(kvc, pos, w1t, w1b, w2, qkg[5:6, :LANES], bsz, seq)
        o_cmp, nsel = _nsa_cmp(qc, kk, vvt, band_cmp, gc, ovlt, gexp, bsz, seq)
        o_c = _nsa_attn(qc, nsel, o_cmp, gc, ks, e_pen, vs, kw, vw, tsel, twin, gexp, bsz, seq)

        x2 = _out_ffn(x2, o_a, y, o_c, glu_w_all[l], row(glu_b[l]), gout_all[l:l + 1], w_out_all[l],
                      row(norm2_g[l]), w_up_all[l], w_down_all[l], tm)
    return x2.reshape(bsz, seq, d)
```

```python
import functools
import math

import jax
import jax.numpy as jnp
from jax import lax
from jax.experimental import pallas as pl
from jax.experimental.pallas import tpu as pltpu

F32 = jnp.float32
BF16 = jnp.bfloat16
HIGHEST = lax.Precision.HIGHEST

HEAD_DIM = 64
SWA_HEADS = 4
SWA_KV_HEADS = 2
SWA_WINDOW = 128
NSA_HEADS = 4
NSA_CMP_BLOCK = 32
NSA_CMP_STRIDE = 16
NSA_SEL_BLOCK = 64
NSA_TOP_N = 16
NSA_WINDOW = 512
NSA_BRANCHES = 3
SSM_GROUP_CH = 16
SSM_STATE = 64
NUM_BUCKETS = 32
BUCKET_EXACT = NUM_BUCKETS // 2
BUCKET_MAX_DIST = 1024
EPS = 1e-6
NEG_INF = -1e30
SEL_FORCE = 1e4
SEL_PENALTY = NEG_INF
LOG2E = math.log2(math.e)

LANES = 128
QB = 128
SSM_CHUNK = 16
SSM_OCT = LANES // SSM_GROUP_CH
CMP_PHASES = LANES * NSA_CMP_STRIDE // QB
CMP_QBLOCKS = 2
VMEM_LIMIT = 56 * 1024 * 1024

SWA_WIDTH = SWA_HEADS * HEAD_DIM
SWA_KV_WIDTH = SWA_KV_HEADS * HEAD_DIM
NSA_WIDTH = NSA_HEADS * HEAD_DIM


def _cparams(*sem):
    return pltpu.CompilerParams(dimension_semantics=sem, vmem_limit_bytes=VMEM_LIMIT)


def _dot(a, b):
    return jnp.dot(a, b, preferred_element_type=F32)


def _dot_nt(a, b):
    return lax.dot_general(a, b, (((1,), (1,)), ((), ())), preferred_element_type=F32)


def _dot_split(a, b):
    hi = a.astype(BF16)
    lo = (a - hi.astype(F32)).astype(BF16)
    return _dot(hi, b) + _dot(lo, b)


def _with_ones(v):
    return jnp.concatenate([v, jnp.ones_like(v)], axis=1)


def _lane_lo(rows):
    return lax.broadcasted_iota(jnp.int32, (rows, LANES), 1) < HEAD_DIM


def _stack_heads(q):
    lo = _lane_lo(q.shape[0])
    zero = jnp.zeros_like(q[:, :LANES])
    g0 = q[:, :LANES]
    g1 = q[:, LANES:]
    return jnp.concatenate([jnp.where(lo, g0, zero), jnp.where(lo, zero, g0),
                            jnp.where(lo, g1, zero), jnp.where(lo, zero, g1)], axis=0)


def _unstack_heads(o):
    n = o.shape[0] // 4
    lo = _lane_lo(n)
    return jnp.concatenate([jnp.where(lo, o[0:n], o[n:2 * n]),
                            jnp.where(lo, o[2 * n:3 * n], o[3 * n:4 * n])], axis=1)


def _seg_rms(p, seg, gain):
    ms = _dot_split(p * p, seg)
    return p * lax.rsqrt(ms + EPS) * gain


def _in_proj_kernel(x_ref, g_ref, w_ref, qkg_ref, seg_ref,
                    qa_ref, ka_ref, va_ref, u_ref, qc_ref, kvc_ref,
                    ks_ref, vs_ref, kw_ref, vw_ref, gc_ref):
    x = x_ref[...]
    ms = jnp.mean(x * x, axis=-1, keepdims=True)
    h = (x * lax.rsqrt(ms + EPS) * g_ref[...]).astype(BF16)
    seg = seg_ref[...]
    scale = HEAD_DIM ** -0.5

    full = _dot(h, w_ref[...])

    def proj(a, b):
        return full[:, a:b]

    lo = _lane_lo(x.shape[0])

    def dup_halves(kv):
        swapped = pltpu.roll(kv, HEAD_DIM, 1)
        return jnp.where(lo, kv, swapped), jnp.where(lo, swapped, kv)

    def dup_rms(p, gain):
        m = jnp.mean(p * p, axis=-1, keepdims=True)
        return p * lax.rsqrt(m + EPS) * (gain * LOG2E)

    qa_ref[...] = (_seg_rms(proj(0, 256), seg, qkg_ref[0:1, :]) * scale).astype(BF16)
    kva = proj(256, 512)
    ka_ref[...] = _seg_rms(kva, seg, qkg_ref[1:2, :])[:, :LANES].astype(BF16)
    va_ref[...] = kva[:, LANES:].astype(BF16)
    u_ref[...] = proj(512, 1024)
    qc_ref[...] = (_seg_rms(proj(1024, 1280), seg, qkg_ref[2:3, :]) * scale).astype(BF16)
    cmp_sel = proj(1280, 1536)
    kvc_ref[...] = cmp_sel[:, :LANES]
    ks, vs = dup_halves(cmp_sel[:, LANES:])
    ks_ref[...] = dup_rms(ks, qkg_ref[3:4, :LANES]).astype(BF16)
    vs_ref[...] = vs.astype(BF16)
    win_gate = proj(1536, 1792)
    kw, vw = dup_halves(win_gate[:, :LANES])
    kw_ref[...] = dup_rms(kw, qkg_ref[4:5, :LANES]).astype(BF16)
    vw_ref[...] = vw.astype(BF16)
    gc_ref[...] = win_gate[:, LANES:]


def _in_proj(x2, g1, w, qkg, seg, tm):
    n, d = x2.shape
    widths = [(256, BF16), (128, BF16), (128, BF16), (512, F32), (256, BF16), (128, F32),
              (128, BF16), (128, BF16), (128, BF16), (128, BF16), (128, F32)]
    full = lambda a: pl.BlockSpec(a.shape, lambda i: (0,) * a.ndim)
    return pl.pallas_call(
        _in_proj_kernel,
        grid=(n // tm,),
        in_specs=[pl.BlockSpec((tm, d), lambda i: (i, 0)), full(g1), full(w), full(qkg), full(seg)],
        out_specs=[pl.BlockSpec((tm, wd), lambda i: (i, 0)) for wd, _ in widths],
        out_shape=[jax.ShapeDtypeStruct((n, wd), dt) for wd, dt in widths],
        compiler_params=_cparams("parallel"),
        name="in_proj",
    )(x2, g1, w, qkg, seg)


SWA_TQ = 512


def _swa_kernel(q_ref, kc_ref, kp_ref, vc_ref, vp_ref, bias_ref, sink_ref, o_ref):
    first = pl.program_id(1) == 0
    prev_cols = lax.broadcasted_iota(jnp.int32, (1, 2 * QB), 1) < QB
    pen = jnp.where(first & prev_cols, NEG_INF, 0.0)
    sink = sink_ref[...]
    bias = bias_ref[...]
    for s in range(SWA_TQ // QB):
        r0, r1 = s * QB, (s + 1) * QB
        qs = _stack_heads(q_ref[r0:r1, :])
        k_cur = kc_ref[r0:r1, :]
        v_cur = vc_ref[r0:r1, :]
        if s == 0:
            k_prev, v_prev = kp_ref[...], vp_ref[...]
        else:
            k_prev, v_prev = kc_ref[r0 - QB:r0, :], vc_ref[r0 - QB:r0, :]
        lg = jnp.concatenate([_dot_nt(qs, k_prev), _dot_nt(qs, k_cur)], axis=1) + bias
        if s == 0:
            lg = lg + pen
        m = jnp.maximum(jnp.max(lg, axis=-1, keepdims=True), sink)
        p = jnp.exp(lg - m)
        denom = jnp.sum(p, axis=-1, keepdims=True) + jnp.exp(sink - m)
        pv = _dot(p[:, :QB].astype(BF16), v_prev) + _dot(p[:, QB:].astype(BF16), v_cur)
        o_ref[r0:r1, :] = _unstack_heads(pv / denom)


def _swa(qa, ka, va, bias, sink_rows, bsz, seq):
    n = qa.shape[0]
    nq = seq // SWA_TQ
    per = SWA_TQ // QB
    cur = lambda b, i: (b * nq + i, 0)
    prev = lambda b, i: (jnp.maximum((b * nq + i) * per - 1, 0), 0)
    const = lambda b, i: (0, 0)
    return pl.pallas_call(
        _swa_kernel,
        grid=(bsz, nq),
        in_specs=[pl.BlockSpec((SWA_TQ, 256), cur),
                  pl.BlockSpec((SWA_TQ, LANES), cur), pl.BlockSpec((QB, LANES), prev),
                  pl.BlockSpec((SWA_TQ, LANES), cur), pl.BlockSpec((QB, LANES), prev),
                  pl.BlockSpec(bias.shape, const), pl.BlockSpec(sink_rows.shape, const)],
        out_specs=pl.BlockSpec((SWA_TQ, 256), cur),
        out_shape=jax.ShapeDtypeStruct((n, 256), F32),
        compiler_params=_cparams("parallel", "parallel"),
        name="swa",
    )(qa, ka, ka, va, va, bias, sink_rows)


def _chunk_rows(u_ref):
    return jnp.concatenate([u_ref[:, t, :] for t in range(SSM_CHUNK)], axis=1)


def _ssm_z_kernel(u_ref, pz_ref, zre_ref, zim_ref):
    z = _dot(_chunk_rows(u_ref).astype(BF16), pz_ref[0])
    half = z.shape[1] // 2
    zre_ref[...] = z[:, :half]
    zim_ref[...] = z[:, half:]


def _u_spec(tnc):
    return pl.BlockSpec((None, tnc, SSM_CHUNK, LANES), lambda j, b, r: (b, r, 0, j))


def _state_spec(tnc, sw, noct):
    return pl.BlockSpec((tnc, sw), lambda j, b, r: (r, b * noct + j))


def _ssm_z(u4, pz, tnc):
    bsz, nch = u4.shape[:2]
    noct = pz.shape[0]
    sw = pz.shape[2] // 2
    state = _state_spec(tnc, sw, noct)
    return pl.pallas_call(
        _ssm_z_kernel,
        grid=(noct, bsz, nch // tnc),
        in_specs=[_u_spec(tnc), pl.BlockSpec((1,) + pz.shape[1:], lambda j, b, r: (j, 0, 0))],
        out_specs=[state, state],
        out_shape=[jax.ShapeDtypeStruct((nch, bsz * noct * sw), F32)] * 2,
        compiler_params=_cparams("parallel", "parallel", "parallel"),
        name="ssm_chunk_state",
    )(u4, pz)


def _ssm_scan_kernel(zre_ref, zim_ref, ar_ref, ai_ref, sre_ref, sim_ref):
    a_r = ar_ref[...]
    a_i = ai_ref[...]

    def body(c, carry):
        s_r, s_i = carry
        row = pl.ds(c, 1)
        sre_ref[row, :] = s_r
        sim_ref[row, :] = s_i
        return (a_r * s_r - a_i * s_i + zre_ref[row, :], a_r * s_i + a_i * s_r + zim_ref[row, :])

    zero = jnp.zeros(a_r.shape, F32)
    lax.fori_loop(0, zre_ref.shape[0], body, (zero, zero))


def _ssm_scan(zre, zim, a_r, a_i, tl):
    nchunk, width = zre.shape
    blk = pl.BlockSpec((nchunk, tl), lambda j: (0, j))
    coef = pl.BlockSpec((1, tl), lambda j: (0, j))
    return pl.pallas_call(
        _ssm_scan_kernel,
        grid=(width // tl,),
        in_specs=[blk, blk, coef, coef],
        out_specs=[blk, blk],
        out_shape=[jax.ShapeDtypeStruct(zre.shape, F32)] * 2,
        compiler_params=_cparams("parallel"),
        name="ssm_scan",
    )(zre, zim, a_r, a_i)


def _ssm_y_kernel(u_ref, sre_ref, sim_ref, tz_ref, c_ref, d_ref, y_ref):
    v = _chunk_rows(u_ref)
    vb = v.astype(BF16)
    state = jnp.concatenate([sre_ref[...], sim_ref[...]], axis=1).astype(BF16)
    w = 2 * LANES
    n_pairs = SSM_CHUNK // 2
    for tt in range(n_pairs):
        c0, c1 = tt * w, (tt + 1) * w
        acc = _dot(vb[:, :c1], tz_ref[0, (n_pairs - 1 - tt) * w:, :])
        acc = acc + _dot(state, c_ref[0, :, c0:c1]) + d_ref[0, :, c0:c1] * v[:, c0:c1]
        y_ref[:, 2 * tt, :] = acc[:, :LANES]
        y_ref[:, 2 * tt + 1, :] = acc[:, LANES:]


def _ssm_y(u4, sre, sim, mm, c_cat, dvec, tnc):
    bsz, nch = u4.shape[:2]
    noct = c_cat.shape[0]
    sw = c_cat.shape[1] // 2
    per_oct = lambda a: pl.BlockSpec((1,) + a.shape[1:], lambda j, b, r: (j,) + (0,) * (a.ndim - 1))
    state = _state_spec(tnc, sw, noct)
    return pl.pallas_call(
        _ssm_y_kernel,
        grid=(noct, bsz, nch // tnc),
        in_specs=[_u_spec(tnc), state, state, per_oct(mm), per_oct(c_cat), per_oct(dvec)],
        out_specs=_u_spec(tnc),
        out_shape=jax.ShapeDtypeStruct(u4.shape, F32),
        compiler_params=_cparams("parallel", "parallel", "parallel"),
        name="ssm_output",
    )(u4, sre, sim, mm, c_cat, dvec)


def _ssm_operators(a_re, a_im, log_dt, b_re, b_im, c_re, c_im, d_skip):
    g, n = a_re.shape
    p = SSM_GROUP_CH
    t = SSM_CHUNK
    o = SSM_OCT
    noct = g // o
    a = lax.complex(a_re.astype(F32), a_im.astype(F32))
    adt = a * jnp.exp(log_dt.astype(F32))[:, None]
    b_bar = ((jnp.exp(adt) - 1.0) / a)[..., None] * lax.complex(b_re.astype(F32), b_im.astype(F32))
    cm = lax.complex(c_re.astype(F32), c_im.astype(F32))
    pw = jnp.exp(adt[None] * jnp.arange(t + 1, dtype=F32)[:, None, None].astype(jnp.complex64))
    kern = jnp.einsum('gpn,tgn,gnq->tgpq', cm, pw[:t], b_bar, precision=HIGHEST).real
    bz = (pw[:t][::-1][:, :, :, None] * b_bar[None]).transpose(1, 0, 3, 2)
    cz = (cm[None] * pw[1:, :, None, :]).transpose(1, 3, 0, 2)
    oct_ = lambda x: x.reshape((noct, o) + x.shape[1:])

    def placement(reps, w):
        src = jnp.arange(reps * w)
        dst = (src // w)[None, :] * (o * w) + jnp.arange(o)[:, None] * w + (src % w)[None, :]
        return jax.nn.one_hot(dst, reps * o * w, dtype=F32)

    dlag = jnp.einsum('ljapq,apm->ljaqm', kern.reshape(t, noct, o, p, p), placement(1, p)).reshape(t, noct, o * p, o * p)
    dlag = jnp.concatenate([jnp.zeros_like(dlag[:1]), dlag], axis=0)

    def pair_block(dl):
        top = jnp.concatenate([dlag[2 * dl + 1], dlag[2 * dl + 2]], axis=-1)
        bot = jnp.concatenate([dlag[2 * dl], dlag[2 * dl + 1]], axis=-1)
        return jnp.concatenate([top, bot], axis=-2)

    m_oct = jnp.concatenate([pair_block(dl) for dl in reversed(range(t // 2))], axis=1)
    pz_part = lambda x: jnp.einsum('jasqn,anm->jsaqm', oct_(x), placement(1, n)).reshape(noct, t * o * p, o * n)
    c_part = lambda x: jnp.einsum('janx,axm->janm', oct_(x).reshape(noct, o, n, t * p),
                                  placement(t, p)).reshape(noct, o * n, t * o * p)
    pz = jnp.concatenate([pz_part(bz.real), pz_part(bz.imag)], axis=2)
    dvec = jnp.broadcast_to(d_skip.astype(F32).reshape(noct, 1, o, p), (noct, t, o, p)).reshape(noct, 1, t * o * p)
    a_chunk = pw[t].reshape(1, -1)
    c_cat = jnp.concatenate([c_part(cz.real), c_part(-cz.imag)], axis=1)
    return m_oct.astype(BF16), pz.astype(BF16), c_cat.astype(BF16), dvec, a_chunk.real, a_chunk.imag


def _s5(u, ops, bsz, seq):
    m_oct, pz, c_cat, dvec, a_r, a_i = ops
    nch = seq // SSM_CHUNK
    u4 = u.reshape(bsz, nch, SSM_CHUNK, u.shape[1])
    tnc = min(nch, 256)
    zre, zim = _ssm_z(u4, pz, tnc)
    coef = lambda c: jnp.tile(c, (1, bsz))
    sre, sim = _ssm_scan(zre, zim, coef(a_r), coef(a_i), 2048)
    return _ssm_y(u4, sre, sim, m_oct, c_cat, dvec, tnc).reshape(u.shape)


def _cmp_kernel(kvc_ref, pos_ref, w1t_ref, w1b_ref, w2_ref, gk_ref, kk_ref, vvt_ref):
    rows = kvc_ref.shape[0] // NSA_CMP_STRIDE
    top = jnp.zeros((rows, w1t_ref.shape[2]), F32)
    bot = top
    for tau in range(NSA_CMP_STRIDE):
        tok = kvc_ref[pl.ds(tau, rows, stride=NSA_CMP_STRIDE), :]
        top = top + _dot((tok + pos_ref[tau:tau + 1, :]).astype(BF16), w1t_ref[tau])
        bot = bot + _dot((tok + pos_ref[NSA_CMP_STRIDE + tau:NSA_CMP_STRIDE + tau + 1, :]).astype(BF16), w1b_ref[tau])
    hid = top + pltpu.roll(bot, rows - 1, 0)
    out = _dot(jax.nn.gelu(hid).astype(BF16), w2_ref[...])
    k = out[:, :LANES]
    ms = jnp.mean(k * k, axis=-1, keepdims=True)
    kk_ref[0] = (k * lax.rsqrt(ms + EPS) * gk_ref[...]).astype(BF16)
    vvt_ref[0] = out[:, LANES:].T[:HEAD_DIM].astype(BF16)


def _nsa_compress(kvc, pos, w1t, w1b, w2, gk, bsz, seq):
    rows = seq // NSA_CMP_STRIDE
    full = lambda a: pl.BlockSpec(a.shape, lambda b: (0,) * a.ndim)
    out = pl.BlockSpec((1, rows, LANES), lambda b: (b, 0, 0))
    return pl.pallas_call(
        _cmp_kernel,
        grid=(bsz,),
        in_specs=[pl.BlockSpec((seq, LANES), lambda b: (b, 0)), full(pos), full(w1t), full(w1b), full(w2), full(gk)],
        out_specs=[out, pl.BlockSpec((1, HEAD_DIM, rows), lambda b: (b, 0, 0))],
        out_shape=[jax.ShapeDtypeStruct((bsz, rows, LANES), BF16), jax.ShapeDtypeStruct((bsz, HEAD_DIM, rows), BF16)],
        compiler_params=_cparams("parallel"),
        name="nsa_compress",
    )(kvc, pos, w1t, w1b, w2, gk)


def _prep_compress(cmp_pos, cmp_w1, cmp_w2):
    eye = jnp.eye(2, dtype=F32)
    pos = cmp_pos.astype(F32).transpose(1, 0, 2).reshape(NSA_CMP_BLOCK, 2 * HEAD_DIM)
    w1 = cmp_w1.astype(F32).reshape(2, NSA_CMP_BLOCK, HEAD_DIM, -1)
    w1 = jnp.einsum('spdh,sz->psdzh', w1, eye).reshape(NSA_CMP_BLOCK, 2 * HEAD_DIM, -1).astype(BF16)
    w2 = jnp.concatenate([cmp_w2, cmp_w2], axis=-1).astype(F32)
    w2 = jnp.einsum('shd,sz->shzd', w2, eye).reshape(2 * w2.shape[1], 2 * LANES).astype(BF16)
    return pos, w1[:NSA_CMP_STRIDE], w1[NSA_CMP_STRIDE:], w2


def _gate_lanes(gc, gexp):
    return _dot_split(jax.nn.sigmoid(gc), gexp)


def _nsa_cmp_kernel(n_sel_blocks, q_ref, kk_ref, vvt_ref, band_ref, gc_ref, ovlt_ref, gexp_ref,
                    o_ref, nsel_ref):
    bi0 = pl.program_id(0) * CMP_QBLOCKS
    qs = jnp.concatenate([_stack_heads(q_ref[g * QB:(g + 1) * QB, :]) for g in range(CMP_QBLOCKS)], axis=0)
    band = jnp.concatenate([band_ref[(bi0 + g) % CMP_PHASES] for g in range(CMP_QBLOCKS)], axis=1)
    far = band[0:1, :]
    width = 4 * QB

    def attend(nt):
        def run():
            near = band if nt > 1 else band[LANES:]
            bias = jnp.concatenate([jnp.broadcast_to(far, ((nt - 2) * LANES, band.shape[1])), near], axis=0) if nt > 2 else near
            lg = _dot_nt(kk_ref[0, :nt * LANES, :], qs) + bias
            valid = lg > 0.5 * NEG_INF
            m = jnp.max(lg, axis=0, keepdims=True)
            p = jnp.where(valid, jnp.exp(lg - m), 0.0)
            denom = jnp.sum(p, axis=0, keepdims=True)
            p = p * (1.0 / jnp.where(denom > 0.0, denom, 1.0))
            o_t = _dot(vvt_ref[0, :, :nt * LANES], p.astype(BF16))
            p_sum = jnp.concatenate([sum(p[:, g * width + h * QB:g * width + (h + 1) * QB] for h in range(NSA_HEADS))
                                     for g in range(CMP_QBLOCKS)], axis=1)
            hi = p_sum.astype(BF16)
            lo = (p_sum - hi.astype(F32)).astype(BF16)
            ovl_t = ovlt_ref[:, :nt * LANES]
            return o_t, _dot(ovl_t, hi) + _dot(ovl_t, lo)
        return run

    o_t, imp_t = lax.switch(bi0 // CMP_PHASES, [attend(nt) for nt in range(1, kk_ref.shape[1] // LANES + 1)])
    o_cmp = jnp.concatenate([jnp.concatenate([o_t[:, g * width + h * QB:g * width + (h + 1) * QB].T
                                              for h in range(NSA_HEADS)], axis=1) for g in range(CMP_QBLOCKS)], axis=0)
    gates = _gate_lanes(gc_ref[...], gexp_ref[...])
    o_ref[...] = gates[:, :NSA_WIDTH] * o_cmp
    nq = CMP_QBLOCKS * QB
    blk = lax.broadcasted_iota(jnp.int32, (LANES, nq), 0)
    cur = (bi0 * QB + lax.broadcasted_iota(jnp.int32, (LANES, nq), 1)) // NSA_SEL_BLOCK
    forced = (blk == 0) | (blk == cur) | (blk == cur - 1)
    st = jnp.where(forced, -3.4e38, jnp.where(blk > cur, -SEL_FORCE, imp_t))
    st = jnp.where(blk < n_sel_blocks, st, -3e38)
    jidx = blk.astype(F32)
    unsel = jnp.where(forced, 0.0, 1.0)
    for _ in range(min(NSA_TOP_N, n_sel_blocks) - 3):
        mx = jnp.max(st, axis=0, keepdims=True)
        first = jnp.min(jnp.where(st == mx, jidx, 1e9), axis=0, keepdims=True)
        pick = jidx == first
        unsel = jnp.where(pick, 0.0, unsel)
        st = jnp.where(pick, -3.4e38, st)
    nsel_ref[...] = unsel.T.astype(BF16)


def _nsa_cmp(qc, kk, vvt, band, gc, ovlt, gexp, bsz, seq):
    n = qc.shape[0]
    steps = seq // (CMP_QBLOCKS * QB)
    tok = lambda w: pl.BlockSpec((CMP_QBLOCKS * QB, w), lambda i, b: (b * steps + i, 0))
    seqblk = lambda a: pl.BlockSpec((1,) + a.shape[1:], lambda i, b: (b, 0, 0))
    const = lambda a: pl.BlockSpec(a.shape, lambda i, b: (0,) * a.ndim)
    return pl.pallas_call(
        functools.partial(_nsa_cmp_kernel, seq // NSA_SEL_BLOCK),
        grid=(steps, bsz),
        in_specs=[tok(256), seqblk(kk), seqblk(vvt), const(band), tok(LANES), const(ovlt), const(gexp)],
        out_specs=[tok(256), tok(LANES)],
        out_shape=[jax.ShapeDtypeStruct((n, 256), F32), jax.ShapeDtypeStruct((n, LANES), BF16)],
        compiler_params=_cparams("parallel", "parallel"),
        name="nsa_compressed_topn",
    )(qc, kk, vvt, band, gc, ovlt, gexp)


SEL_TK = 512
SEL_LONG_PAIRS = 2
WIN_TILES = NSA_WINDOW // QB + 1
V_ROWS = HEAD_DIM + 16


def _lag_bias(tbl_ref, bi, first_tile, n_tiles):
    last = tbl_ref.shape[0] - 2
    return jnp.concatenate([tbl_ref[jnp.clip(bi - (first_tile + i), -1, last) + 1] for i in range(n_tiles)], axis=0)


def _value_tiles(vt_ref, first_tile, n_tiles):
    return jnp.concatenate([vt_ref[first_tile + i] for i in range(n_tiles)], axis=1)


def _finish_heads(acc):
    o_t = acc[:HEAD_DIM] / acc[HEAD_DIM:HEAD_DIM + 1]
    return jnp.concatenate([o_t[:, h * QB:(h + 1) * QB].T for h in range(NSA_HEADS)], axis=1)


ATTN_QBLOCKS = 2
ATTN_LANES = ATTN_QBLOCKS * NSA_HEADS * QB


def _block_bias(tbl_ref, bi0, first_tile, n_tiles):
    return jnp.concatenate([_lag_bias(tbl_ref, bi0 + g, first_tile, n_tiles) for g in range(ATTN_QBLOCKS)], axis=1)


def _nsa_attn_kernel(q_ref, nsel_ref, ocmp_ref, gc_ref, ks_ref, e_ref, vst_ref, kw_ref, vwt_ref,
                     tsel_ref, twin_ref, gexp_ref, o_ref, qaug_ref, lga_ref, lgb_ref, acc_ref):
    bi0 = pl.program_id(1) * ATTN_QBLOCKS
    blocks = [slice(g * QB, (g + 1) * QB) for g in range(ATTN_QBLOCKS)]
    qs = jnp.concatenate([_stack_heads(q_ref[rows, :]) for rows in blocks], axis=0)
    nsel = jnp.concatenate([nsel_ref[rows, :] for rows in blocks for _ in range(NSA_HEADS)], axis=0)
    qaug_ref[...] = jnp.concatenate([qs, nsel], axis=1)
    per = SEL_TK // QB
    n_tiles = ks_ref.shape[0] // SEL_TK

    def clamp(kt):
        return jnp.minimum(kt, n_tiles - 1)

    def sel_logits(kt, lg_ref):
        rows = pl.ds(pl.multiple_of(clamp(kt) * SEL_TK, SEL_TK), SEL_TK)
        k_aug = jnp.concatenate([ks_ref[rows, :], e_ref[rows, :]], axis=1)
        lg_ref[...] = _dot_nt(k_aug, qaug_ref[...]) + _block_bias(tsel_ref, bi0, kt * per, per)

    def softmax_pv(kt, lg_ref, m):
        ps, ms, alphas = [], [], []
        for c in range(ATTN_LANES // QB):
            cols = slice(c * QB, (c + 1) * QB)
            lg = lg_ref[:, cols]
            m_new = jnp.maximum(m[:, cols], jnp.max(lg, axis=0, keepdims=True))
            ms.append(m_new)
            alphas.append(jnp.exp2(m[:, cols] - m_new))
            ps.append(jnp.exp2(lg - m_new).astype(BF16))
        pv = _dot(_value_tiles(vst_ref, clamp(kt) * per, per), jnp.concatenate(ps, axis=1))
        acc_ref[...] = jnp.concatenate(alphas, axis=1) * acc_ref[...] + pv
        return jnp.concatenate(ms, axis=1)

    def trip(kt, m, n_pairs):
        for i in range(n_pairs):
            sel_logits(kt + 2 * i + 1, lgb_ref)
            m = softmax_pv(kt + 2 * i, lga_ref, m)
            sel_logits(kt + 2 * i + 2, lga_ref)
            m = softmax_pv(kt + 2 * i + 1, lgb_ref, m)
        return m

    acc_ref[...] = jnp.zeros(acc_ref.shape, F32)
    sel_logits(0, lga_ref)
    n_live = (bi0 + ATTN_QBLOCKS - 1) // per + 1
    n_long = n_live // (2 * SEL_LONG_PAIRS)
    m = lax.fori_loop(0, n_long, lambda j, m: trip(2 * SEL_LONG_PAIRS * j, m, SEL_LONG_PAIRS),
                      jnp.full((1, ATTN_LANES), NEG_INF, F32))
    first_short = 2 * SEL_LONG_PAIRS * n_long
    lax.fori_loop(0, (n_live - first_short + 1) // 2, lambda j, m: trip(first_short + 2 * j, m, 1), m)

    n_win = WIN_TILES + ATTN_QBLOCKS - 1
    first = jnp.maximum(bi0 - (WIN_TILES - 1), 0)
    r = pl.multiple_of(first * QB, QB)
    lg = _dot_nt(kw_ref[pl.ds(r, n_win * QB), :], qs) + _block_bias(twin_ref, bi0, first, n_win)
    p = jnp.exp2(lg - jnp.max(lg, axis=0, keepdims=True)).astype(BF16)
    acc_w = _dot(_value_tiles(vwt_ref, first, n_win), p)

    def finish(acc):
        w = NSA_HEADS * QB
        return jnp.concatenate([_finish_heads(acc[:, g * w:(g + 1) * w]) for g in range(ATTN_QBLOCKS)], axis=0)

    gates = _gate_lanes(gc_ref[...], gexp_ref[...])
    o_ref[...] = (ocmp_ref[...] + gates[:, NSA_WIDTH:2 * NSA_WIDTH] * finish(acc_ref[...])
                  + gates[:, 2 * NSA_WIDTH:] * finish(acc_w))


def _value_rows(v, bsz, seq):
    vt = v[:, :HEAD_DIM].reshape(bsz, seq // QB, QB, HEAD_DIM).transpose(0, 1, 3, 2)
    return jnp.concatenate([vt, jnp.ones(vt.shape[:2] + (V_ROWS - HEAD_DIM, QB), vt.dtype)], axis=2)


def _nsa_attn(qc, nsel, ocmp, gc, ks, e_pen, vs, kw, vw, tsel, twin, gexp, bsz, seq):
    n = qc.shape[0]
    nb = seq // QB
    steps = nb // ATTN_QBLOCKS
    tok = lambda w: pl.BlockSpec((ATTN_QBLOCKS * QB, w), lambda b, i: (b * steps + i, 0))
    seqblk = pl.BlockSpec((seq, LANES), lambda b, i: (b, 0))
    valblk = pl.BlockSpec((None, nb, V_ROWS, QB), lambda b, i: (b, 0, 0, 0))
    const = lambda a: pl.BlockSpec(a.shape, lambda b, i: (0,) * a.ndim)
    return pl.pallas_call(
        _nsa_attn_kernel,
        grid=(bsz, steps),
        in_specs=[tok(256), tok(LANES), tok(256), tok(LANES),
                  seqblk, const(e_pen), valblk, seqblk, valblk,
                  const(tsel), const(twin), const(gexp)],
        out_specs=tok(256),
        out_shape=jax.ShapeDtypeStruct((n, 256), F32),
        scratch_shapes=[pltpu.VMEM((ATTN_LANES, 2 * LANES), BF16), pltpu.VMEM((SEL_TK, ATTN_LANES), F32),
                        pltpu.VMEM((SEL_TK, ATTN_LANES), F32), pltpu.VMEM((V_ROWS, ATTN_LANES), F32)],
        compiler_params=_cparams("parallel", "arbitrary"),
        name="nsa_selected_window",
    )(qc, nsel, ocmp, gc, ks, e_pen, _value_rows(vs, bsz, seq), kw, _value_rows(vw, bsz, seq), tsel, twin, gexp)


def _rms(x, gain):
    return x * lax.rsqrt(jnp.mean(x * x, axis=-1, keepdims=True) + EPS) * gain


FFN_CHUNK = 1024


def _out_ffn_kernel(x_ref, oa_ref, y_ref, oc_ref, gluw_ref, glub_ref, gout_ref, wout_ref, g2_ref, wu_ref, wd_ref, o_ref):
    wb = y_ref.shape[1]
    ab = _dot(jax.nn.gelu(y_ref[...]).astype(BF16), gluw_ref[...]) + glub_ref[...]
    ob = ab[:, :wb] * jax.nn.sigmoid(ab[:, wb:])
    a0, a1 = SWA_WIDTH, SWA_WIDTH + wb
    mixed = jnp.concatenate([_rms(oa_ref[...], gout_ref[:, :a0]).astype(BF16),
                             _rms(ob, gout_ref[:, a0:a1]).astype(BF16),
                             _rms(oc_ref[...], gout_ref[:, a1:]).astype(BF16)], axis=1)
    x = x_ref[...] + _dot(mixed, wout_ref[...])
    h = _rms(x, g2_ref[...]).astype(BF16)
    acc = x
    for c in range(wu_ref.shape[1] // FFN_CHUNK):
        c0, c1 = c * FFN_CHUNK, (c + 1) * FFN_CHUNK
        hid = jnp.maximum(_dot(h, wu_ref[:, c0:c1]), 0.0)
        acc = acc + _dot((hid * hid).astype(BF16), wd_ref[c0:c1, :])
    o_ref[...] = acc


def _out_ffn(x2, oa, y, oc, gluw, glub, gout, wout, g2, wu, wd, tm):
    n, d = x2.shape
    tok = lambda w: pl.BlockSpec((tm, w), lambda i: (i, 0))
    full = lambda a: pl.BlockSpec(a.shape, lambda i: (0,) * a.ndim, pipeline_mode=pl.Buffered(1))
    return pl.pallas_call(
        _out_ffn_kernel,
        grid=(n // tm,),
        in_specs=[tok(d), tok(oa.shape[1]), tok(y.shape[1]), tok(oc.shape[1]),
                  full(gluw), full(glub), full(gout), full(wout), full(g2), full(wu), full(wd)],
        out_specs=tok(d),
        out_shape=jax.ShapeDtypeStruct((n, d), F32),
        compiler_params=_cparams("parallel"),
        name="out_proj_ffn",
    )(x2, oa, y, oc, gluw, glub, gout, wout, g2, wu, wd)


def _rel_bucket(dist):
    n = jnp.maximum(dist, 0)
    nf = jnp.maximum(n, 1).astype(F32)
    large = BUCKET_EXACT + (jnp.log(nf / BUCKET_EXACT) / math.log(BUCKET_MAX_DIST / BUCKET_EXACT)
                            * (NUM_BUCKETS - BUCKET_EXACT)).astype(jnp.int32)
    return jnp.where(n < BUCKET_EXACT, n, jnp.minimum(large, NUM_BUCKETS - 1))


def _bias_rows(tbl, dist, valid):
    onehot = jax.nn.one_hot(_rel_bucket(dist), NUM_BUCKETS, dtype=F32)
    b = jnp.einsum('...qkc,ch->...hqk', onehot, tbl.astype(F32), precision=HIGHEST)
    b = jnp.where(valid[..., None, :, :], b, NEG_INF)
    return b.reshape(b.shape[:-3] + (b.shape[-3] * b.shape[-2], b.shape[-1]))


def _bias_tables(rel_bias):
    swa_order = jnp.array([0, 2, 1, 3])
    tbl_a = rel_bias[:, :SWA_HEADS][:, swa_order]
    tbl_c = rel_bias[:, SWA_HEADS:]
    i = jnp.arange(QB)[:, None]
    d_swa = i - jnp.arange(2 * QB)[None, :] + QB
    bias_swa = _bias_rows(tbl_a, d_swa, (d_swa >= 0) & (d_swa < SWA_WINDOW))
    r = jnp.arange(CMP_PHASES)[:, None, None]
    c = jnp.arange(2 * LANES)[None, None, :]
    d_cmp = r * QB + i[None] - (c - LANES) * NSA_CMP_STRIDE - (NSA_CMP_BLOCK - 1)
    band_cmp = jnp.swapaxes(_bias_rows(tbl_c, d_cmp, d_cmp >= 0), 1, 2)
    j = jnp.arange(QB)[None, :]
    n_far = -(-(BUCKET_MAX_DIST + QB) // QB)
    d_sel = jnp.arange(n_far)[:, None, None] * QB + (i - j)[None]
    tsel = _bias_rows(tbl_c, d_sel, d_sel >= 0)
    d_win = jnp.arange(WIN_TILES)[:, None, None] * QB + (i - j)[None]
    twin = _bias_rows(tbl_c, d_win, (d_win >= 0) & (d_win < NSA_WINDOW))
    future = jnp.full((1,) + tsel.shape[1:], NEG_INF, F32)
    lag_table = lambda t: jnp.swapaxes(jnp.concatenate([future, t * LOG2E]), 1, 2)
    return bias_swa, band_cmp, lag_table(tsel), lag_table(jnp.concatenate([twin, future / LOG2E]))


def _nsa_constants(seq):
    m = seq // NSA_CMP_STRIDE
    cs = jnp.arange(m)[:, None] * NSA_CMP_STRIDE
    ss = jnp.arange(LANES)[None, :] * NSA_SEL_BLOCK
    ovl = ((cs < ss + NSA_SEL_BLOCK) & (cs + NSA_CMP_BLOCK > ss) & (cs < seq - NSA_CMP_STRIDE)).astype(BF16)
    key_blk = jnp.arange(seq)[:, None] // NSA_SEL_BLOCK
    e_pen = jnp.where(key_blk == jnp.arange(LANES)[None, :], SEL_PENALTY, 0.0).astype(BF16)
    col = jnp.arange(LANES)[:, None]
    lane = jnp.arange(NSA_BRANCHES * NSA_WIDTH)[None, :]
    br, hd = lane // NSA_WIDTH, (lane % NSA_WIDTH) // HEAD_DIM
    gexp = (col == hd * NSA_BRANCHES + br).astype(BF16)
    return ovl.T, e_pen, gexp


def _seg_matrix():
    r = jnp.arange(2 * LANES)
    return jnp.where((r[:, None] // HEAD_DIM) == (r[None, :] // HEAD_DIM), 1.0 / HEAD_DIM, 0.0).astype(BF16)


def _prep_w_in(w):
    d = w.shape[0]
    hd = HEAD_DIM
    ssm_w = d - SWA_WIDTH - NSA_WIDTH
    o_ka = SWA_WIDTH
    o_va = o_ka + SWA_KV_WIDTH
    o_u = o_va + SWA_KV_WIDTH
    o_qc = o_u + ssm_w
    o_kv = o_qc + NSA_WIDTH
    o_gc = o_kv + 6 * hd
    head = lambda off, h: w[:, off + h * hd: off + (h + 1) * hd]
    gates = w[:, o_gc:]
    cols = [head(0, 0), head(0, 2), head(0, 1), head(0, 3), w[:, o_ka:o_gc],
            gates, jnp.zeros((d, LANES - gates.shape[1]), w.dtype)]
    return jnp.concatenate(cols, axis=1).astype(BF16)


def _prep_qk_gains(qk_g):
    g = qk_g.astype(F32)
    t4 = lambda v: jnp.tile(v, 4)
    return jnp.stack([t4(g[0]), t4(g[1]), t4(g[2]), t4(g[4]), t4(g[5]),
                      t4(g[3]), t4(g[3]), t4(g[3])])


def kernel(x, norm1_g, w_in, qk_g, sinks, rel_bias, ssm_a_re, ssm_a_im, ssm_log_dt, ssm_b_re, ssm_b_im, ssm_c_re, ssm_c_im, ssm_d, glu_w, glu_b, cmp_pos, cmp_w1, cmp_w2, out_norm_g, w_out, norm2_g, w_up, w_down):
    bsz, seq, d = x.shape
    depth = w_in.shape[0]
    n = bsz * seq
    assert seq % SWA_TQ == 0 and seq // NSA_SEL_BLOCK <= LANES and d == 1024
    assert (seq // NSA_CMP_STRIDE) % LANES == 0 and seq % SEL_TK == 0
    assert seq >= (WIN_TILES + ATTN_QBLOCKS - 1) * QB and (seq // QB) % ATTN_QBLOCKS == 0
    tm = 512
    row = lambda v: v.astype(F32).reshape(1, -1)

    bias_swa, band_cmp, tsel, twin = _bias_tables(rel_bias)
    ovlt, e_pen, gexp = _nsa_constants(seq)
    seg = _seg_matrix()
    swa_order = jnp.array([0, 2, 1, 3])
    swa_cols = (swa_order[:, None] * HEAD_DIM + jnp.arange(HEAD_DIM)[None, :]).reshape(-1)

    qkg_all = jax.vmap(_prep_qk_gains)(qk_g)
    w_in_all = jax.vmap(_prep_w_in)(w_in)
    ssm_all = jax.vmap(_ssm_operators)(ssm_a_re, ssm_a_im, ssm_log_dt, ssm_b_re, ssm_b_im, ssm_c_re, ssm_c_im, ssm_d)
    cmp_all = jax.vmap(_prep_compress)(cmp_pos, cmp_w1, cmp_w2)
    sink_all = jnp.repeat(sinks.astype(F32)[:, swa_order], QB, axis=1)[..., None]
    gout_all = jnp.concatenate([out_norm_g[:, :SWA_WIDTH][:, swa_cols], out_norm_g[:, SWA_WIDTH:]], axis=1).astype(F32)
    w_out_all = jnp.concatenate([w_out[:, :SWA_WIDTH][:, swa_cols], w_out[:, SWA_WIDTH:]], axis=1).astype(BF16)
    glu_w_all, w_up_all, w_down_all = glu_w.astype(BF16), w_up.astype(BF16), w_down.astype(BF16)

    x2 = x.reshape(n, d)
    for l in range(depth):
        qkg = qkg_all[l]
        (qa, ka, va, u, qc, kvc, ks, vs, kw, vw, gc) = _in_proj(x2, row(norm1_g[l]), w_in_all[l], qkg, seg, tm)

        o_a = _swa(qa, ka, va, bias_swa, sink_all[l], bsz, seq)

        y = _s5(u, [op[l] for op in ssm_all], bsz, seq)

        pos, w1t, w1b, w2 = [c[l] for c in cmp_all]
        kk, vvt = _nsa_compress(kvc, pos, w1t, w1b, w2, qkg[5:6, :LANES], bsz, seq)
        o_cmp, nsel = _nsa_cmp(qc, kk, vvt, band_cmp, gc, ovlt, gexp, bsz, seq)
        o_c = _nsa_attn(qc, nsel, o_cmp, gc, ks, e_pen, vs, kw, vw, tsel, twin, gexp, bsz, seq)

        x2 = _out_ffn(x2, o_a, y, o_c, glu_w_all[l], row(glu_b[l]), gout_all[l:l + 1], w_out_all[l],
                      row(norm2_g[l]), w_up_all[l], w_down_all[l], tm)
    return x2.reshape(bsz, seq, d)
```
